```python
import math
import jax
import jax.numpy as jnp
from jax import lax
import numpy as np

D_MODEL = 1024
BATCH = 32
SEQ = 2048
DEPTH = 1
DEC_BATCH = 128
DEC_SEQ = 4
PAST_LEN = 8192
PAGE_SIZE = 128

HEAD_DIM = 64
ATT_HEADS = 8
ATT_KV_HEADS = 2
ATT_REP = ATT_HEADS // ATT_KV_HEADS
CMP_BLOCK = 32
CMP_STRIDE = 16
CMP_HIDDEN = 64
SEL_BLOCK = 64
SEL_TOPN = 16
WINDOW = 512
SEL_Q_BLOCK = 64
WIN_Q_BLOCK = 128
N_BUCKETS = 32
MAX_DISTANCE = 128
SSD_HEADS = 8
SSD_HEAD_DIM = 64
SSD_GROUPS = 2
SSD_REP = SSD_HEADS // SSD_GROUPS
SSD_STATE = 64
SSD_CONV = 4
SSD_CHUNK = 128
D_ATT = ATT_HEADS * HEAD_DIM
D_SSD = SSD_HEADS * SSD_HEAD_DIM
D_MIX = D_ATT + D_SSD
D_KV = ATT_KV_HEADS * HEAD_DIM
D_CONV = D_SSD + 2 * SSD_GROUPS * SSD_STATE
D_FF = ((8 * D_MODEL // 3 + 255) // 256) * 256
OFF_KV = D_ATT
OFF_GATE = OFF_KV + 6 * D_KV
OFF_Z = OFF_GATE + 3 * ATT_HEADS
OFF_XBC = OFF_Z + D_SSD
OFF_DT = OFF_XBC + D_CONV
D_IN = OFF_DT + SSD_HEADS
RMS_EPS = 1e-6
SEL_FORCE = 1e30

kernel_name = 'hymba_nsa_ssd_step'


def rmsnorm(x, g):
    xf = x.astype(jnp.float32)
    y = xf * lax.rsqrt(jnp.mean(xf * xf, axis=-1, keepdims=True) + RMS_EPS)
    return (y * g.astype(jnp.float32)).astype(x.dtype)


def rel_bucket(dist):
    n = jnp.maximum(dist, 0)
    max_exact = N_BUCKETS // 2
    nf = jnp.maximum(n, 1).astype(jnp.float32)
    large = max_exact + (jnp.log(nf / max_exact) / math.log(MAX_DISTANCE / max_exact)
                         * (N_BUCKETS - max_exact)).astype(jnp.int32)
    large = jnp.minimum(large, N_BUCKETS - 1)
    return jnp.where(n < max_exact, n, large)


def masked_softmax(s, mask):
    s = jnp.where(mask, s.astype(jnp.float32), -1e30)
    m = jnp.max(s, axis=-1, keepdims=True)
    e = jnp.where(mask, jnp.exp(s - m), 0.0)
    return e / jnp.maximum(jnp.sum(e, axis=-1, keepdims=True), 1e-30)


def attend_shared(q, q_pos, k, v, k_pos, mask, rel_bias):
    s = jnp.einsum('...qgrd,...kgd->...qgrk', q, k).astype(jnp.float32) * (HEAD_DIM ** -0.5)
    dist = q_pos[..., :, None] - k_pos[..., None, :]
    bias = rel_bias[rel_bucket(dist)]
    bias = bias.reshape(bias.shape[:-1] + (ATT_KV_HEADS, ATT_REP))
    bias = jnp.moveaxis(bias, -3, -1)
    p = masked_softmax(s + bias, mask[..., :, None, None, :])
    o = jnp.einsum('...qgrk,...kgd->...qgrd', p.astype(v.dtype), v)
    return o, p


def attend_gathered(q, q_pos, kg, vg, k_pos, rel_bias):
    s = jnp.einsum('bqgrd,bqgkd->bqgrk', q, kg).astype(jnp.float32) * (HEAD_DIM ** -0.5)
    dist = q_pos[None, :, None, None] - k_pos
    g_i = jnp.arange(ATT_KV_HEADS)[None, None, :, None]
    bias = rel_bias.reshape(N_BUCKETS, ATT_KV_HEADS, ATT_REP)[rel_bucket(dist), g_i]
    p = masked_softmax(s + jnp.swapaxes(bias, -1, -2), (dist >= 0)[:, :, :, None, :])
    return jnp.einsum('bqgrk,bqgkd->bqgrd', p.astype(vg.dtype), vg)


def compress(rows, pe, w1, w2):
    B, L = rows.shape[:2]
    n_sub = L // CMP_STRIDE
    n_part = CMP_BLOCK // CMP_STRIDE
    n_blk = n_sub - n_part + 1
    sub = rows[:, :n_sub * CMP_STRIDE].reshape(B, n_sub, CMP_STRIDE, ATT_KV_HEADS, HEAD_DIM)
    u = jnp.einsum('bnsgd,jsdh->bjngh', sub, w1.reshape(n_part, CMP_STRIDE, HEAD_DIM, CMP_HIDDEN))
    pre = jnp.einsum('sd,sdh->h', pe, w1)
    for j in range(n_part):
        pre = pre + u[:, j, j:j + n_blk]
    return jnp.einsum('bngh,hd->bngd', jax.nn.silu(pre), w2)


def cmp_branch(q, q_pos, kc_rows, vc_rows, W):
    kc = rmsnorm(compress(kc_rows, W['cmp_pe'][0], W['cmp_w1'][0], W['cmp_w2'][0]), W['k_norm'])
    vc = compress(vc_rows, W['cmp_pe'][1], W['cmp_w1'][1], W['cmp_w2'][1])
    e = CMP_STRIDE * jnp.arange(kc.shape[1], dtype=jnp.int32) + (CMP_BLOCK - 1)
    mask = e[None, :] <= q_pos[:, None]
    return attend_shared(q, q_pos, kc, vc, e, mask, W['rel_bias'])


def select_blocks(p_cmp, q_pos, n_sel, k_top):
    imp = jnp.sum(p_cmp, axis=3)
    ratio = SEL_BLOCK // CMP_STRIDE
    imp = jnp.concatenate([imp, jnp.zeros(imp.shape[:-1] + (1,), imp.dtype)], axis=-1)[..., :ratio * n_sel]
    grp = imp.reshape(imp.shape[:-1] + (n_sel, ratio))
    prev = jnp.concatenate([jnp.zeros(grp.shape[:-2] + (1,), grp.dtype), grp[..., :-1, ratio - 1]], axis=-1)
    score = jnp.sum(grp, axis=-1) + prev
    j = jnp.arange(n_sel, dtype=jnp.int32)
    cur = (q_pos // SEL_BLOCK)[None, :, None, None]
    forced = (j == 0) | (j == cur) | (j == cur - 1)
    score = jnp.where(forced, SEL_FORCE, jnp.where(j <= cur, score, -SEL_FORCE))
    _, idx = lax.top_k(score, k_top)
    return idx


def sel_branch_prompt(q, pos, idx, ks, vs, rel_bias):
    B, T = q.shape[:2]
    k_top = idx.shape[-1]
    kb = ks.reshape(B, T // SEL_BLOCK, SEL_BLOCK, ATT_KV_HEADS, HEAD_DIM)
    vb = vs.reshape(B, T // SEL_BLOCK, SEL_BLOCK, ATT_KV_HEADS, HEAD_DIM)
    b_i = jnp.arange(B)[:, None, None, None]
    g_i = jnp.arange(ATT_KV_HEADS)[None, None, :, None]
    offs = jnp.arange(SEL_BLOCK, dtype=jnp.int32)
    nqb = T // SEL_Q_BLOCK

    def one_block(args):
        qb, ib, pb = args
        kg = kb[b_i, ib, :, g_i].reshape(B, SEL_Q_BLOCK, ATT_KV_HEADS, k_top * SEL_BLOCK, HEAD_DIM)
        vg = vb[b_i, ib, :, g_i].reshape(B, SEL_Q_BLOCK, ATT_KV_HEADS, k_top * SEL_BLOCK, HEAD_DIM)
        kpos = (ib[..., None] * SEL_BLOCK + offs).reshape(B, SEL_Q_BLOCK, ATT_KV_HEADS, k_top * SEL_BLOCK)
        return attend_gathered(qb, pb, kg, vg, kpos, rel_bias)

    q_blocks = jnp.swapaxes(q.reshape(B, nqb, SEL_Q_BLOCK, ATT_KV_HEADS, ATT_REP, HEAD_DIM), 0, 1)
    i_blocks = jnp.swapaxes(idx.reshape(B, nqb, SEL_Q_BLOCK, ATT_KV_HEADS, k_top), 0, 1)
    p_blocks = pos.reshape(nqb, SEL_Q_BLOCK)
    o = lax.map(one_block, (q_blocks, i_blocks, p_blocks))
    return jnp.swapaxes(o, 0, 1).reshape(B, T, ATT_KV_HEADS, ATT_REP, HEAD_DIM)


def sel_branch_sample(q, pos, idx, ks, vs, c_ks, c_vs, page_table, rel_bias):
    B, S = q.shape[:2]
    k_top = idx.shape[-1]
    per_page = c_ks.shape[1] // SEL_BLOCK
    b_i = jnp.arange(B)[:, None, None, None]
    g_i = jnp.arange(ATT_KV_HEADS)[None, None, :, None]
    phys = page_table[b_i, idx // per_page]
    off = idx % per_page

    def gather(cache, new_rows):
        cr = cache.reshape(cache.shape[0], per_page, SEL_BLOCK, ATT_KV_HEADS, HEAD_DIM)
        past = cr[phys, off, :, g_i].reshape(B, S, ATT_KV_HEADS, k_top * SEL_BLOCK, HEAD_DIM)
        new = jnp.broadcast_to(jnp.swapaxes(new_rows, 1, 2)[:, None], (B, S, ATT_KV_HEADS, S, HEAD_DIM))
        return jnp.concatenate([past, new], axis=3)

    kpos_past = (idx[..., None] * SEL_BLOCK + jnp.arange(SEL_BLOCK, dtype=jnp.int32)).reshape(
        B, S, ATT_KV_HEADS, k_top * SEL_BLOCK)
    kpos_new = jnp.broadcast_to(pos, (B, S, ATT_KV_HEADS, S))
    kpos = jnp.concatenate([kpos_past, kpos_new], axis=3)
    return attend_gathered(q, pos, gather(c_ks, ks), gather(c_vs, vs), kpos, rel_bias)


def win_branch_prompt(q, pos, kw, vw, rel_bias):
    B, T = q.shape[:2]
    nq = T // WIN_Q_BLOCK
    nwb = WINDOW // WIN_Q_BLOCK

    def band(t):
        tp = jnp.concatenate([jnp.zeros((B, WINDOW, ATT_KV_HEADS, HEAD_DIM), t.dtype), t], axis=1)
        tp = tp.reshape(B, nq + nwb, WIN_Q_BLOCK, ATT_KV_HEADS, HEAD_DIM)
        return jnp.concatenate([tp[:, m:m + nq] for m in range(nwb + 1)], axis=2)

    qpos = pos.reshape(nq, WIN_Q_BLOCK)
    kpos = (jnp.arange(nq, dtype=jnp.int32)[:, None] * WIN_Q_BLOCK - WINDOW
            + jnp.arange((nwb + 1) * WIN_Q_BLOCK, dtype=jnp.int32)[None, :])
    dist = qpos[:, :, None] - kpos[:, None, :]
    mask = (dist >= 0) & (dist < WINDOW) & (kpos[:, None, :] >= 0)
    o, _ = attend_shared(q.reshape(B, nq, WIN_Q_BLOCK, ATT_KV_HEADS, ATT_REP, HEAD_DIM),
                         qpos, band(kw), band(vw), kpos, mask, rel_bias)
    return o.reshape(B, T, ATT_KV_HEADS, ATT_REP, HEAD_DIM)


def causal_conv(xbc, buf, w, b):
    L = xbc.shape[1]
    xp = jnp.concatenate([buf.astype(xbc.dtype), xbc], axis=1)
    out = b
    for k in range(SSD_CONV):
        out = out + xp[:, k:k + L] * w[k]
    return jax.nn.silu(out), xp[:, L:]


def ssd_scan(x, dt, A, Bm, Cm, h0, chunk):
    b, L = x.shape[:2]
    c = L // chunk
    xr = x.reshape(b, c, chunk, SSD_GROUPS, SSD_REP, SSD_HEAD_DIM)
    dtr = dt.reshape(b, c, chunk, SSD_GROUPS, SSD_REP)
    Br = Bm.reshape(b, c, chunk, SSD_GROUPS, SSD_STATE)
    Cr = Cm.reshape(b, c, chunk, SSD_GROUPS, SSD_STATE)
    acum = jnp.cumsum(dtr * A, axis=2)
    xdt = xr * dtr[..., None]
    causal = jnp.tril(jnp.ones((chunk, chunk), bool))[:, :, None, None]
    seg = acum[:, :, :, None] - acum[:, :, None, :]
    decay = jnp.where(causal, jnp.exp(jnp.where(causal, seg, 0.0)), 0.0)
    scores = jnp.einsum('bclgn,bcsgn->bclsg', Cr, Br)[..., None] * decay
    y_diag = jnp.einsum('bclsgr,bcsgrp->bclgrp', scores, xdt)
    to_end = jnp.exp(acum[:, :, -1:] - acum)
    states = jnp.einsum('bcsgn,bcsgr,bcsgrp->bcgrpn', Br, to_end, xdt)
    chunk_decay = jnp.exp(acum[:, :, -1])

    def step(h, inp):
        st, dec = inp
        return h * dec[..., None, None] + st, h

    h_last, h_in = lax.scan(step, h0, (jnp.moveaxis(states, 1, 0), jnp.moveaxis(chunk_decay, 1, 0)))
    h_in = jnp.moveaxis(h_in, 0, 1)
    y_off = jnp.einsum('bclgn,bcgrpn,bclgr->bclgrp', Cr, h_in, jnp.exp(acum))
    return (y_diag + y_off).reshape(b, L, SSD_GROUPS, SSD_REP, SSD_HEAD_DIM), h_last


def ssd_mixer(z, xbc, dt_raw, conv_buf, h0, W, chunk):
    B, L = z.shape[:2]
    f = jnp.float32
    xbc, new_buf = causal_conv(xbc, conv_buf, W['conv_w'], W['conv_b'])
    xs = xbc[..., :D_SSD].astype(f).reshape(B, L, SSD_GROUPS, SSD_REP, SSD_HEAD_DIM)
    Bm = xbc[..., D_SSD:D_SSD + SSD_GROUPS * SSD_STATE].astype(f).reshape(B, L, SSD_GROUPS, SSD_STATE)
    Cm = xbc[..., D_SSD + SSD_GROUPS * SSD_STATE:].astype(f).reshape(B, L, SSD_GROUPS, SSD_STATE)
    dt = jax.nn.softplus(dt_raw.astype(f) + W['dt_bias'].astype(f)).reshape(B, L, SSD_GROUPS, SSD_REP)
    A = -jnp.exp(W['a_log'].astype(f)).reshape(SSD_GROUPS, SSD_REP)
    h0r = h0.astype(f).reshape(B, SSD_GROUPS, SSD_REP, SSD_HEAD_DIM, SSD_STATE)
    y, h = ssd_scan(xs, dt, A, Bm, Cm, h0r, chunk)
    y = y + W['d_skip'].astype(f).reshape(SSD_GROUPS, SSD_REP)[..., None] * xs
    y = y.reshape(B, L, D_SSD) * jax.nn.silu(z.astype(f))
    y = rmsnorm(y.reshape(B, L, SSD_GROUPS, D_SSD // SSD_GROUPS),
                W['ssd_norm'].reshape(SSD_GROUPS, D_SSD // SSD_GROUPS)).reshape(B, L, D_SSD)
    return y.astype(z.dtype), new_buf, h.reshape(B, SSD_HEADS, SSD_HEAD_DIM, SSD_STATE).astype(h0.dtype)


def project(x, W):
    B, L = x.shape[:2]
    p = rmsnorm(x, W['norm_mix']) @ W['w_in']
    q, kv, gate, z, xbc, dt_raw = jnp.split(p, [OFF_KV, OFF_GATE, OFF_Z, OFF_XBC, OFF_DT], axis=-1)
    q = rmsnorm(q.reshape(B, L, ATT_KV_HEADS, ATT_REP, HEAD_DIM), W['q_norm'])
    kc, vc, ks, vs, kw, vw = [t.reshape(B, L, ATT_KV_HEADS, HEAD_DIM) for t in jnp.split(kv, 6, axis=-1)]
    ks = rmsnorm(ks, W['k_norm'])
    kw = rmsnorm(kw, W['k_norm'])
    gate = jax.nn.sigmoid(gate.astype(jnp.float32)).reshape(B, L, ATT_KV_HEADS, ATT_REP, 3)
    return q, kc, vc, ks, vs, kw, vw, gate, z, xbc, dt_raw


def combine(gate, o_cmp, o_sel, o_win):
    return gate[..., 0:1] * o_cmp + gate[..., 1:2] * o_sel + gate[..., 2:3] * o_win


def finish(x, o_att, y_ssd, W):
    B, L = x.shape[:2]
    mix = jnp.concatenate([o_att.reshape(B, L, D_ATT).astype(x.dtype), y_ssd.astype(x.dtype)], axis=-1)
    h = x + mix @ W['w_out']
    u = rmsnorm(h, W['norm_ffn'])
    return h + (jax.nn.silu(u @ W['w_gate']) * (u @ W['w_up'])) @ W['w_down']


def prompt_layer(x, W):
    B, T = x.shape[:2]
    pos = jnp.arange(T, dtype=jnp.int32)
    q, kc, vc, ks, vs, kw, vw, gate, z, xbc, dt_raw = project(x, W)
    o_cmp, p_cmp = cmp_branch(q, pos, kc, vc, W)
    n_sel = T // SEL_BLOCK
    idx = select_blocks(p_cmp, pos, n_sel, min(SEL_TOPN, n_sel))
    o_sel = sel_branch_prompt(q, pos, idx, ks, vs, W['rel_bias'])
    o_win = win_branch_prompt(q, pos, kw, vw, W['rel_bias'])
    o_att = combine(gate, o_cmp, o_sel, o_win)
    conv0 = jnp.zeros((B, SSD_CONV - 1, D_CONV), x.dtype)
    h0 = jnp.zeros((B, SSD_HEADS, SSD_HEAD_DIM, SSD_STATE), jnp.float32)
    y_ssd, conv_state, ssm_state = ssd_mixer(z, xbc, dt_raw, conv0, h0, W, min(SSD_CHUNK, T))
    y = finish(x, o_att, y_ssd, W)
    wb = min(WINDOW, T)
    return y, (kc, vc, ks, vs, kw[:, T - wb:], vw[:, T - wb:], conv_state, ssm_state)


def sample_layer(x, c_kc, c_vc, c_ks, c_vs, c_kw, c_vw, s_conv, s_ssm, page_table, W):
    B, S = x.shape[:2]
    past = page_table.shape[1] * c_kc.shape[1]
    pos = past + jnp.arange(S, dtype=jnp.int32)
    q, kc, vc, ks, vs, kw, vw, gate, z, xbc, dt_raw = project(x, W)

    def gather_all(cache):
        return cache[page_table].reshape(B, past, ATT_KV_HEADS, HEAD_DIM)

    o_cmp, p_cmp = cmp_branch(q, pos, jnp.concatenate([gather_all(c_kc), kc], axis=1),
                              jnp.concatenate([gather_all(c_vc), vc], axis=1), W)
    n_sel = past // SEL_BLOCK
    idx = select_blocks(p_cmp, pos, n_sel, min(SEL_TOPN - 1, n_sel))
    o_sel = sel_branch_sample(q, pos, idx, ks, vs, c_ks, c_vs, page_table, W['rel_bias'])
    lb = c_kw.shape[1]
    kwin = jnp.concatenate([c_kw, kw], axis=1)
    vwin = jnp.concatenate([c_vw, vw], axis=1)
    kpos = past - lb + jnp.arange(lb + S, dtype=jnp.int32)
    dist = pos[:, None] - kpos[None, :]
    o_win, _ = attend_shared(q, pos, kwin, vwin, kpos, (dist >= 0) & (dist < WINDOW), W['rel_bias'])
    o_att = combine(gate, o_cmp, o_sel, o_win)
    y_ssd, conv_state, ssm_state = ssd_mixer(z, xbc, dt_raw, s_conv, s_ssm, W, S)
    y = finish(x, o_att, y_ssd, W)
    return y, (kc, vc, ks, vs, kwin[:, S:], vwin[:, S:], conv_state, ssm_state)


def setup_inputs(seed: int = 0) -> dict:
    key = jax.random.key(seed)
    ks = jax.random.split(key, 32)
    f32 = jnp.float32

    def nrm(k, shape, scale):
        return jax.random.normal(k, shape, f32) * scale

    n_pages = PAST_LEN // PAGE_SIZE
    n_phys = (DEC_BATCH * n_pages * 5) // 4
    win_buf = min(WINDOW, PAST_LEN)
    page_table = jax.random.permutation(ks[0], n_phys)[:DEC_BATCH * n_pages].reshape(
        DEC_BATCH, n_pages).astype(jnp.int32)
    kv_page = (DEPTH, n_phys, PAGE_SIZE, ATT_KV_HEADS, HEAD_DIM)
    kv_win = (DEPTH, DEC_BATCH, win_buf, ATT_KV_HEADS, HEAD_DIM)
    dt0 = jnp.exp(jax.random.uniform(ks[1], (DEPTH, SSD_HEADS), f32, math.log(1e-3), math.log(1e-1)))
    return {
        'x_prompt': nrm(ks[2], (BATCH, SEQ, D_MODEL), 1.0),
        'x_sample': nrm(ks[3], (DEC_BATCH, DEC_SEQ, D_MODEL), 1.0),
        'cache_k_cmp': nrm(ks[4], kv_page, 1.0),
        'cache_v_cmp': nrm(ks[5], kv_page, 1.0),
        'cache_k_sel': nrm(ks[6], kv_page, 1.0),
        'cache_v_sel': nrm(ks[7], kv_page, 1.0),
        'cache_k_win': nrm(ks[8], kv_win, 1.0),
        'cache_v_win': nrm(ks[9], kv_win, 1.0),
        'state_conv': nrm(ks[10], (DEPTH, DEC_BATCH, SSD_CONV - 1, D_CONV), 1.0),
        'state_ssm': nrm(ks[11], (DEPTH, DEC_BATCH, SSD_HEADS, SSD_HEAD_DIM, SSD_STATE), 0.5),
        'page_table': page_table,
        'norm_mix': 1.0 + nrm(ks[12], (DEPTH, D_MODEL), 0.02),
        'w_in': nrm(ks[13], (DEPTH, D_MODEL, D_IN), D_MODEL ** -0.5),
        'q_norm': 1.0 + nrm(ks[14], (DEPTH, HEAD_DIM), 0.02),
        'k_norm': 1.0 + nrm(ks[15], (DEPTH, HEAD_DIM), 0.02),
        'cmp_pe': nrm(ks[16], (DEPTH, 2, CMP_BLOCK, HEAD_DIM), 0.1),
        'cmp_w1': nrm(ks[17], (DEPTH, 2, CMP_BLOCK, HEAD_DIM, CMP_HIDDEN), (CMP_BLOCK * HEAD_DIM) ** -0.5),
        'cmp_w2': nrm(ks[18], (DEPTH, 2, CMP_HIDDEN, HEAD_DIM), CMP_HIDDEN ** -0.5),
        'rel_bias': nrm(ks[19], (N_BUCKETS, ATT_HEADS), 0.3),
        'conv_w': nrm(ks[20], (DEPTH, SSD_CONV, D_CONV), SSD_CONV ** -0.5),
        'conv_b': nrm(ks[21], (DEPTH, D_CONV), 0.02),
        'dt_bias': dt0 + jnp.log(-jnp.expm1(-dt0)),
        'a_log': jnp.log(jax.random.uniform(ks[22], (DEPTH, SSD_HEADS), f32, 1.0, 16.0)),
        'd_skip': 1.0 + nrm(ks[23], (DEPTH, SSD_HEADS), 0.02),
        'ssd_norm': 1.0 + nrm(ks[24], (DEPTH, D_SSD), 0.02),
        'w_out': nrm(ks[25], (DEPTH, D_MIX, D_MODEL), D_MIX ** -0.5),
        'norm_ffn': 1.0 + nrm(ks[26], (DEPTH, D_MODEL), 0.02),
        'w_gate': nrm(ks[27], (DEPTH, D_MODEL, D_FF), D_MODEL ** -0.5),
        'w_up': nrm(ks[28], (DEPTH, D_MODEL, D_FF), D_MODEL ** -0.5),
        'w_down': nrm(ks[29], (DEPTH, D_FF, D_MODEL), D_FF ** -0.5),
    }


def reference(x_prompt, x_sample, cache_k_cmp, cache_v_cmp, cache_k_sel, cache_v_sel,
              cache_k_win, cache_v_win, state_conv, state_ssm, page_table,
              norm_mix, w_in, q_norm, k_norm, cmp_pe, cmp_w1, cmp_w2, rel_bias,
              conv_w, conv_b, dt_bias, a_log, d_skip, ssd_norm, w_out, norm_ffn,
              w_gate, w_up, w_down):
    y_prompt, y_sample = x_prompt, x_sample
    prompt_states, sample_states = [], []
    for l in range(DEPTH):
        W = dict(norm_mix=norm_mix[l], w_in=w_in[l], q_norm=q_norm[l], k_norm=k_norm[l],
                 cmp_pe=cmp_pe[l], cmp_w1=cmp_w1[l], cmp_w2=cmp_w2[l], rel_bias=rel_bias,
                 conv_w=conv_w[l], conv_b=conv_b[l], dt_bias=dt_bias[l], a_log=a_log[l],
                 d_skip=d_skip[l], ssd_norm=ssd_norm[l], w_out=w_out[l], norm_ffn=norm_ffn[l],
                 w_gate=w_gate[l], w_up=w_up[l], w_down=w_down[l])
        y_prompt, st_p = prompt_layer(y_prompt, W)
        y_sample, st_s = sample_layer(y_sample, cache_k_cmp[l], cache_v_cmp[l], cache_k_sel[l],
                                      cache_v_sel[l], cache_k_win[l], cache_v_win[l],
                                      state_conv[l], state_ssm[l], page_table, W)
        prompt_states.append(st_p)
        sample_states.append(st_s)
    p_k_cmp, p_v_cmp, p_k_sel, p_v_sel, p_k_win, p_v_win, p_conv, p_ssm = [
        jnp.stack(a) for a in zip(*prompt_states)]
    s_k_cmp, s_v_cmp, s_k_sel, s_v_sel, s_k_win, s_v_win, s_conv, s_ssm = [
        jnp.stack(a) for a in zip(*sample_states)]
    return (y_prompt, y_sample,
            p_k_cmp, p_v_cmp, p_k_sel, p_v_sel, p_k_win, p_v_win, p_conv, p_ssm,
            s_k_cmp, s_v_cmp, s_k_sel, s_v_sel, s_k_win, s_v_win, s_conv, s_ssm)
```

```python
import functools
import math

import numpy as np
import jax
import jax.numpy as jnp
from jax import lax
from jax.experimental import pallas as pl
from jax.experimental.pallas import tpu as pltpu

F32 = jnp.float32
BF16 = jnp.bfloat16

D_MODEL = 1024
HEAD_DIM = 64
ATT_HEADS = 8
ATT_KV_HEADS = 2
ATT_REP = ATT_HEADS // ATT_KV_HEADS
CMP_BLOCK = 32
CMP_STRIDE = 16
CMP_HIDDEN = 64
SEL_BLOCK = 64
SEL_TOPN = 16
WINDOW = 512
N_BUCKETS = 32
MAX_DISTANCE = 128
PAGE_SIZE = 128
SSD_HEADS = 8
SSD_STATE = 64
SSD_CONV = 4
SSD_CHUNK = 128
D_ATT = ATT_HEADS * HEAD_DIM
D_SSD = SSD_HEADS * 64
D_KV = ATT_KV_HEADS * HEAD_DIM
D_CONV = D_SSD + 2 * 2 * SSD_STATE
D_FF = ((8 * D_MODEL // 3 + 255) // 256) * 256
OFF_KV = D_ATT
OFF_GATE = OFF_KV + 6 * D_KV
OFF_Z = OFF_GATE + 3 * ATT_HEADS
OFF_XBC = OFF_Z + D_SSD
OFF_DT = OFF_XBC + D_CONV
D_IN = OFF_DT + SSD_HEADS
RMS_EPS = 1e-6
NEG = -1e30

C_Q, C_KV, C_Z, C_XBC, C_GD = 0, 512, 1280, 1792, 2560
D_INR = 2688
DT_LANE = 24
LANES = 128
TQ = 256
TQS = 16
VMEM_LIMIT = 48 * 1024 * 1024


def _dot(a, b):
    return jnp.dot(a, b, preferred_element_type=F32)


def _dot_nt(a, b):
    return lax.dot_general(a, b, (((1,), (1,)), ((), ())), preferred_element_type=F32)


def _split_dot(a, b, parts):
    acc = None
    rem = a
    for _ in range(parts):
        piece = rem.astype(BF16)
        rem = rem - piece.astype(F32)
        t = _dot(piece, b)
        acc = t if acc is None else acc + t
    return acc


def _silu(x):
    return x * (1.0 / (1.0 + jnp.exp(-x)))


def _params(sem=None):
    kw = dict(vmem_limit_bytes=VMEM_LIMIT)
    if sem is not None:
        kw["dimension_semantics"] = sem
    return pltpu.CompilerParams(**kw)


def _proj_kernel(x_ref, gmix_ref, w_ref, qg_ref, kg_ref, bq_ref, bk_ref,
                 q_ref, kc_ref, vc_ref, ks_ref, vs_ref, kw_ref, vw_ref,
                 ksa_ref, vsb_ref, kwb_ref, vwb_ref, z_ref, xbc_ref, gd_ref, *, tm, t_len):
    x = x_ref[...]
    ms = jnp.mean(x * x, axis=-1, keepdims=True)
    u = (x * lax.rsqrt(ms + RMS_EPS) * gmix_ref[...]).astype(BF16)

    def proj(lo, hi):
        return _dot(u, w_ref[:, lo:hi])

    def headnorm(v, b_ref, g_ref):
        msq = _dot((v * v).astype(BF16), b_ref[...])
        return v * lax.rsqrt(msq + RMS_EPS) * g_ref[...]

    q_ref[...] = headnorm(proj(C_Q, C_Q + 512), bq_ref, qg_ref).astype(BF16)
    kc_ref[...] = proj(C_KV, C_KV + 128)
    vc_ref[...] = proj(C_KV + 128, C_KV + 256)
    ks = headnorm(proj(C_KV + 256, C_KV + 384), bk_ref, kg_ref)
    ks_ref[...] = ks
    vs = proj(C_KV + 384, C_KV + 512)
    vs_ref[...] = vs
    vsb_ref[...] = vs.astype(BF16)
    kw = headnorm(proj(C_KV + 512, C_KV + 640), bk_ref, kg_ref)
    kw_ref[...] = kw
    kwb_ref[...] = kw.astype(BF16)
    vw = proj(C_KV + 640, C_KV + 768)
    vw_ref[...] = vw
    vwb_ref[...] = vw.astype(BF16)
    row = pl.program_id(0) * tm + lax.broadcasted_iota(jnp.int32, (tm, LANES), 0)
    blk = (row % t_len) // SEL_BLOCK
    lane = lax.broadcasted_iota(jnp.int32, (tm, LANES), 1)
    ksa_ref[:, 0:128] = ks.astype(BF16)
    ksa_ref[:, 128:256] = jnp.where(lane == blk, 1.0, 0.0).astype(BF16)
    z_ref[...] = proj(C_Z, C_Z + 512)
    xbc_ref[...] = proj(C_XBC, C_XBC + 768)
    gd_ref[...] = proj(C_GD, C_GD + 128)


def _project(x2d, wts, t_len):
    n = x2d.shape[0]
    tm = min(512, n)
    row = lambda w: pl.BlockSpec((tm, w), lambda i: (i, 0))
    full = lambda a: pl.BlockSpec(a.shape, lambda i: (0,) * a.ndim)
    ins = [x2d, wts["gmix"], wts["w_in"], wts["qg"], wts["kg"], wts["bq"], wts["bk"]]
    out_w = [(512, BF16)] + [(128, F32)] * 6 + [(256, BF16)] + [(128, BF16)] * 3 + [(512, F32), (768, F32), (128, F32)]
    return pl.pallas_call(
        functools.partial(_proj_kernel, tm=tm, t_len=t_len),
        grid=(n // tm,),
        in_specs=[row(D_MODEL)] + [full(a) for a in ins[1:]],
        out_specs=[row(w) for w, _ in out_w],
        out_shape=[jax.ShapeDtypeStruct((n, w), d) for w, d in out_w],
        compiler_params=_params(("parallel",)),
        name="proj",
    )(*ins)


def _bucket_np(dist):
    n = np.maximum(dist, 0)
    max_exact = N_BUCKETS // 2
    nf = np.maximum(n, 1).astype(np.float64)
    large = max_exact + (np.log(nf / max_exact) / math.log(MAX_DISTANCE / max_exact)
                         * (N_BUCKETS - max_exact)).astype(np.int64)
    large = np.minimum(large, N_BUCKETS - 1)
    return np.where(n < max_exact, n, large).astype(np.int32)


def _idx_table(dist, valid):
    return np.where(valid, _bucket_np(dist), -1).astype(np.int32)


def _table_kernel(rb_ref, idx_ref, out_ref):
    h = pl.program_id(0) * ATT_REP + pl.program_id(2)
    idx = idx_ref[0]
    far = rb_ref[N_BUCKETS - 1, h]
    acc = jnp.zeros(idx.shape, F32)
    for b in range(N_BUCKETS - 1):
        acc = jnp.where(idx == b, rb_ref[b, h] - far, acc)
    out_ref[0, 0, 0] = jnp.where(idx < 0, NEG, acc)


def _bias_tables(rel_bias, idx):
    k, r, c = idx.shape
    out = pl.pallas_call(
        _table_kernel,
        grid=(ATT_KV_HEADS, k, ATT_REP),
        in_specs=[pl.BlockSpec(memory_space=pltpu.SMEM),
                  pl.BlockSpec((1, r, c), lambda g, kk, rr: (kk, 0, 0))],
        out_specs=pl.BlockSpec((1, 1, 1, r, c), lambda g, kk, rr: (g, kk, rr, 0, 0)),
        out_shape=jax.ShapeDtypeStruct((ATT_KV_HEADS, k, ATT_REP, r, c), F32),
        name="bias_table",
    )(rel_bias, jnp.asarray(idx))
    return out.reshape(ATT_KV_HEADS, k, ATT_REP * r, c)


def _compress_core(x, w1, pe, w2):
    n_sub = x.shape[0]
    u = _dot(x.astype(BF16), w1)
    upe = _dot(pe, w1)
    nxt = pltpu.roll(u[:, 128:256], n_sub - 1, 0)
    pre = u[:, 0:128] + nxt + upe[0:1, 0:128] + upe[1:2, 128:256]
    return _dot(_silu(pre).astype(BF16), w2)


def _knorm(v, bk, kg):
    msq = _dot((v * v).astype(BF16), bk)
    return v * lax.rsqrt(msq + RMS_EPS) * kg


def _compress_prompt_kernel(kc_ref, vc_ref, w1k_ref, w1v_ref, pek_ref, pev_ref, w2k_ref, w2v_ref,
                            kg_ref, bk_ref, ko_ref, vo_ref):
    kc = _compress_core(kc_ref[0], w1k_ref[...], pek_ref[...], w2k_ref[...])
    ko_ref[0] = _knorm(kc, bk_ref[...], kg_ref[...]).astype(BF16)
    vo_ref[0] = _compress_core(vc_ref[0], w1v_ref[...], pev_ref[...], w2v_ref[...]).astype(BF16)


def _compress_prompt(kc, vc, wts):
    b, n_sub, w = kc.shape
    full = lambda a: pl.BlockSpec(a.shape, lambda i: (0,) * a.ndim)
    consts = [wts["w1k"], wts["w1v"], wts["pek"], wts["pev"], wts["w2k"], wts["w2v"], wts["kg"], wts["bk"]]
    blk = pl.BlockSpec((1, n_sub, w), lambda i: (i, 0, 0))
    oblk = pl.BlockSpec((1, n_sub, 128), lambda i: (i, 0, 0))
    return pl.pallas_call(
        _compress_prompt_kernel,
        grid=(b,),
        in_specs=[blk, blk] + [full(a) for a in consts],
        out_specs=[oblk, oblk],
        out_shape=[jax.ShapeDtypeStruct((b, n_sub, 128), BF16)] * 2,
        compiler_params=_params(("parallel",)),
        name="compress_prompt",
    )(kc, vc, *consts)


def _page_copy(cache_ref, page, buf_ref, slot, p, rows, sem_ref):
    return pltpu.make_async_copy(cache_ref.at[page], buf_ref.at[slot, pl.ds(p * rows, rows)], sem_ref.at[slot])


def _gather_start(pt_ref, b, slot, caches, bufs, sems, n_pages, rows):
    def body(p, carry):
        page = pt_ref[b, p]
        for cache_ref, buf_ref, sem_ref in zip(caches, bufs, sems):
            _page_copy(cache_ref, page, buf_ref, slot, p, rows, sem_ref).start()
        return carry
    lax.fori_loop(0, n_pages, body, 0)


def _gather_wait(slot, caches, bufs, sems, n_pages, rows):
    def body(p, carry):
        for cache_ref, buf_ref, sem_ref in zip(caches, bufs, sems):
            _page_copy(cache_ref, 0, buf_ref, slot, p, rows, sem_ref).wait()
        return carry
    lax.fori_loop(0, n_pages, body, 0)


def _gather_pipeline(pt_ref, caches, bufs, sems, n_pages, rows):
    b = pl.program_id(0)
    nb = pl.num_programs(0)
    slot = b % 2

    @pl.when(b == 0)
    def _():
        _gather_start(pt_ref, 0, 0, caches, bufs, sems, n_pages, rows)

    @pl.when(b + 1 < nb)
    def _():
        _gather_start(pt_ref, b + 1, 1 - slot, caches, bufs, sems, n_pages, rows)

    _gather_wait(slot, caches, bufs, sems, n_pages, rows)
    return slot


def _compress_sample_kernel(pt_ref, ck_ref, cv_ref, w1k_ref, w1v_ref, pek_ref, pev_ref, w2k_ref, w2v_ref,
                            kg_ref, bk_ref, ko_ref, vo_ref, kbuf, vbuf, ksem, vsem, *, n_pages):
    rows = PAGE_SIZE // CMP_STRIDE
    slot = _gather_pipeline(pt_ref, (ck_ref, cv_ref), (kbuf, vbuf), (ksem, vsem), n_pages, rows)
    kc = _compress_core(kbuf[slot], w1k_ref[...], pek_ref[...], w2k_ref[...])
    ko_ref[0] = _knorm(kc, bk_ref[...], kg_ref[...]).astype(BF16)
    vo_ref[0] = _compress_core(vbuf[slot], w1v_ref[...], pev_ref[...], w2v_ref[...]).astype(BF16)


def _compress_sample(page_table, ck, cv, wts):
    bs, n_pages = page_table.shape
    rows = PAGE_SIZE // CMP_STRIDE
    n_sub = n_pages * rows
    w = ck.shape[-1]
    full = lambda a: pl.BlockSpec(a.shape, lambda i, pt: (0,) * a.ndim)
    consts = [wts["w1k"], wts["w1v"], wts["pek"], wts["pev"], wts["w2k"], wts["w2v"], wts["kg"], wts["bk"]]
    anyspec = pl.BlockSpec(memory_space=pl.ANY)
    oblk = pl.BlockSpec((1, n_sub, 128), lambda i, pt: (i, 0, 0))
    return pl.pallas_call(
        functools.partial(_compress_sample_kernel, n_pages=n_pages),
        grid_spec=pltpu.PrefetchScalarGridSpec(
            num_scalar_prefetch=1,
            grid=(bs,),
            in_specs=[anyspec, anyspec] + [full(a) for a in consts],
            out_specs=[oblk, oblk],
            scratch_shapes=[pltpu.VMEM((2, n_sub, w), F32), pltpu.VMEM((2, n_sub, w), F32),
                            pltpu.SemaphoreType.DMA((2,)), pltpu.SemaphoreType.DMA((2,))]),
        out_shape=[jax.ShapeDtypeStruct((bs, n_sub, 128), BF16)] * 2,
        compiler_params=_params(("arbitrary",)),
        name="compress_sample",
    )(page_table, ck, cv, *consts)


def _group_queries(q, g, tq):
    lane = lax.broadcasted_iota(jnp.int32, (tq, LANES), 1)
    mine = (lane >= g * HEAD_DIM) & (lane < (g + 1) * HEAD_DIM)
    zero = jnp.zeros((tq, LANES), q.dtype)
    return jnp.concatenate([jnp.where(mine, q[:, r * 128:(r + 1) * 128], zero) for r in range(ATT_REP)], axis=0)


def _pack_heads(o, g, tq):
    lane = lax.broadcasted_iota(jnp.int32, (tq, LANES), 1)
    first = g == 0
    chunks = []
    for k in range(ATT_REP // 2):
        a = o[2 * k * tq:(2 * k + 1) * tq]
        b = o[(2 * k + 1) * tq:(2 * k + 2) * tq]
        lo = jnp.where(first, a, pltpu.roll(a, HEAD_DIM, 1))
        hi = jnp.where(first, pltpu.roll(b, HEAD_DIM, 1), b)
        chunks.append(jnp.where(lane < HEAD_DIM, lo, hi))
    return jnp.concatenate(chunks, axis=1)


def _cmp_select_kernel(q_ref, kc_ref, vc_ref, tab_ref, msel_ref, ovr_ref, o_ref, pen_ref, *, tq, n_sel, k_top):
    q = q_ref[0]
    kc = kc_ref[0]
    vc = vc_ref[0]
    ovr = ovr_ref[0]
    lane = lax.broadcasted_iota(jnp.int32, (tq, LANES), 1)
    outs = []
    for g in range(ATT_KV_HEADS):
        tab = tab_ref[g, 0]
        s = _dot_nt(_group_queries(q, g, tq), kc) + tab
        m = jnp.max(s, axis=-1, keepdims=True)
        e = jnp.where(tab > 0.5 * NEG, jnp.exp(s - m), 0.0)
        p = e / jnp.maximum(jnp.sum(e, axis=-1, keepdims=True), 1e-30)
        outs.append(_dot(p.astype(BF16), vc))
        imp = p[0:tq]
        for r in range(1, ATT_REP):
            imp = imp + p[r * tq:(r + 1) * tq]
        score = _split_dot(imp, msel_ref[...], 3)
        score = jnp.where(ovr == 0.0, score, ovr)
        rank = jnp.zeros((tq, LANES), F32)
        for j in range(n_sel):
            col = score[:, j:j + 1]
            beats = (col > score) | ((col == score) & (lane > j))
            rank = rank + jnp.where(beats, 1.0, 0.0)
        pen_ref[0, :, g * 128:(g + 1) * 128] = jnp.where(rank < k_top, 0.0, NEG).astype(BF16)
    for g in range(ATT_KV_HEADS):
        o_ref[0, :, g * 256:(g + 1) * 256] = _pack_heads(outs[g], g, tq)


def _cmp_select(q, kc, vc, tab, msel, ovr, *, tq, n_sel, k_top):
    b, t, _ = q.shape
    nb = kc.shape[1]
    n_qt = t // tq
    return pl.pallas_call(
        functools.partial(_cmp_select_kernel, tq=tq, n_sel=n_sel, k_top=k_top),
        grid=(n_qt, b),
        in_specs=[pl.BlockSpec((1, tq, 512), lambda qi, bi: (bi, qi, 0)),
                  pl.BlockSpec((1, nb, 128), lambda qi, bi: (bi, 0, 0)),
                  pl.BlockSpec((1, nb, 128), lambda qi, bi: (bi, 0, 0)),
                  pl.BlockSpec((ATT_KV_HEADS, 1, ATT_REP * tq, nb), lambda qi, bi: (0, qi, 0, 0)),
                  pl.BlockSpec(msel.shape, lambda qi, bi: (0, 0)),
                  pl.BlockSpec((1, tq, 128), lambda qi, bi: (qi, 0, 0))],
        out_specs=[pl.BlockSpec((1, tq, 512), lambda qi, bi: (bi, qi, 0)),
                   pl.BlockSpec((1, tq, 256), lambda qi, bi: (bi, qi, 0))],
        out_shape=[jax.ShapeDtypeStruct((b, t, 512), F32), jax.ShapeDtypeStruct((b, t, 256), BF16)],
        compiler_params=_params(("parallel", "parallel")),
        name="cmp_select",
    )(q, kc, vc, tab, msel, ovr)


def _flash_init(m_ref, l_ref, acc_ref):
    m_ref[...] = jnp.full(m_ref.shape, NEG, F32)
    l_ref[...] = jnp.zeros(l_ref.shape, F32)
    acc_ref[...] = jnp.zeros(acc_ref.shape, F32)


def _flash_step(s, v, m_ref, l_ref, acc_ref):
    m_old = m_ref[...]
    m_new = jnp.maximum(m_old, jnp.max(s, axis=-1, keepdims=True))
    alpha = jnp.exp(m_old - m_new)
    p = jnp.exp(s - m_new)
    l_ref[...] = alpha * l_ref[...] + jnp.sum(p, axis=-1, keepdims=True)
    acc_ref[...] = alpha * acc_ref[...] + _dot(p.astype(BF16), v)
    m_ref[...] = m_new


def _gate_chunks(gd, gexp):
    sig = 1.0 / (1.0 + jnp.exp(-gd))
    return _split_dot(sig, gexp, 3)


def _combine(gx, o_cmp, o_sel, o_win):
    return gx[:, 0:256] * o_cmp + gx[:, 256:512] * o_sel + gx[:, 512:768] * o_win


def _prompt_attn_kernel(q_ref, pen_ref, oc_ref, gd_ref, ksa_ref, vs_ref, kw_ref, vw_ref, a_ref, gexp_ref,
                        o_ref, m_ref, l_ref, acc_ref, qa_s, *, tq):
    qi = pl.program_id(1)
    g = pl.program_id(2)
    qa_s[:, 0:128] = _group_queries(q_ref[0], g, tq)
    qa_s[:, 128:256] = jnp.concatenate([pen_ref[0]] * ATT_REP, axis=0)

    def sel_tile(j, bias):
        start = pl.multiple_of(j * tq, tq)
        s = _dot_nt(qa_s[...], ksa_ref[0, pl.ds(start, tq), :])
        if bias is not None:
            s = s + bias
        _flash_step(s, vs_ref[0, pl.ds(start, tq), :], m_ref, l_ref, acc_ref)

    def win_tile(j, bias):
        start = pl.multiple_of(j * tq, tq)
        s = _dot_nt(qa_s[:, 0:128], kw_ref[0, pl.ds(start, tq), :]) + bias
        _flash_step(s, vw_ref[0, pl.ds(start, tq), :], m_ref, l_ref, acc_ref)

    def near(tile_fn, n_tiles):
        def body(i, carry):
            tile_fn(qi - i, a_ref[g, i])
            return carry
        lax.fori_loop(0, jnp.minimum(qi + 1, n_tiles), body, 0)

    _flash_init(m_ref, l_ref, acc_ref)
    near(sel_tile, 2)

    def far(j, carry):
        sel_tile(j, None)
        return carry
    lax.fori_loop(0, qi - 1, far, 0)
    o_sel = _pack_heads(acc_ref[...] / l_ref[...], g, tq)

    _flash_init(m_ref, l_ref, acc_ref)
    near(win_tile, 3)
    o_win = _pack_heads(acc_ref[...] / l_ref[...], g, tq)

    gx = _gate_chunks(gd_ref[0], gexp_ref[g])
    o_ref[0] = _combine(gx, oc_ref[0], o_sel, o_win).astype(BF16)


def _prompt_attn(q, pen, o_cmp, gd, ksa, vsb, kwb, vwb, atab, gexp, *, tq):
    b, t, _ = q.shape
    n_qt = t // tq
    qblk = lambda w: pl.BlockSpec((1, tq, w), lambda bi, qi, g: (bi, qi, 0))
    gblk = lambda w: pl.BlockSpec((1, tq, w), lambda bi, qi, g: (bi, qi, g))
    seq = lambda w: pl.BlockSpec((1, t, w), lambda bi, qi, g: (bi, 0, 0))
    full = lambda a: pl.BlockSpec(a.shape, lambda bi, qi, g: (0,) * a.ndim)
    return pl.pallas_call(
        functools.partial(_prompt_attn_kernel, tq=tq),
        grid=(b, n_qt, ATT_KV_HEADS),
        in_specs=[qblk(512), gblk(128), gblk(256), qblk(128), seq(256), seq(128), seq(128), seq(128),
                  full(atab), full(gexp)],
        out_specs=gblk(256),
        out_shape=jax.ShapeDtypeStruct((b, t, 512), BF16),
        scratch_shapes=[pltpu.VMEM((ATT_REP * tq, 1), F32), pltpu.VMEM((ATT_REP * tq, 1), F32),
                        pltpu.VMEM((ATT_REP * tq, 128), F32), pltpu.VMEM((ATT_REP * tq, 256), BF16)],
        compiler_params=_params(("parallel", "arbitrary", "arbitrary")),
        name="prompt_attn",
    )(q, pen, o_cmp, gd, ksa, vsb, kwb, vwb, atab, gexp)


def _softmax_two(s_a, s_b, v_a, v_b):
    m = jnp.maximum(jnp.max(s_a, axis=-1, keepdims=True), jnp.max(s_b, axis=-1, keepdims=True))
    p_a = jnp.exp(s_a - m)
    p_b = jnp.exp(s_b - m)
    l = jnp.sum(p_a, axis=-1, keepdims=True) + jnp.sum(p_b, axis=-1, keepdims=True)
    return (_dot(p_a.astype(BF16), v_a) + _dot(p_b.astype(BF16), v_b)) / l


def _sample_attn_kernel(pt_ref, q_ref, pen_ref, oc_ref, gd_ref, ck_ref, cv_ref, ksn_ref, vsn_ref,
                        cw_ref, cvw_ref, kwn_ref, vwn_ref, kwf_ref, vwf_ref, eall_ref, tsel_ref, tnew_ref,
                        twin_ref, gexp_ref, o_ref, kwo_ref, vwo_ref, kbuf, vbuf, ksem, vsem,
                        *, n_pages, n_new):
    slot = _gather_pipeline(pt_ref, (ck_ref, cv_ref), (kbuf, vbuf), (ksem, vsem), n_pages, PAGE_SIZE)
    tq = TQS
    q = q_ref[0]
    pen = pen_ref[0]
    kb = kbuf[slot].astype(BF16)
    vb = vbuf[slot].astype(BF16)
    kwb = cw_ref[0].astype(BF16)
    vwb = cvw_ref[0].astype(BF16)
    o_sel, o_win = [], []
    for g in range(ATT_KV_HEADS):
        qg = _group_queries(q, g, tq)
        pg = jnp.concatenate([pen[:, g * 128:(g + 1) * 128]] * ATT_REP, axis=0)
        s_past = _dot_nt(qg, kb) + _dot(pg, eall_ref[...]) + tsel_ref[g, 0]
        s_new = _dot_nt(qg, ksn_ref[0]) + tnew_ref[g, 0]
        o_sel.append(_softmax_two(s_past, s_new, vb, vsn_ref[0]))
        s_buf = _dot_nt(qg, kwb) + twin_ref[g, 0]
        s_new = _dot_nt(qg, kwn_ref[0]) + tnew_ref[g, 0]
        o_win.append(_softmax_two(s_buf, s_new, vwb, vwn_ref[0]))
    for g in range(ATT_KV_HEADS):
        gx = _gate_chunks(gd_ref[0], gexp_ref[g])
        o = _combine(gx, oc_ref[0, :, g * 256:(g + 1) * 256], _pack_heads(o_sel[g], g, tq), _pack_heads(o_win[g], g, tq))
        o_ref[0, :, g * 256:(g + 1) * 256] = o.astype(BF16)
    keep = WINDOW - n_new
    kwo_ref[0, pl.ds(0, keep), :] = cw_ref[0, pl.ds(n_new, keep), :]
    kwo_ref[0, pl.ds(keep, n_new), :] = kwf_ref[0, pl.ds(0, n_new), :]
    vwo_ref[0, pl.ds(0, keep), :] = cvw_ref[0, pl.ds(n_new, keep), :]
    vwo_ref[0, pl.ds(keep, n_new), :] = vwf_ref[0, pl.ds(0, n_new), :]


def _sample_attn(page_table, q, pen, o_cmp, gd, ck, cv, ksn, vsn, cw, cvw, kwn, vwn, kwf, vwf,
                 eall, tsel, tnew, twin, gexp, *, n_new):
    bs, n_pages = page_table.shape
    past = n_pages * PAGE_SIZE
    per_b = lambda a: pl.BlockSpec((1,) + a.shape[1:], lambda i, pt: (i,) + (0,) * (a.ndim - 1))
    full = lambda a: pl.BlockSpec(a.shape, lambda i, pt: (0,) * a.ndim)
    anyspec = pl.BlockSpec(memory_space=pl.ANY)
    ins = [q, pen, o_cmp, gd, ck, cv, ksn, vsn, cw, cvw, kwn, vwn, kwf, vwf, eall, tsel, tnew, twin, gexp]
    specs = [per_b(q), per_b(pen), per_b(o_cmp), per_b(gd), anyspec, anyspec, per_b(ksn), per_b(vsn),
             per_b(cw), per_b(cvw), per_b(kwn), per_b(vwn), per_b(kwf), per_b(vwf),
             full(eall), full(tsel), full(tnew), full(twin), full(gexp)]
    win = pl.BlockSpec((1, WINDOW, 128), lambda i, pt: (i, 0, 0))
    return pl.pallas_call(
        functools.partial(_sample_attn_kernel, n_pages=n_pages, n_new=n_new),
        grid_spec=pltpu.PrefetchScalarGridSpec(
            num_scalar_prefetch=1,
            grid=(bs,),
            in_specs=specs,
            out_specs=[pl.BlockSpec((1, TQS, 512), lambda i, pt: (i, 0, 0)), win, win],
            scratch_shapes=[pltpu.VMEM((2, past, 128), F32), pltpu.VMEM((2, past, 128), F32),
                            pltpu.SemaphoreType.DMA((2,)), pltpu.SemaphoreType.DMA((2,))]),
        out_shape=[jax.ShapeDtypeStruct((bs, TQS, 512), BF16),
                   jax.ShapeDtypeStruct((bs, WINDOW, 128), F32), jax.ShapeDtypeStruct((bs, WINDOW, 128), F32)],
        compiler_params=_params(("arbitrary",)),
        name="sample_attn",
    )(page_table, *ins)


def _lane_pair(cols, h0, h1, rows):
    lane = lax.broadcasted_iota(jnp.int32, (rows, LANES), 1)
    a = jnp.broadcast_to(cols[:, h0:h0 + 1], (rows, LANES))
    b = jnp.broadcast_to(cols[:, h1:h1 + 1], (rows, LANES))
    return jnp.where(lane < HEAD_DIM, a, b)


def _ssd_kernel(xbc_ref, z_ref, gd_ref, carry_ref, h0_ref, cw_ref, cb_ref, dtb_ref, arow_ref, dskip_ref,
                gnorm_ref, ltri_ref, y_ref, hout_ref, h_s, xfull, *, rows, n_valid):
    c = pl.program_id(1)
    halo = 8

    @pl.when(c == 0)
    def _():
        h_s[...] = h0_ref[0]
        xfull[0:halo, :] = carry_ref[0]

    xfull[halo:halo + rows, :] = xbc_ref[0]
    conv = cb_ref[...]
    for k in range(SSD_CONV):
        conv = conv + xfull[pl.ds(halo - (SSD_CONV - 1) + k, rows), :] * cw_ref[k:k + 1, :]
    tail = xfull[rows:rows + halo, :]
    xfull[0:halo, :] = tail
    xc = _silu(conv)
    xs = xc[:, 0:D_SSD]
    bm = xc[:, D_SSD:D_SSD + 128]
    cm = xc[:, D_SSD + 128:D_SSD + 256]

    lane = lax.broadcasted_iota(jnp.int32, (rows, LANES), 1)
    rowi = lax.broadcasted_iota(jnp.int32, (rows, LANES), 0)
    t = gd_ref[0] + dtb_ref[...]
    sp = jnp.maximum(t, 0.0) + jnp.log(1.0 + jnp.exp(-jnp.abs(t)))
    dt = jnp.where((lane >= DT_LANE) & (lane < DT_LANE + SSD_HEADS) & (rowi < n_valid), sp, 0.0)
    a = dt * arow_ref[...]
    acum = _split_dot_left(ltri_ref[...], a)
    acum_t = acum.T
    a_last = acum[rows - 1:rows, :]
    to_end = jnp.exp(a_last - acum)
    eac = jnp.exp(acum)
    dec = jnp.exp(a_last)

    li = lax.broadcasted_iota(jnp.int32, (rows, rows), 0)
    si = lax.broadcasted_iota(jnp.int32, (rows, rows), 1)
    causal = li >= si
    bmb = bm.astype(BF16)
    rowp = lax.broadcasted_iota(jnp.int32, (LANES, LANES), 0)
    cbs = []
    cmask = []
    for g in range(2):
        cg = jnp.where((lane >= g * 64) & (lane < (g + 1) * 64), cm, 0.0).astype(BF16)
        cmask.append(cg)
        cbs.append(_dot_nt(cg, bmb))
    for k in range(SSD_HEADS // 2):
        g = k // 2
        h0, h1 = DT_LANE + 2 * k, DT_LANE + 2 * k + 1
        xs_p = xs[:, k * 128:(k + 1) * 128]
        xdt = xs_p * _lane_pair(dt, h0, h1, rows)
        xdt_b = xdt.astype(BF16)
        ys = []
        for h in (h0, h1):
            seg = jnp.broadcast_to(acum[:, h:h + 1], (rows, rows)) - acum_t[h:h + 1, :]
            decay = jnp.where(causal, jnp.exp(jnp.where(causal, seg, 0.0)), 0.0)
            ys.append(_dot((cbs[g] * decay).astype(BF16), xdt_b))
        y = jnp.where(lane < HEAD_DIM, ys[0], ys[1])
        hp = h_s[k * 128:(k + 1) * 128, :]
        y = y + _dot_nt(cmask[g], hp.astype(BF16)) * _lane_pair(eac, h0, h1, rows)
        y = y + dskip_ref[:, k * 128:(k + 1) * 128] * xs_p
        xw = xdt * _lane_pair(to_end, h0, h1, rows)
        st = _dot(xw.T.astype(BF16), bmb)
        dfac = jnp.where(rowp < HEAD_DIM, dec[:, h0:h0 + 1], dec[:, h1:h1 + 1])
        h_s[k * 128:(k + 1) * 128, :] = hp * dfac + st
        xfull_y = y * _silu(z_ref[0, :, k * 128:(k + 1) * 128])
        y_ref[0, :, k * 128:(k + 1) * 128] = xfull_y.astype(y_ref.dtype)

    for g in range(2):
        yg = y_ref[0, :, g * 256:(g + 1) * 256].astype(F32)
        ms = jnp.mean(yg * yg, axis=-1, keepdims=True)
        y_ref[0, :, g * 256:(g + 1) * 256] = (yg * lax.rsqrt(ms + RMS_EPS)
                                               * gnorm_ref[:, g * 256:(g + 1) * 256]).astype(y_ref.dtype)

    @pl.when(c == pl.num_programs(1) - 1)
    def _():
        half = D_SSD // 2
        hout_ref[0, 0:half, :] = h_s[0:half, 0:SSD_STATE]
        hout_ref[0, half:D_SSD, :] = h_s[half:D_SSD, SSD_STATE:2 * SSD_STATE]


def _split_dot_left(tri, a):
    acc = None
    rem = a
    for _ in range(3):
        piece = rem.astype(BF16)
        rem = rem - piece.astype(F32)
        t = _dot(tri, piece)
        acc = t if acc is None else acc + t
    return acc


def _ssd(xbc, z, gd, carry, h0, wts, *, n_valid):
    b, l, _ = xbc.shape
    rows = SSD_CHUNK
    nc = l // rows
    blk = lambda w: pl.BlockSpec((1, rows, w), lambda bi, ci: (bi, ci, 0))
    per_b = lambda a: pl.BlockSpec((1,) + a.shape[1:], lambda bi, ci: (bi,) + (0,) * (a.ndim - 1))
    full = lambda a: pl.BlockSpec(a.shape, lambda bi, ci: (0,) * a.ndim)
    consts = [wts["conv_w"], wts["conv_b"], wts["dtb"], wts["arow"], wts["dskip"], wts["gnorm"], wts["ltri"]]
    return pl.pallas_call(
        functools.partial(_ssd_kernel, rows=rows, n_valid=n_valid),
        grid=(b, nc),
        in_specs=[blk(768), blk(512), blk(128), per_b(carry), per_b(h0)] + [full(a) for a in consts],
        out_specs=[blk(512), pl.BlockSpec((1, D_SSD, SSD_STATE), lambda bi, ci: (bi, 0, 0))],
        out_shape=[jax.ShapeDtypeStruct((b, l, 512), F32), jax.ShapeDtypeStruct((b, D_SSD, SSD_STATE), F32)],
        scratch_shapes=[pltpu.VMEM((512, 128), F32), pltpu.VMEM((rows + 8, 768), F32)],
        compiler_params=_params(("parallel", "arbitrary")),
        name="ssd",
    )(xbc, z, gd, carry, h0, *consts)


def _finish_kernel(x_ref, oa_ref, ys_ref, woa_ref, wos_ref, gf_ref, wg_ref, wu_ref, wd_ref,
                   y_ref, h_s, u_s, acc_s):
    f = pl.program_id(1)

    @pl.when(f == 0)
    def _():
        h = x_ref[...] + _dot(oa_ref[...], woa_ref[...]) + _dot(ys_ref[...].astype(BF16), wos_ref[...])
        h_s[...] = h
        ms = jnp.mean(h * h, axis=-1, keepdims=True)
        u_s[...] = (h * lax.rsqrt(ms + RMS_EPS) * gf_ref[...]).astype(BF16)
        acc_s[...] = jnp.zeros(acc_s.shape, F32)

    u = u_s[...]
    act = _silu(_dot(u, wg_ref[...])) * _dot(u, wu_ref[...])
    acc_s[...] += _dot(act.astype(BF16), wd_ref[...])

    @pl.when(f == pl.num_programs(1) - 1)
    def _():
        y_ref[...] = h_s[...] + acc_s[...]


def _finish(x2d, o_att, y_ssd, wts):
    n = x2d.shape[0]
    tm = min(512, n)
    nf = 2
    tf = D_FF // nf
    row = lambda w: pl.BlockSpec((tm, w), lambda i, f: (i, 0))
    full = lambda a: pl.BlockSpec(a.shape, lambda i, f: (0,) * a.ndim)
    return pl.pallas_call(
        _finish_kernel,
        grid=(n // tm, nf),
        in_specs=[row(D_MODEL), row(512), row(512), full(wts["wo_att"]), full(wts["wo_ssd"]), full(wts["gffn"]),
                  pl.BlockSpec((D_MODEL, tf), lambda i, f: (0, f)),
                  pl.BlockSpec((D_MODEL, tf), lambda i, f: (0, f)),
                  pl.BlockSpec((tf, D_MODEL), lambda i, f: (f, 0))],
        out_specs=row(D_MODEL),
        out_shape=jax.ShapeDtypeStruct((n, D_MODEL), F32),
        scratch_shapes=[pltpu.VMEM((tm, D_MODEL), F32), pltpu.VMEM((tm, D_MODEL), BF16),
                        pltpu.VMEM((tm, D_MODEL), F32)],
        compiler_params=_params(("parallel", "arbitrary")),
        name="finish",
    )(x2d, o_att, y_ssd, wts["wo_att"], wts["wo_ssd"], wts["gffn"], wts["w_gate"], wts["w_up"], wts["w_down"])


def _pair_perm():
    cols = []
    for r in range(ATT_REP):
        cols += list(range(r * HEAD_DIM, (r + 1) * HEAD_DIM))
        cols += list(range((ATT_REP + r) * HEAD_DIM, (ATT_REP + r + 1) * HEAD_DIM))
    return np.asarray(cols, np.int32)


def _block_ones(n, blk):
    i = np.arange(n)
    return (i[:, None] // blk == i[None, :] // blk).astype(np.float32) / blk


def _prep_weights(norm_mix, w_in, q_norm, k_norm, cmp_pe, cmp_w1, cmp_w2, conv_w, conv_b, dt_bias, a_log,
                  d_skip, ssd_norm, w_out, norm_ffn, w_gate, w_up, w_down):
    perm = _pair_perm()
    w = w_in
    gd = jnp.concatenate([w[:, OFF_GATE:OFF_Z], w[:, OFF_DT:D_IN],
                          jnp.zeros((D_MODEL, 128 - 3 * ATT_HEADS - SSD_HEADS), w.dtype)], axis=1)
    w_r = jnp.concatenate([w[:, :D_ATT][:, perm], w[:, OFF_KV:OFF_GATE], w[:, OFF_Z:OFF_XBC],
                           w[:, OFF_XBC:OFF_DT], gd], axis=1).astype(BF16)
    wts = dict(
        gmix=norm_mix.reshape(1, D_MODEL), w_in=w_r,
        qg=(jnp.tile(q_norm, ATT_HEADS) * (HEAD_DIM ** -0.5)).reshape(1, D_ATT),
        kg=jnp.tile(k_norm, ATT_KV_HEADS).reshape(1, D_KV),
        bq=jnp.asarray(_block_ones(D_ATT, HEAD_DIM), BF16), bk=jnp.asarray(_block_ones(D_KV, HEAD_DIM), BF16))

    def w1_big(w1):
        w1r = w1.reshape(2, CMP_STRIDE, HEAD_DIM, CMP_HIDDEN)
        eye = jnp.eye(ATT_KV_HEADS, dtype=w1.dtype)
        big = jnp.einsum("jsdh,gk->sgdjkh", w1r, eye)
        return big.reshape(CMP_STRIDE * D_KV, 2 * D_KV).astype(BF16)

    def pe_rows(pe):
        per = jnp.broadcast_to(pe.reshape(2, CMP_STRIDE, 1, HEAD_DIM), (2, CMP_STRIDE, ATT_KV_HEADS, HEAD_DIM))
        per = per.reshape(2, CMP_STRIDE * D_KV)
        return jnp.concatenate([per, jnp.zeros((6, CMP_STRIDE * D_KV), pe.dtype)], axis=0).astype(BF16)

    def w2_big(w2):
        eye = jnp.eye(ATT_KV_HEADS, dtype=w2.dtype)
        return jnp.einsum("hd,gk->ghkd", w2, eye).reshape(D_KV, D_KV).astype(BF16)

    wts.update(w1k=w1_big(cmp_w1[0]), w1v=w1_big(cmp_w1[1]), pek=pe_rows(cmp_pe[0]), pev=pe_rows(cmp_pe[1]),
               w2k=w2_big(cmp_w2[0]), w2v=w2_big(cmp_w2[1]))

    pad_lanes = lambda v: jnp.zeros((1, LANES), F32).at[0, DT_LANE:DT_LANE + SSD_HEADS].set(v)
    ltri = np.tril(np.ones((SSD_CHUNK, SSD_CHUNK), np.float32))
    wts.update(conv_w=jnp.concatenate([conv_w, jnp.zeros((4, D_CONV), F32)], axis=0), conv_b=conv_b.reshape(1, D_CONV),
               dtb=pad_lanes(dt_bias), arow=pad_lanes(-jnp.exp(a_log)),
               dskip=jnp.repeat(d_skip, 64).reshape(1, D_SSD), gnorm=ssd_norm.reshape(1, D_SSD),
               ltri=jnp.asarray(ltri, BF16))
    wts.update(wo_att=w_out[:D_ATT].astype(BF16), wo_ssd=w_out[D_ATT:].astype(BF16),
               gffn=norm_ffn.reshape(1, D_MODEL), w_gate=w_gate.astype(BF16), w_up=w_up.astype(BF16),
               w_down=w_down.astype(BF16))
    return wts


def _gate_expand():
    m = np.zeros((ATT_KV_HEADS, LANES, 3 * ATT_REP * HEAD_DIM), np.float32)
    for g in range(ATT_KV_HEADS):
        for r in range(ATT_REP):
            for br in range(3):
                c0 = br * ATT_REP * HEAD_DIM + r * HEAD_DIM
                m[g, g * 3 * ATT_REP + r * 3 + br, c0:c0 + HEAD_DIM] = 1.0
    return jnp.asarray(m, BF16)


def _sel_matrix(n_blk_pad, n_sel):
    m = np.zeros((n_blk_pad, LANES), np.float32)
    for n in range(n_blk_pad - 1):
        for j in {n // 4, (n + 1) // 4}:
            if j < n_sel:
                m[n, j] = 1.0
    return jnp.asarray(m, BF16)


def _override(q_pos, n_sel):
    j = np.arange(LANES)[None, :]
    cur = (q_pos // SEL_BLOCK)[:, None]
    forced = (j == 0) | (j == cur) | (j == cur - 1)
    ovr = np.where(forced, 1e30, np.where(j <= cur, 0.0, -1e30))
    ovr = np.where(j < n_sel, ovr, -1e30)
    return ovr.astype(np.float32)


def _prompt_tables(rel_bias, t):
    n_qt = t // TQ
    pos = np.arange(t)
    nb = t // CMP_STRIDE
    e = CMP_STRIDE * np.arange(nb) + (CMP_BLOCK - 1)
    dist = pos[:, None] - e[None, :]
    cmp_idx = _idx_table(dist, (dist >= 0) & (np.arange(nb)[None, :] < nb - 1)).reshape(n_qt, TQ, nb)
    i = np.arange(TQ)[:, None]
    j = np.arange(TQ)[None, :]
    diag = _idx_table(i - j, i >= j)
    prev = _idx_table(TQ + i - j, np.ones((TQ, TQ), bool))
    prev2 = _idx_table(2 * TQ + i - j, (2 * TQ + i - j) < WINDOW)
    att_idx = np.stack([diag, prev, prev2])
    return _bias_tables(rel_bias, cmp_idx), _bias_tables(rel_bias, att_idx)


def _prompt_layer(x, wts, rel_bias):
    b, t, _ = x.shape
    n = b * t
    (q, kc, vc, ks, vs, kw, vw, ksa, vsb, kwb, vwb, z, xbc, gd) = _project(x.reshape(n, D_MODEL), wts, t)
    n_sub = t // CMP_STRIDE
    kcmp, vcmp = _compress_prompt(kc.reshape(b, n_sub, CMP_STRIDE * D_KV), vc.reshape(b, n_sub, CMP_STRIDE * D_KV), wts)
    cmp_tab, att_tab = _prompt_tables(rel_bias, t)
    n_sel = t // SEL_BLOCK
    ovr = jnp.asarray(_override(np.arange(t), n_sel).reshape(t // TQ, TQ, LANES))
    r3 = lambda a: a.reshape(b, t, a.shape[-1])
    o_cmp, pen = _cmp_select(r3(q), kcmp, vcmp, cmp_tab, _sel_matrix(n_sub, n_sel), ovr,
                             tq=TQ, n_sel=n_sel, k_top=min(SEL_TOPN, n_sel))
    o_att = _prompt_attn(r3(q), pen, o_cmp, r3(gd), r3(ksa), r3(vsb), r3(kwb), r3(vwb), att_tab, _gate_expand(), tq=TQ)
    carry = jnp.zeros((b, 8, D_CONV), F32)
    h0 = jnp.zeros((b, 512, 128), F32)
    y_ssd, h_fin = _ssd(r3(xbc), r3(z), r3(gd), carry, h0, wts, n_valid=SSD_CHUNK)
    y = _finish(x.reshape(n, D_MODEL), o_att.reshape(n, 512), y_ssd.reshape(n, 512), wts)
    wb = min(WINDOW, t)
    kv4 = lambda a: a.reshape(b, t, ATT_KV_HEADS, HEAD_DIM)
    ssm = h_fin.reshape(b, SSD_HEADS, 64, SSD_STATE)
    return y.reshape(b, t, D_MODEL), (kv4(kc), kv4(vc), kv4(ks), kv4(vs), kv4(kw)[:, t - wb:], kv4(vw)[:, t - wb:],
                                      r3(xbc)[:, t - (SSD_CONV - 1):], ssm)


def _sample_tables(rel_bias, past, s_new, n_blk_pad):
    s = np.minimum(np.arange(TQS), s_new - 1)[:, None]
    pos = past + s
    nidx = np.arange(n_blk_pad)[None, :]
    e = CMP_STRIDE * nidx + (CMP_BLOCK - 1)
    cmp_idx = _idx_table(pos - e, (e <= pos) & (nidx < n_blk_pad - 1))[None]
    key = np.arange(past)[None, :]
    sel_idx = _idx_table(pos - key, np.ones((TQS, past), bool))[None]
    jn = np.arange(LANES)[None, :]
    new_idx = _idx_table(s - jn, (jn <= s) & (jn < s_new))[None]
    wi = np.arange(WINDOW)[None, :]
    wdist = pos - (past - WINDOW + wi)
    win_idx = _idx_table(wdist, (wdist >= 0) & (wdist < WINDOW))[None]
    tabs = [_bias_tables(rel_bias, t) for t in (cmp_idx, sel_idx, new_idx, win_idx)]
    return tabs, pos[:, 0]


def _sample_layer(x, c_kc, c_vc, c_ks, c_vs, c_kw, c_vw, s_conv, s_ssm, page_table, wts, rel_bias):
    bs, s_new, _ = x.shape
    n = bs * s_new
    n_pages = page_table.shape[1]
    past = n_pages * PAGE_SIZE
    (q, kc, vc, ks, vs, kw, vw, _, vsb, kwb, vwb, z, xbc, gd) = _project(x.reshape(n, D_MODEL), wts, s_new)
    r3 = lambda a: a.reshape(bs, s_new, a.shape[-1])
    padq = lambda a, rows: jnp.pad(r3(a), ((0, 0), (0, rows - s_new), (0, 0)))
    n_phys = c_kc.shape[0]
    rows = PAGE_SIZE // CMP_STRIDE
    kcmp, vcmp = _compress_sample(page_table, c_kc.reshape(n_phys, rows, CMP_STRIDE * D_KV),
                                  c_vc.reshape(n_phys, rows, CMP_STRIDE * D_KV), wts)
    n_blk_pad = past // CMP_STRIDE
    (cmp_tab, sel_tab, new_tab, win_tab), pos = _sample_tables(rel_bias, past, s_new, n_blk_pad)
    n_sel = past // SEL_BLOCK
    ovr = jnp.asarray(_override(pos, n_sel)[None])
    qp = padq(q, TQS)
    o_cmp, pen = _cmp_select(qp, kcmp, vcmp, cmp_tab, _sel_matrix(n_blk_pad, n_sel), ovr,
                             tq=TQS, n_sel=n_sel, k_top=min(SEL_TOPN - 1, n_sel))
    blk_of_key = np.arange(past)[None, :] // SEL_BLOCK
    eall = jnp.asarray((np.arange(LANES)[:, None] == blk_of_key).astype(np.float32), BF16)
    o_att, kw_new, vw_new = _sample_attn(
        page_table, qp, pen, o_cmp, padq(gd, TQS),
        c_ks.reshape(n_phys, PAGE_SIZE, D_KV), c_vs.reshape(n_phys, PAGE_SIZE, D_KV),
        padq(ks.astype(BF16), LANES), padq(vsb, LANES),
        c_kw.reshape(bs, WINDOW, D_KV), c_vw.reshape(bs, WINDOW, D_KV),
        padq(kwb, LANES), padq(vwb, LANES), padq(kw, 8), padq(vw, 8),
        eall, sel_tab, new_tab, win_tab, _gate_expand(), n_new=s_new)
    carry = jnp.pad(s_conv, ((0, 0), (8 - (SSD_CONV - 1), 0), (0, 0)))
    h0 = s_ssm.reshape(bs, 512, SSD_STATE)
    h0 = jnp.concatenate([h0, h0], axis=-1)
    y_ssd, h_fin = _ssd(padq(xbc, SSD_CHUNK), padq(z, SSD_CHUNK), padq(gd, SSD_CHUNK), carry, h0, wts, n_valid=s_new)
    y = _finish(x.reshape(n, D_MODEL), o_att[:, :s_new].reshape(n, 512), y_ssd[:, :s_new].reshape(n, 512), wts)
    kv4 = lambda a: a.reshape(bs, -1, ATT_KV_HEADS, HEAD_DIM)
    ssm = h_fin.reshape(bs, SSD_HEADS, 64, SSD_STATE)
    conv_state = jnp.concatenate([s_conv, r3(xbc)], axis=1)[:, s_new:]
    return y.reshape(bs, s_new, D_MODEL), (kv4(kc), kv4(vc), kv4(ks), kv4(vs), kv4(kw_new), kv4(vw_new), conv_state, ssm)


def kernel(x_prompt, x_sample, cache_k_cmp, cache_v_cmp, cache_k_sel, cache_v_sel, cache_k_win, cache_v_win,
           state_conv, state_ssm, page_table, norm_mix, w_in, q_norm, k_norm, cmp_pe, cmp_w1, cmp_w2, rel_bias,
           conv_w, conv_b, dt_bias, a_log, d_skip, ssd_norm, w_out, norm_ffn, w_gate, w_up, w_down):
    depth = w_in.shape[0]
    y_p, y_s = x_prompt, x_sample
    p_states, s_states = [], []
    for l in range(depth):
        wts = _prep_weights(norm_mix[l], w_in[l], q_norm[l], k_norm[l], cmp_pe[l], cmp_w1[l], cmp_w2[l],
                            conv_w[l], conv_b[l], dt_bias[l], a_log[l], d_skip[l], ssd_norm[l], w_out[l],
                            norm_ffn[l], w_gate[l], w_up[l], w_down[l])
        y_p, st_p = _prompt_layer(y_p, wts, rel_bias)
        y_s, st_s = _sample_layer(y_s, cache_k_cmp[l], cache_v_cmp[l], cache_k_sel[l], cache_v_sel[l],
                                  cache_k_win[l], cache_v_win[l], state_conv[l], state_ssm[l], page_table,
                                  wts, rel_bias)
        p_states.append(st_p)
        s_states.append(st_s)
    p_out = [jnp.stack(a) for a in zip(*p_states)]
    s_out = [jnp.stack(a) for a in zip(*s_states)]
    return (y_p, y_s, *p_out, *s_out)
```

```python
import functools
import math

import numpy as np
import jax
import jax.numpy as jnp
from jax import lax
from jax.experimental import pallas as pl
from jax.experimental.pallas import tpu as pltpu

F32 = jnp.float32
BF16 = jnp.bfloat16

D_MODEL = 1024
HEAD_DIM = 64
ATT_HEADS = 8
ATT_KV_HEADS = 2
ATT_REP = ATT_HEADS // ATT_KV_HEADS
CMP_BLOCK = 32
CMP_STRIDE = 16
CMP_HIDDEN = 64
SEL_BLOCK = 64
SEL_TOPN = 16
WINDOW = 512
N_BUCKETS = 32
MAX_DISTANCE = 128
PAGE_SIZE = 128
SSD_HEADS = 8
SSD_STATE = 64
SSD_CONV = 4
SSD_CHUNK = 128
D_ATT = ATT_HEADS * HEAD_DIM
D_SSD = SSD_HEADS * 64
D_KV = ATT_KV_HEADS * HEAD_DIM
D_CONV = D_SSD + 2 * 2 * SSD_STATE
D_FF = ((8 * D_MODEL // 3 + 255) // 256) * 256
OFF_KV = D_ATT
OFF_GATE = OFF_KV + 6 * D_KV
OFF_Z = OFF_GATE + 3 * ATT_HEADS
OFF_XBC = OFF_Z + D_SSD
OFF_DT = OFF_XBC + D_CONV
D_IN = OFF_DT + SSD_HEADS
RMS_EPS = 1e-6
NEG = -1e30

C_Q, C_KV, C_Z, C_XBC, C_GD = 0, 512, 1280, 1792, 2560
D_INR = 2688
DT_LANE = 24
LANES = 128
TQ = 256
TQS = 16
VMEM_LIMIT = 48 * 1024 * 1024


def _dot(a, b):
    return jnp.dot(a, b, preferred_element_type=F32)


def _dot_nt(a, b):
    return lax.dot_general(a, b, (((1,), (1,)), ((), ())), preferred_element_type=F32)


def _split_dot(a, b, parts):
    acc = None
    rem = a
    for _ in range(parts):
        piece = rem.astype(BF16)
        rem = rem - piece.astype(F32)
        t = _dot(piece, b)
        acc = t if acc is None else acc + t
    return acc


def _silu(x):
    return x * (1.0 / (1.0 + jnp.exp(-x)))


def _params(sem=None):
    kw = dict(vmem_limit_bytes=VMEM_LIMIT)
    if sem is not None:
        kw["dimension_semantics"] = sem
    return pltpu.CompilerParams(**kw)


def _proj_kernel(x_ref, gmix_ref, w_ref, qg_ref, kg_ref, bq_ref, bk_ref,
                 q_ref, kc_ref, vc_ref, ks_ref, vs_ref, kw_ref, vw_ref,
                 ksa_ref, vsb_ref, kwb_ref, vwb_ref, z_ref, xbc_ref, gd_ref, qt_ref, vst_ref, vwt_ref,
                 *, tm, t_len):
    x = x_ref[...]
    ms = jnp.mean(x * x, axis=-1, keepdims=True)
    u = (x * lax.rsqrt(ms + RMS_EPS) * gmix_ref[...]).astype(BF16)

    def proj(lo, hi):
        return _dot(u, w_ref[:, lo:hi])

    def headnorm(v, b_ref, g_ref):
        msq = _dot((v * v).astype(BF16), b_ref[...])
        return v * lax.rsqrt(msq + RMS_EPS) * g_ref[...]

    q = headnorm(proj(C_Q, C_Q + 512), bq_ref, qg_ref)
    q_ref[...] = q.astype(BF16)
    qt_ref[...] = q.T.astype(BF16)
    kc_ref[...] = proj(C_KV, C_KV + 128)
    vc_ref[...] = proj(C_KV + 128, C_KV + 256)
    ks = headnorm(proj(C_KV + 256, C_KV + 384), bk_ref, kg_ref)
    ks_ref[...] = ks
    vs = proj(C_KV + 384, C_KV + 512)
    vs_ref[...] = vs
    vsb_ref[...] = vs.astype(BF16)
    vst_ref[...] = vs.T.astype(BF16)
    kw = headnorm(proj(C_KV + 512, C_KV + 640), bk_ref, kg_ref)
    kw_ref[...] = kw
    kwb_ref[...] = kw.astype(BF16)
    vw = proj(C_KV + 640, C_KV + 768)
    vw_ref[...] = vw
    vwb_ref[...] = vw.astype(BF16)
    vwt_ref[...] = vw.T.astype(BF16)
    row = pl.program_id(0) * tm + lax.broadcasted_iota(jnp.int32, (tm, LANES), 0)
    blk = (row % t_len) // SEL_BLOCK
    lane = lax.broadcasted_iota(jnp.int32, (tm, LANES), 1)
    ksa_ref[:, 0:128] = ks.astype(BF16)
    ksa_ref[:, 128:256] = jnp.where(lane == blk, 1.0, 0.0).astype(BF16)
    z_ref[...] = proj(C_Z, C_Z + 512)
    xbc_ref[...] = proj(C_XBC, C_XBC + 768)
    gd_ref[...] = proj(C_GD, C_GD + 128)


def _project(x2d, wts, t_len):
    n = x2d.shape[0]
    tm = min(512, n)
    row = lambda w: pl.BlockSpec((tm, w), lambda i: (i, 0))
    full = lambda a: pl.BlockSpec(a.shape, lambda i: (0,) * a.ndim)
    ins = [x2d, wts["gmix"], wts["w_in"], wts["qg"], wts["kg"], wts["bq"], wts["bk"]]
    out_w = [(512, BF16)] + [(128, F32)] * 6 + [(256, BF16)] + [(128, BF16)] * 3 + [(512, F32), (768, F32), (128, F32)]
    out_t = [512, 128, 128]
    col = lambda h: pl.BlockSpec((h, tm), lambda i: (0, i))
    return pl.pallas_call(
        functools.partial(_proj_kernel, tm=tm, t_len=t_len),
        grid=(n // tm,),
        in_specs=[row(D_MODEL)] + [full(a) for a in ins[1:]],
        out_specs=[row(w) for w, _ in out_w] + [col(h) for h in out_t],
        out_shape=[jax.ShapeDtypeStruct((n, w), d) for w, d in out_w]
        + [jax.ShapeDtypeStruct((h, n), BF16) for h in out_t],
        compiler_params=_params(("parallel",)),
        name="proj",
    )(*ins)


def _bucket_np(dist):
    n = np.maximum(dist, 0)
    max_exact = N_BUCKETS // 2
    nf = np.maximum(n, 1).astype(np.float64)
    large = max_exact + (np.log(nf / max_exact) / math.log(MAX_DISTANCE / max_exact)
                         * (N_BUCKETS - max_exact)).astype(np.int64)
    large = np.minimum(large, N_BUCKETS - 1)
    return np.where(n < max_exact, n, large).astype(np.int32)


def _idx_table(dist, valid):
    return np.where(valid, _bucket_np(dist), -1).astype(np.int32)


def _table_kernel(rb_ref, idx_ref, out_ref):
    h = pl.program_id(0) * ATT_REP + pl.program_id(2)
    idx = idx_ref[0]
    far = rb_ref[N_BUCKETS - 1, h]
    acc = jnp.zeros(idx.shape, F32)
    for b in range(N_BUCKETS - 1):
        acc = jnp.where(idx == b, rb_ref[b, h] - far, acc)
    out_ref[...] = jnp.where(idx < 0, NEG, acc).reshape(out_ref.shape)


def _bias_tables(rel_bias, idx, stack_cols=False):
    k, r, c = idx.shape
    if stack_cols:
        return pl.pallas_call(
            _table_kernel,
            grid=(ATT_KV_HEADS, k, ATT_REP),
            in_specs=[pl.BlockSpec(memory_space=pltpu.SMEM),
                      pl.BlockSpec((1, r, c), lambda g, kk, rr: (kk, 0, 0))],
            out_specs=pl.BlockSpec((1, 1, r, c), lambda g, kk, rr: (g, kk, 0, rr)),
            out_shape=jax.ShapeDtypeStruct((ATT_KV_HEADS, k, r, ATT_REP * c), F32),
            name="bias_table_t",
        )(rel_bias, jnp.asarray(idx))
    out = pl.pallas_call(
        _table_kernel,
        grid=(ATT_KV_HEADS, k, ATT_REP),
        in_specs=[pl.BlockSpec(memory_space=pltpu.SMEM),
                  pl.BlockSpec((1, r, c), lambda g, kk, rr: (kk, 0, 0))],
        out_specs=pl.BlockSpec((1, 1, 1, r, c), lambda g, kk, rr: (g, kk, rr, 0, 0)),
        out_shape=jax.ShapeDtypeStruct((ATT_KV_HEADS, k, ATT_REP, r, c), F32),
        name="bias_table",
    )(rel_bias, jnp.asarray(idx))
    return out.reshape(ATT_KV_HEADS, k, ATT_REP * r, c)


def _compress_core(x, w1, pe, w2):
    n_sub = x.shape[0]
    u = _dot(x.astype(BF16), w1)
    upe = _dot(pe, w1)
    nxt = pltpu.roll(u[:, 128:256], n_sub - 1, 0)
    pre = u[:, 0:128] + nxt + upe[0:1, 0:128] + upe[1:2, 128:256]
    return _dot(_silu(pre).astype(BF16), w2)


def _knorm(v, bk, kg):
    msq = _dot((v * v).astype(BF16), bk)
    return v * lax.rsqrt(msq + RMS_EPS) * kg


def _compress_prompt_kernel(kc_ref, vc_ref, w1k_ref, w1v_ref, pek_ref, pev_ref, w2k_ref, w2v_ref,
                            kg_ref, bk_ref, ko_ref, vo_ref):
    kc = _compress_core(kc_ref[0], w1k_ref[...], pek_ref[...], w2k_ref[...])
    ko_ref[0] = _knorm(kc, bk_ref[...], kg_ref[...]).astype(BF16)
    vo_ref[0] = _compress_core(vc_ref[0], w1v_ref[...], pev_ref[...], w2v_ref[...]).astype(BF16)


def _compress_prompt(kc, vc, wts):
    b, n_sub, w = kc.shape
    full = lambda a: pl.BlockSpec(a.shape, lambda i: (0,) * a.ndim)
    consts = [wts["w1k"], wts["w1v"], wts["pek"], wts["pev"], wts["w2k"], wts["w2v"], wts["kg"], wts["bk"]]
    blk = pl.BlockSpec((1, n_sub, w), lambda i: (i, 0, 0))
    oblk = pl.BlockSpec((1, n_sub, 128), lambda i: (i, 0, 0))
    return pl.pallas_call(
        _compress_prompt_kernel,
        grid=(b,),
        in_specs=[blk, blk] + [full(a) for a in consts],
        out_specs=[oblk, oblk],
        out_shape=[jax.ShapeDtypeStruct((b, n_sub, 128), BF16)] * 2,
        compiler_params=_params(("parallel",)),
        name="compress_prompt",
    )(kc, vc, *consts)


def _page_copy(cache_ref, page, buf_ref, slot, p, rows, sem_ref):
    return pltpu.make_async_copy(cache_ref.at[page], buf_ref.at[slot, pl.ds(p * rows, rows)], sem_ref.at[slot])


def _gather_start(pt_ref, b, slot, caches, bufs, sems, n_pages, rows):
    def body(p, carry):
        page = pt_ref[b, p]
        for cache_ref, buf_ref, sem_ref in zip(caches, bufs, sems):
            _page_copy(cache_ref, page, buf_ref, slot, p, rows, sem_ref).start()
        return carry
    lax.fori_loop(0, n_pages, body, 0)


def _gather_wait(slot, caches, bufs, sems, n_pages, rows):
    def body(p, carry):
        for cache_ref, buf_ref, sem_ref in zip(caches, bufs, sems):
            _page_copy(cache_ref, 0, buf_ref, slot, p, rows, sem_ref).wait()
        return carry
    lax.fori_loop(0, n_pages, body, 0)


def _gather_pipeline(pt_ref, caches, bufs, sems, n_pages, rows):
    b = pl.program_id(0)
    nb = pl.num_programs(0)
    slot = b % 2

    @pl.when(b == 0)
    def _():
        _gather_start(pt_ref, 0, 0, caches, bufs, sems, n_pages, rows)

    @pl.when(b + 1 < nb)
    def _():
        _gather_start(pt_ref, b + 1, 1 - slot, caches, bufs, sems, n_pages, rows)

    _gather_wait(slot, caches, bufs, sems, n_pages, rows)
    return slot


def _compress_sample_kernel(pt_ref, ck_ref, cv_ref, w1k_ref, w1v_ref, pek_ref, pev_ref, w2k_ref, w2v_ref,
                            kg_ref, bk_ref, ko_ref, vo_ref, kbuf, vbuf, ksem, vsem, *, n_pages):
    rows = PAGE_SIZE // CMP_STRIDE
    slot = _gather_pipeline(pt_ref, (ck_ref, cv_ref), (kbuf, vbuf), (ksem, vsem), n_pages, rows)
    kc = _compress_core(kbuf[slot], w1k_ref[...], pek_ref[...], w2k_ref[...])
    ko_ref[0] = _knorm(kc, bk_ref[...], kg_ref[...]).astype(BF16)
    vo_ref[0] = _compress_core(vbuf[slot], w1v_ref[...], pev_ref[...], w2v_ref[...]).astype(BF16)


def _compress_sample(page_table, ck, cv, wts):
    bs, n_pages = page_table.shape
    rows = PAGE_SIZE // CMP_STRIDE
    n_sub = n_pages * rows
    w = ck.shape[-1]
    full = lambda a: pl.BlockSpec(a.shape, lambda i, pt: (0,) * a.ndim)
    consts = [wts["w1k"], wts["w1v"], wts["pek"], wts["pev"], wts["w2k"], wts["w2v"], wts["kg"], wts["bk"]]
    anyspec = pl.BlockSpec(memory_space=pl.ANY)
    oblk = pl.BlockSpec((1, n_sub, 128), lambda i, pt: (i, 0, 0))
    return pl.pallas_call(
        functools.partial(_compress_sample_kernel, n_pages=n_pages),
        grid_spec=pltpu.PrefetchScalarGridSpec(
            num_scalar_prefetch=1,
            grid=(bs,),
            in_specs=[anyspec, anyspec] + [full(a) for a in consts],
            out_specs=[oblk, oblk],
            scratch_shapes=[pltpu.VMEM((2, n_sub, w), F32), pltpu.VMEM((2, n_sub, w), F32),
                            pltpu.SemaphoreType.DMA((2,)), pltpu.SemaphoreType.DMA((2,))]),
        out_shape=[jax.ShapeDtypeStruct((bs, n_sub, 128), BF16)] * 2,
        compiler_params=_params(("arbitrary",)),
        name="compress_sample",
    )(page_table, ck, cv, *consts)


def _group_queries(q, g, tq):
    lane = lax.broadcasted_iota(jnp.int32, (tq, LANES), 1)
    mine = (lane >= g * HEAD_DIM) & (lane < (g + 1) * HEAD_DIM)
    zero = jnp.zeros((tq, LANES), q.dtype)
    return jnp.concatenate([jnp.where(mine, q[:, r * 128:(r + 1) * 128], zero) for r in range(ATT_REP)], axis=0)


def _pack_heads(o, g, tq):
    lane = lax.broadcasted_iota(jnp.int32, (tq, LANES), 1)
    first = g == 0
    chunks = []
    for k in range(ATT_REP // 2):
        a = o[2 * k * tq:(2 * k + 1) * tq]
        b = o[(2 * k + 1) * tq:(2 * k + 2) * tq]
        lo = jnp.where(first, a, pltpu.roll(a, HEAD_DIM, 1))
        hi = jnp.where(first, pltpu.roll(b, HEAD_DIM, 1), b)
        chunks.append(jnp.where(lane < HEAD_DIM, lo, hi))
    return jnp.concatenate(chunks, axis=1)


def _cmp_select_kernel(q_ref, kc_ref, vc_ref, tab_ref, msel_ref, ovr_ref, o_ref, pen_ref, *, tq, n_sel, k_top):
    q = q_ref[0]
    kc = kc_ref[0]
    vc = vc_ref[0]
    ovr = ovr_ref[0]
    lane = lax.broadcasted_iota(jnp.int32, (tq, LANES), 1)
    outs = []
    for g in range(ATT_KV_HEADS):
        tab = tab_ref[g, 0]
        s = _dot_nt(_group_queries(q, g, tq), kc) + tab
        m = jnp.max(s, axis=-1, keepdims=True)
        e = jnp.where(tab > 0.5 * NEG, jnp.exp(s - m), 0.0)
        p = e / jnp.maximum(jnp.sum(e, axis=-1, keepdims=True), 1e-30)
        outs.append(_dot(p.astype(BF16), vc))
        imp = p[0:tq]
        for r in range(1, ATT_REP):
            imp = imp + p[r * tq:(r + 1) * tq]
        score = _split_dot(imp, msel_ref[...], 3)
        score = jnp.where(ovr == 0.0, score, ovr)
        rank = jnp.zeros((tq, LANES), F32)
        for j in range(n_sel):
            col = score[:, j:j + 1]
            beats = (col > score) | ((col == score) & (lane > j))
            rank = rank + jnp.where(beats, 1.0, 0.0)
        pen_ref[0, :, g * 128:(g + 1) * 128] = jnp.where(rank < k_top, 0.0, NEG).astype(BF16)
    for g in range(ATT_KV_HEADS):
        o_ref[0, :, g * 256:(g + 1) * 256] = _pack_heads(outs[g], g, tq)


def _cmp_select(q, kc, vc, tab, msel, ovr, *, tq, n_sel, k_top):
    b, t, _ = q.shape
    nb = kc.shape[1]
    n_qt = t // tq
    return pl.pallas_call(
        functools.partial(_cmp_select_kernel, tq=tq, n_sel=n_sel, k_top=k_top),
        grid=(n_qt, b),
        in_specs=[pl.BlockSpec((1, tq, 512), lambda qi, bi: (bi, qi, 0)),
                  pl.BlockSpec((1, nb, 128), lambda qi, bi: (bi, 0, 0)),
                  pl.BlockSpec((1, nb, 128), lambda qi, bi: (bi, 0, 0)),
                  pl.BlockSpec((ATT_KV_HEADS, 1, ATT_REP * tq, nb), lambda qi, bi: (0, qi, 0, 0)),
                  pl.BlockSpec(msel.shape, lambda qi, bi: (0, 0)),
                  pl.BlockSpec((1, tq, 128), lambda qi, bi: (qi, 0, 0))],
        out_specs=[pl.BlockSpec((1, tq, 512), lambda qi, bi: (bi, qi, 0)),
                   pl.BlockSpec((1, tq, 256), lambda qi, bi: (bi, qi, 0))],
        out_shape=[jax.ShapeDtypeStruct((b, t, 512), F32), jax.ShapeDtypeStruct((b, t, 256), BF16)],
        compiler_params=_params(("parallel", "parallel")),
        name="cmp_select",
    )(q, kc, vc, tab, msel, ovr)


def _cmp_select_t_kernel(qt_ref, kc_ref, vc_ref, tab_ref, msel_ref, ovr_ref, o_ref, pen_ref,
                         *, tq, n_sel, k_top):
    qt = qt_ref[...]
    kc = kc_ref[0]
    vct = vc_ref[0].astype(F32).T.astype(BF16)
    ovr = ovr_ref[0, 0:n_sel, :]
    row = lax.broadcasted_iota(jnp.int32, (n_sel, tq), 0)
    for g in range(ATT_KV_HEADS):
        tab = tab_ref[g, 0]
        s = _dot(kc, _group_queries_t(qt, g, tq)) + tab
        m = jnp.max(s, axis=0, keepdims=True)
        e = jnp.where(tab > 0.5 * NEG, jnp.exp(s - m), 0.0)
        p = e / jnp.maximum(jnp.sum(e, axis=0, keepdims=True), 1e-30)
        o_ref[0, :, g * 256:(g + 1) * 256] = _unpack_heads_t(_dot(vct, p.astype(BF16)), g, tq)
        imp = p[:, 0:tq]
        for r in range(1, ATT_REP):
            imp = imp + p[:, r * tq:(r + 1) * tq]
        score = _split_dot_left(msel_ref[...], imp)[0:n_sel, :]
        score = jnp.where(ovr == 0.0, score, ovr)
        rank = jnp.zeros((n_sel, tq), F32)
        for j in range(n_sel):
            cand = score[j:j + 1, :]
            beats = (cand > score) | ((cand == score) & (row > j))
            rank = rank + jnp.where(beats, 1.0, 0.0)
        pen_ref[0, g, 0:n_sel, :] = jnp.where(rank < k_top, 0.0, NEG).astype(BF16)
        pen_ref[0, g, n_sel:LANES, :] = jnp.zeros((LANES - n_sel, tq), BF16)


def _cmp_select_t(qt, kc, vc, tab, msel_t, ovr_t, *, tq, n_sel, k_top):
    b, nb, _ = kc.shape
    n_qt = tab.shape[1]
    t = n_qt * tq
    return pl.pallas_call(
        functools.partial(_cmp_select_t_kernel, tq=tq, n_sel=n_sel, k_top=k_top),
        grid=(n_qt, b),
        in_specs=[pl.BlockSpec((512, tq), lambda qi, bi: (0, bi * n_qt + qi)),
                  pl.BlockSpec((1, nb, 128), lambda qi, bi: (bi, 0, 0)),
                  pl.BlockSpec((1, nb, 128), lambda qi, bi: (bi, 0, 0)),
                  pl.BlockSpec((ATT_KV_HEADS, 1, nb, ATT_REP * tq), lambda qi, bi: (0, qi, 0, 0)),
                  pl.BlockSpec(msel_t.shape, lambda qi, bi: (0, 0)),
                  pl.BlockSpec((1, 128, tq), lambda qi, bi: (qi, 0, 0))],
        out_specs=[pl.BlockSpec((1, tq, 512), lambda qi, bi: (bi, qi, 0)),
                   pl.BlockSpec((1, ATT_KV_HEADS, 128, tq), lambda qi, bi: (bi, 0, 0, qi))],
        out_shape=[jax.ShapeDtypeStruct((b, t, 512), F32), jax.ShapeDtypeStruct((b, ATT_KV_HEADS, 128, t), BF16)],
        compiler_params=_params(("parallel", "parallel")),
        name="cmp_select_t",
    )(qt, kc, vc, tab, msel_t, ovr_t)


def _flash_init(m_ref, l_ref, acc_ref):
    m_ref[...] = jnp.full(m_ref.shape, NEG, F32)
    l_ref[...] = jnp.zeros(l_ref.shape, F32)
    acc_ref[...] = jnp.zeros(acc_ref.shape, F32)


def _flash_step_t(s_t, v_t, m_ref, l_ref, acc_ref):
    m_old = m_ref[...]
    m_new = jnp.maximum(m_old, jnp.max(s_t, axis=0, keepdims=True))
    alpha = jnp.exp(m_old - m_new)
    p = jnp.exp(s_t - m_new)
    l_ref[...] = alpha * l_ref[...] + jnp.sum(p, axis=0, keepdims=True)
    acc_ref[...] = alpha * acc_ref[...] + _dot(v_t, p.astype(BF16))
    m_ref[...] = m_new


def _group_queries_t(qt, g, tq):
    row = lax.broadcasted_iota(jnp.int32, (LANES, tq), 0)
    mine = (row >= g * HEAD_DIM) & (row < (g + 1) * HEAD_DIM)
    zero = jnp.zeros((LANES, tq), qt.dtype)
    return jnp.concatenate([jnp.where(mine, qt[r * 128:(r + 1) * 128, :], zero) for r in range(ATT_REP)], axis=1)


def _unpack_heads_t(o_t, g, tq):
    rows = jnp.concatenate([o_t[:, r * tq:(r + 1) * tq].T for r in range(ATT_REP)], axis=0)
    return _pack_heads(rows, g, tq)


def _gate_chunks(gd, gexp):
    sig = 1.0 / (1.0 + jnp.exp(-gd))
    return _split_dot(sig, gexp, 3)


def _combine(gx, o_cmp, o_sel, o_win):
    return gx[:, 0:256] * o_cmp + gx[:, 256:512] * o_sel + gx[:, 512:768] * o_win


def _prompt_attn_kernel(qt_ref, pen_ref, oc_ref, gd_ref, ksa_ref, vst_ref, kw_ref, vwt_ref, a_ref, gexp_ref,
                        o_ref, m_ref, l_ref, acc_ref, qa_s, *, tq):
    qi = pl.program_id(1)
    g = pl.program_id(2)
    qa_s[0:128, :] = _group_queries_t(qt_ref[...], g, tq)
    qa_s[128:256, :] = jnp.concatenate([pen_ref[0, 0]] * ATT_REP, axis=1)

    def sel_tile(j, bias):
        start = pl.multiple_of(j * tq, tq)
        s = _dot(ksa_ref[0, pl.ds(start, tq), :], qa_s[...])
        if bias is not None:
            s = s + bias
        _flash_step_t(s, vst_ref[:, pl.ds(start, tq)], m_ref, l_ref, acc_ref)

    def win_tile(j, bias):
        start = pl.multiple_of(j * tq, tq)
        s = _dot(kw_ref[0, pl.ds(start, tq), :], qa_s[0:128, :]) + bias
        _flash_step_t(s, vwt_ref[:, pl.ds(start, tq)], m_ref, l_ref, acc_ref)

    def near(tile_fn, n_tiles):
        def body(i, carry):
            tile_fn(qi - i, a_ref[0, i])
            return carry
        lax.fori_loop(0, jnp.minimum(qi + 1, n_tiles), body, 0)

    _flash_init(m_ref, l_ref, acc_ref)
    near(sel_tile, 2)

    def far(j, carry):
        sel_tile(j, None)
        return carry
    lax.fori_loop(0, qi - 1, far, 0)
    o_sel = _unpack_heads_t(acc_ref[...] / l_ref[...], g, tq)

    _flash_init(m_ref, l_ref, acc_ref)
    near(win_tile, 3)
    o_win = _unpack_heads_t(acc_ref[...] / l_ref[...], g, tq)

    gx = _gate_chunks(gd_ref[0], gexp_ref[g])
    o_ref[0] = _combine(gx, oc_ref[0], o_sel, o_win).astype(BF16)


def _prompt_attn(qt, pen_t, o_cmp, gd, ksa, vst, kwb, vwt, atab, gexp, *, tq):
    b, t, _ = ksa.shape
    n_qt = t // tq
    gblk = lambda w: pl.BlockSpec((1, tq, w), lambda bi, qi, g: (bi, qi, g))
    seq = lambda w: pl.BlockSpec((1, t, w), lambda bi, qi, g: (bi, 0, 0))
    seq_t = pl.BlockSpec((128, t), lambda bi, qi, g: (0, bi))
    return pl.pallas_call(
        functools.partial(_prompt_attn_kernel, tq=tq),
        grid=(b, n_qt, ATT_KV_HEADS),
        in_specs=[pl.BlockSpec((512, tq), lambda bi, qi, g: (0, bi * n_qt + qi)),
                  pl.BlockSpec((1, 1, 128, tq), lambda bi, qi, g: (bi, g, 0, qi)),
                  gblk(256),
                  pl.BlockSpec((1, tq, 128), lambda bi, qi, g: (bi, qi, 0)),
                  seq(256), seq_t, seq(128), seq_t,
                  pl.BlockSpec((1,) + atab.shape[1:], lambda bi, qi, g: (g, 0, 0, 0)),
                  pl.BlockSpec(gexp.shape, lambda bi, qi, g: (0, 0, 0))],
        out_specs=gblk(256),
        out_shape=jax.ShapeDtypeStruct((b, t, 512), BF16),
        scratch_shapes=[pltpu.VMEM((1, ATT_REP * tq), F32), pltpu.VMEM((1, ATT_REP * tq), F32),
                        pltpu.VMEM((128, ATT_REP * tq), F32), pltpu.VMEM((256, ATT_REP * tq), BF16)],
        compiler_params=_params(("parallel", "arbitrary", "arbitrary")),
        name="prompt_attn",
    )(qt, pen_t, o_cmp, gd, ksa, vst, kwb, vwt, atab, gexp)


def _softmax_two(s_a, s_b, v_a, v_b):
    m = jnp.maximum(jnp.max(s_a, axis=-1, keepdims=True), jnp.max(s_b, axis=-1, keepdims=True))
    p_a = jnp.exp(s_a - m)
    p_b = jnp.exp(s_b - m)
    l = jnp.sum(p_a, axis=-1, keepdims=True) + jnp.sum(p_b, axis=-1, keepdims=True)
    return (_dot(p_a.astype(BF16), v_a) + _dot(p_b.astype(BF16), v_b)) / l


def _sample_attn_kernel(pt_ref, q_ref, pen_ref, oc_ref, gd_ref, ck_ref, cv_ref, ksn_ref, vsn_ref,
                        cw_ref, cvw_ref, kwn_ref, vwn_ref, kwf_ref, vwf_ref, eall_ref, tsel_ref, tnew_ref,
                        twin_ref, gexp_ref, o_ref, kwo_ref, vwo_ref, kbuf, vbuf, ksem, vsem,
                        *, n_pages, n_new):
    slot = _gather_pipeline(pt_ref, (ck_ref, cv_ref), (kbuf, vbuf), (ksem, vsem), n_pages, PAGE_SIZE)
    tq = TQS
    q = q_ref[0]
    pen = pen_ref[0]
    kb = kbuf[slot].astype(BF16)
    vb = vbuf[slot].astype(BF16)
    kwb = cw_ref[0].astype(BF16)
    vwb = cvw_ref[0].astype(BF16)
    o_sel, o_win = [], []
    for g in range(ATT_KV_HEADS):
        qg = _group_queries(q, g, tq)
        pg = jnp.concatenate([pen[:, g * 128:(g + 1) * 128]] * ATT_REP, axis=0)
        s_past = _dot_nt(qg, kb) + _dot(pg, eall_ref[...]) + tsel_ref[g, 0]
        s_new = _dot_nt(qg, ksn_ref[0]) + tnew_ref[g, 0]
        o_sel.append(_softmax_two(s_past, s_new, vb, vsn_ref[0]))
        s_buf = _dot_nt(qg, kwb) + twin_ref[g, 0]
        s_new = _dot_nt(qg, kwn_ref[0]) + tnew_ref[g, 0]
        o_win.append(_softmax_two(s_buf, s_new, vwb, vwn_ref[0]))
    for g in range(ATT_KV_HEADS):
        gx = _gate_chunks(gd_ref[0], gexp_ref[g])
        o = _combine(gx, oc_ref[0, :, g * 256:(g + 1) * 256], _pack_heads(o_sel[g], g, tq), _pack_heads(o_win[g], g, tq))
        o_ref[0, :, g * 256:(g + 1) * 256] = o.astype(BF16)
    keep = WINDOW - n_new
    kwo_ref[0, pl.ds(0, keep), :] = cw_ref[0, pl.ds(n_new, keep), :]
    kwo_ref[0, pl.ds(keep, n_new), :] = kwf_ref[0, pl.ds(0, n_new), :]
    vwo_ref[0, pl.ds(0, keep), :] = cvw_ref[0, pl.ds(n_new, keep), :]
    vwo_ref[0, pl.ds(keep, n_new), :] = vwf_ref[0, pl.ds(0, n_new), :]


def _sample_attn(page_table, q, pen, o_cmp, gd, ck, cv, ksn, vsn, cw, cvw, kwn, vwn, kwf, vwf,
                 eall, tsel, tnew, twin, gexp, *, n_new):
    bs, n_pages = page_table.shape
    past = n_pages * PAGE_SIZE
    per_b = lambda a: pl.BlockSpec((1,) + a.shape[1:], lambda i, pt: (i,) + (0,) * (a.ndim - 1))
    full = lambda a: pl.BlockSpec(a.shape, lambda i, pt: (0,) * a.ndim)
    anyspec = pl.BlockSpec(memory_space=pl.ANY)
    ins = [q, pen, o_cmp, gd, ck, cv, ksn, vsn, cw, cvw, kwn, vwn, kwf, vwf, eall, tsel, tnew, twin, gexp]
    specs = [per_b(q), per_b(pen), per_b(o_cmp), per_b(gd), anyspec, anyspec, per_b(ksn), per_b(vsn),
             per_b(cw), per_b(cvw), per_b(kwn), per_b(vwn), per_b(kwf), per_b(vwf),
             full(eall), full(tsel), full(tnew), full(twin), full(gexp)]
    win = pl.BlockSpec((1, WINDOW, 128), lambda i, pt: (i, 0, 0))
    return pl.pallas_call(
        functools.partial(_sample_attn_kernel, n_pages=n_pages, n_new=n_new),
        grid_spec=pltpu.PrefetchScalarGridSpec(
            num_scalar_prefetch=1,
            grid=(bs,),
            in_specs=specs,
            out_specs=[pl.BlockSpec((1, TQS, 512), lambda i, pt: (i, 0, 0)), win, win],
            scratch_shapes=[pltpu.VMEM((2, past, 128), F32), pltpu.VMEM((2, past, 128), F32),
                            pltpu.SemaphoreType.DMA((2,)), pltpu.SemaphoreType.DMA((2,))]),
        out_shape=[jax.ShapeDtypeStruct((bs, TQS, 512), BF16),
                   jax.ShapeDtypeStruct((bs, WINDOW, 128), F32), jax.ShapeDtypeStruct((bs, WINDOW, 128), F32)],
        compiler_params=_params(("arbitrary",)),
        name="sample_attn",
    )(page_table, *ins)


def _lane_pair(cols, h0, h1, rows):
    lane = lax.broadcasted_iota(jnp.int32, (rows, LANES), 1)
    a = jnp.broadcast_to(cols[:, h0:h0 + 1], (rows, LANES))
    b = jnp.broadcast_to(cols[:, h1:h1 + 1], (rows, LANES))
    return jnp.where(lane < HEAD_DIM, a, b)


def _ssd_kernel(xbc_ref, z_ref, gd_ref, carry_ref, h0_ref, cw_ref, cb_ref, dtb_ref, arow_ref, dskip_ref,
                gnorm_ref, ltri_ref, y_ref, hout_ref, h_s, xfull, *, rows, n_valid):
    c = pl.program_id(1)
    halo = 8

    @pl.when(c == 0)
    def _():
        h_s[...] = h0_ref[0]
        xfull[0:halo, :] = carry_ref[0]

    xfull[halo:halo + rows, :] = xbc_ref[0]
    conv = cb_ref[...]
    for k in range(SSD_CONV):
        conv = conv + xfull[pl.ds(halo - (SSD_CONV - 1) + k, rows), :] * cw_ref[k:k + 1, :]
    tail = xfull[rows:rows + halo, :]
    xfull[0:halo, :] = tail
    xc = _silu(conv)
    xs = xc[:, 0:D_SSD]
    bm = xc[:, D_SSD:D_SSD + 128]
    cm = xc[:, D_SSD + 128:D_SSD + 256]

    lane = lax.broadcasted_iota(jnp.int32, (rows, LANES), 1)
    rowi = lax.broadcasted_iota(jnp.int32, (rows, LANES), 0)
    t = gd_ref[0] + dtb_ref[...]
    sp = jnp.maximum(t, 0.0) + jnp.log(1.0 + jnp.exp(-jnp.abs(t)))
    dt = jnp.where((lane >= DT_LANE) & (lane < DT_LANE + SSD_HEADS) & (rowi < n_valid), sp, 0.0)
    a = dt * arow_ref[...]
    acum = _split_dot_left(ltri_ref[...], a)
    acum_t = acum.T
    a_last = acum[rows - 1:rows, :]
    to_end = jnp.exp(a_last - acum)
    eac = jnp.exp(acum)
    dec = jnp.exp(a_last)

    li = lax.broadcasted_iota(jnp.int32, (rows, rows), 0)
    si = lax.broadcasted_iota(jnp.int32, (rows, rows), 1)
    causal = li >= si
    bmb = bm.astype(BF16)
    rowp = lax.broadcasted_iota(jnp.int32, (LANES, LANES), 0)
    cbs = []
    cmask = []
    for g in range(2):
        cg = jnp.where((lane >= g * 64) & (lane < (g + 1) * 64), cm, 0.0).astype(BF16)
        cmask.append(cg)
        cbs.append(_dot_nt(cg, bmb))
    for k in range(SSD_HEADS // 2):
        g = k // 2
        h0, h1 = DT_LANE + 2 * k, DT_LANE + 2 * k + 1
        xs_p = xs[:, k * 128:(k + 1) * 128]
        xdt = xs_p * _lane_pair(dt, h0, h1, rows)
        xdt_b = xdt.astype(BF16)
        ys = []
        for h in (h0, h1):
            seg = jnp.broadcast_to(acum[:, h:h + 1], (rows, rows)) - acum_t[h:h + 1, :]
            decay = jnp.where(causal, jnp.exp(jnp.where(causal, seg, 0.0)), 0.0)
            ys.append(_dot((cbs[g] * decay).astype(BF16), xdt_b))
        y = jnp.where(lane < HEAD_DIM, ys[0], ys[1])
        hp = h_s[k * 128:(k + 1) * 128, :]
        y = y + _dot_nt(cmask[g], hp.astype(BF16)) * _lane_pair(eac, h0, h1, rows)
        y = y + dskip_ref[:, k * 128:(k + 1) * 128] * xs_p
        xw = xdt * _lane_pair(to_end, h0, h1, rows)
        st = _dot(xw.T.astype(BF16), bmb)
        dfac = jnp.where(rowp < HEAD_DIM, dec[:, h0:h0 + 1], dec[:, h1:h1 + 1])
        h_s[k * 128:(k + 1) * 128, :] = hp * dfac + st
        xfull_y = y * _silu(z_ref[0, :, k * 128:(k + 1) * 128])
        y_ref[0, :, k * 128:(k + 1) * 128] = xfull_y.astype(y_ref.dtype)

    for g in range(2):
        yg = y_ref[0, :, g * 256:(g + 1) * 256].astype(F32)
        ms = jnp.mean(yg * yg, axis=-1, keepdims=True)
        y_ref[0, :, g * 256:(g + 1) * 256] = (yg * lax.rsqrt(ms + RMS_EPS)
                                               * gnorm_ref[:, g * 256:(g + 1) * 256]).astype(y_ref.dtype)

    @pl.when(c == pl.num_programs(1) - 1)
    def _():
        half = D_SSD // 2
        hout_ref[0, 0:half, :] = h_s[0:half, 0:SSD_STATE]
        hout_ref[0, half:D_SSD, :] = h_s[half:D_SSD, SSD_STATE:2 * SSD_STATE]


def _split_dot_left(tri, a):
    acc = None
    rem = a
    for _ in range(3):
        piece = rem.astype(BF16)
        rem = rem - piece.astype(F32)
        t = _dot(tri, piece)
        acc = t if acc is None else acc + t
    return acc


def _ssd(xbc, z, gd, carry, h0, wts, *, n_valid):
    b, l, _ = xbc.shape
    rows = SSD_CHUNK
    nc = l // rows
    blk = lambda w: pl.BlockSpec((1, rows, w), lambda bi, ci: (bi, ci, 0))
    per_b = lambda a: pl.BlockSpec((1,) + a.shape[1:], lambda bi, ci: (bi,) + (0,) * (a.ndim - 1))
    full = lambda a: pl.BlockSpec(a.shape, lambda bi, ci: (0,) * a.ndim)
    consts = [wts["conv_w"], wts["conv_b"], wts["dtb"], wts["arow"], wts["dskip"], wts["gnorm"], wts["ltri"]]
    return pl.pallas_call(
        functools.partial(_ssd_kernel, rows=rows, n_valid=n_valid),
        grid=(b, nc),
        in_specs=[blk(768), blk(512), blk(128), per_b(carry), per_b(h0)] + [full(a) for a in consts],
        out_specs=[blk(512), pl.BlockSpec((1, D_SSD, SSD_STATE), lambda bi, ci: (bi, 0, 0))],
        out_shape=[jax.ShapeDtypeStruct((b, l, 512), F32), jax.ShapeDtypeStruct((b, D_SSD, SSD_STATE), F32)],
        scratch_shapes=[pltpu.VMEM((512, 128), F32), pltpu.VMEM((rows + 8, 768), F32)],
        compiler_params=_params(("parallel", "arbitrary")),
        name="ssd",
    )(xbc, z, gd, carry, h0, *consts)


def _finish_kernel(x_ref, oa_ref, ys_ref, woa_ref, wos_ref, gf_ref, wg_ref, wu_ref, wd_ref,
                   y_ref, h_s, u_s, acc_s):
    f = pl.program_id(1)

    @pl.when(f == 0)
    def _():
        h = x_ref[...] + _dot(oa_ref[...], woa_ref[...]) + _dot(ys_ref[...].astype(BF16), wos_ref[...])
        h_s[...] = h
        ms = jnp.mean(h * h, axis=-1, keepdims=True)
        u_s[...] = (h * lax.rsqrt(ms + RMS_EPS) * gf_ref[...]).astype(BF16)
        acc_s[...] = jnp.zeros(acc_s.shape, F32)

    u = u_s[...]
    act = _silu(_dot(u, wg_ref[...])) * _dot(u, wu_ref[...])
    acc_s[...] += _dot(act.astype(BF16), wd_ref[...])

    @pl.when(f == pl.num_programs(1) - 1)
    def _():
        y_ref[...] = h_s[...] + acc_s[...]


def _finish(x2d, o_att, y_ssd, wts):
    n = x2d.shape[0]
    tm = min(512, n)
    nf = 2
    tf = D_FF // nf
    row = lambda w: pl.BlockSpec((tm, w), lambda i, f: (i, 0))
    full = lambda a: pl.BlockSpec(a.shape, lambda i, f: (0,) * a.ndim)
    return pl.pallas_call(
        _finish_kernel,
        grid=(n // tm, nf),
        in_specs=[row(D_MODEL), row(512), row(512), full(wts["wo_att"]), full(wts["wo_ssd"]), full(wts["gffn"]),
                  pl.BlockSpec((D_MODEL, tf), lambda i, f: (0, f)),
                  pl.BlockSpec((D_MODEL, tf), lambda i, f: (0, f)),
                  pl.BlockSpec((tf, D_MODEL), lambda i, f: (f, 0))],
        out_specs=row(D_MODEL),
        out_shape=jax.ShapeDtypeStruct((n, D_MODEL), F32),
        scratch_shapes=[pltpu.VMEM((tm, D_MODEL), F32), pltpu.VMEM((tm, D_MODEL), BF16),
                        pltpu.VMEM((tm, D_MODEL), F32)],
        compiler_params=_params(("parallel", "arbitrary")),
        name="finish",
    )(x2d, o_att, y_ssd, wts["wo_att"], wts["wo_ssd"], wts["gffn"], wts["w_gate"], wts["w_up"], wts["w_down"])


def _pair_perm():
    cols = []
    for r in range(ATT_REP):
        cols += list(range(r * HEAD_DIM, (r + 1) * HEAD_DIM))
        cols += list(range((ATT_REP + r) * HEAD_DIM, (ATT_REP + r + 1) * HEAD_DIM))
    return np.asarray(cols, np.int32)


def _block_ones(n, blk):
    i = np.arange(n)
    return (i[:, None] // blk == i[None, :] // blk).astype(np.float32) / blk


def _prep_weights(norm_mix, w_in, q_norm, k_norm, cmp_pe, cmp_w1, cmp_w2, conv_w, conv_b, dt_bias, a_log,
                  d_skip, ssd_norm, w_out, norm_ffn, w_gate, w_up, w_down):
    perm = _pair_perm()
    w = w_in
    gd = jnp.concatenate([w[:, OFF_GATE:OFF_Z], w[:, OFF_DT:D_IN],
                          jnp.zeros((D_MODEL, 128 - 3 * ATT_HEADS - SSD_HEADS), w.dtype)], axis=1)
    w_r = jnp.concatenate([w[:, :D_ATT][:, perm], w[:, OFF_KV:OFF_GATE], w[:, OFF_Z:OFF_XBC],
                           w[:, OFF_XBC:OFF_DT], gd], axis=1).astype(BF16)
    wts = dict(
        gmix=norm_mix.reshape(1, D_MODEL), w_in=w_r,
        qg=(jnp.tile(q_norm, ATT_HEADS) * (HEAD_DIM ** -0.5)).reshape(1, D_ATT),
        kg=jnp.tile(k_norm, ATT_KV_HEADS).reshape(1, D_KV),
        bq=jnp.asarray(_block_ones(D_ATT, HEAD_DIM), BF16), bk=jnp.asarray(_block_ones(D_KV, HEAD_DIM), BF16))

    def w1_big(w1):
        w1r = w1.reshape(2, CMP_STRIDE, HEAD_DIM, CMP_HIDDEN)
        eye = jnp.eye(ATT_KV_HEADS, dtype=w1.dtype)
        big = jnp.einsum("jsdh,gk->sgdjkh", w1r, eye)
        return big.reshape(CMP_STRIDE * D_KV, 2 * D_KV).astype(BF16)

    def pe_rows(pe):
        per = jnp.broadcast_to(pe.reshape(2, CMP_STRIDE, 1, HEAD_DIM), (2, CMP_STRIDE, ATT_KV_HEADS, HEAD_DIM))
        per = per.reshape(2, CMP_STRIDE * D_KV)
        return jnp.concatenate([per, jnp.zeros((6, CMP_STRIDE * D_KV), pe.dtype)], axis=0).astype(BF16)

    def w2_big(w2):
        eye = jnp.eye(ATT_KV_HEADS, dtype=w2.dtype)
        return jnp.einsum("hd,gk->ghkd", w2, eye).reshape(D_KV, D_KV).astype(BF16)

    wts.update(w1k=w1_big(cmp_w1[0]), w1v=w1_big(cmp_w1[1]), pek=pe_rows(cmp_pe[0]), pev=pe_rows(cmp_pe[1]),
               w2k=w2_big(cmp_w2[0]), w2v=w2_big(cmp_w2[1]))

    pad_lanes = lambda v: jnp.zeros((1, LANES), F32).at[0, DT_LANE:DT_LANE + SSD_HEADS].set(v)
    ltri = np.tril(np.ones((SSD_CHUNK, SSD_CHUNK), np.float32))
    wts.update(conv_w=jnp.concatenate([conv_w, jnp.zeros((4, D_CONV), F32)], axis=0), conv_b=conv_b.reshape(1, D_CONV),
               dtb=pad_lanes(dt_bias), arow=pad_lanes(-jnp.exp(a_log)),
               dskip=jnp.repeat(d_skip, 64).reshape(1, D_SSD), gnorm=ssd_norm.reshape(1, D_SSD),
               ltri=jnp.asarray(ltri, BF16))
    wts.update(wo_att=w_out[:D_ATT].astype(BF16), wo_ssd=w_out[D_ATT:].astype(BF16),
               gffn=norm_ffn.reshape(1, D_MODEL), w_gate=w_gate.astype(BF16), w_up=w_up.astype(BF16),
               w_down=w_down.astype(BF16))
    return wts


def _gate_expand():
    m = np.zeros((ATT_KV_HEADS, LANES, 3 * ATT_REP * HEAD_DIM), np.float32)
    for g in range(ATT_KV_HEADS):
        for r in range(ATT_REP):
            for br in range(3):
                c0 = br * ATT_REP * HEAD_DIM + r * HEAD_DIM
                m[g, g * 3 * ATT_REP + r * 3 + br, c0:c0 + HEAD_DIM] = 1.0
    return jnp.asarray(m, BF16)


def _sel_matrix(n_blk_pad, n_sel):
    m = np.zeros((n_blk_pad, LANES), np.float32)
    for n in range(n_blk_pad - 1):
        for j in {n // 4, (n + 1) // 4}:
            if j < n_sel:
                m[n, j] = 1.0
    return jnp.asarray(m, BF16)


def _override(q_pos, n_sel):
    j = np.arange(LANES)[None, :]
    cur = (q_pos // SEL_BLOCK)[:, None]
    forced = (j == 0) | (j == cur) | (j == cur - 1)
    ovr = np.where(forced, 1e30, np.where(j <= cur, 0.0, -1e30))
    ovr = np.where(j < n_sel, ovr, -1e30)
    return ovr.astype(np.float32)


def _prompt_tables(rel_bias, t):
    n_qt = t // TQ
    pos = np.arange(t)
    nb = t // CMP_STRIDE
    e = CMP_STRIDE * np.arange(nb) + (CMP_BLOCK - 1)
    dist = pos[None, :] - e[:, None]
    cmp_idx = _idx_table(dist, (dist >= 0) & (np.arange(nb)[:, None] < nb - 1))
    cmp_idx = cmp_idx.reshape(nb, n_qt, TQ).transpose(1, 0, 2)
    i = np.arange(TQ)[None, :]
    j = np.arange(TQ)[:, None]
    diag = _idx_table(i - j, i >= j)
    prev = _idx_table(TQ + i - j, np.ones((TQ, TQ), bool))
    prev2 = _idx_table(2 * TQ + i - j, (2 * TQ + i - j) < WINDOW)
    att_idx = np.stack([diag, prev, prev2])
    return _bias_tables(rel_bias, cmp_idx, stack_cols=True), _bias_tables(rel_bias, att_idx, stack_cols=True)


def _prompt_layer(x, wts, rel_bias):
    b, t, _ = x.shape
    n = b * t
    (_, kc, vc, ks, vs, kw, vw, ksa, _, kwb, _, z, xbc, gd, qt, vst, vwt) = _project(x.reshape(n, D_MODEL), wts, t)
    n_sub = t // CMP_STRIDE
    kcmp, vcmp = _compress_prompt(kc.reshape(b, n_sub, CMP_STRIDE * D_KV), vc.reshape(b, n_sub, CMP_STRIDE * D_KV), wts)
    cmp_tab, att_tab = _prompt_tables(rel_bias, t)
    n_sel = t // SEL_BLOCK
    ovr_t = jnp.asarray(_override(np.arange(t), n_sel).reshape(t // TQ, TQ, LANES).transpose(0, 2, 1))
    r3 = lambda a: a.reshape(b, t, a.shape[-1])
    o_cmp, pen_t = _cmp_select_t(qt, kcmp, vcmp, cmp_tab, _sel_matrix(n_sub, n_sel).T, ovr_t,
                                 tq=TQ, n_sel=n_sel, k_top=min(SEL_TOPN, n_sel))
    o_att = _prompt_attn(qt, pen_t, o_cmp, r3(gd), r3(ksa), vst, r3(kwb), vwt, att_tab, _gate_expand(), tq=TQ)
    carry = jnp.zeros((b, 8, D_CONV), F32)
    h0 = jnp.zeros((b, 512, 128), F32)
    y_ssd, h_fin = _ssd(r3(xbc), r3(z), r3(gd), carry, h0, wts, n_valid=SSD_CHUNK)
    y = _finish(x.reshape(n, D_MODEL), o_att.reshape(n, 512), y_ssd.reshape(n, 512), wts)
    wb = min(WINDOW, t)
    kv4 = lambda a: a.reshape(b, t, ATT_KV_HEADS, HEAD_DIM)
    ssm = h_fin.reshape(b, SSD_HEADS, 64, SSD_STATE)
    return y.reshape(b, t, D_MODEL), (kv4(kc), kv4(vc), kv4(ks), kv4(vs), kv4(kw)[:, t - wb:], kv4(vw)[:, t - wb:],
                                      r3(xbc)[:, t - (SSD_CONV - 1):], ssm)


def _sample_tables(rel_bias, past, s_new, n_blk_pad):
    s = np.minimum(np.arange(TQS), s_new - 1)[:, None]
    pos = past + s
    nidx = np.arange(n_blk_pad)[None, :]
    e = CMP_STRIDE * nidx + (CMP_BLOCK - 1)
    cmp_idx = _idx_table(pos - e, (e <= pos) & (nidx < n_blk_pad - 1))[None]
    key = np.arange(past)[None, :]
    sel_idx = _idx_table(pos - key, np.ones((TQS, past), bool))[None]
    jn = np.arange(LANES)[None, :]
    new_idx = _idx_table(s - jn, (jn <= s) & (jn < s_new))[None]
    wi = np.arange(WINDOW)[None, :]
    wdist = pos - (past - WINDOW + wi)
    win_idx = _idx_table(wdist, (wdist >= 0) & (wdist < WINDOW))[None]
    tabs = [_bias_tables(rel_bias, t) for t in (cmp_idx, sel_idx, new_idx, win_idx)]
    return tabs, pos[:, 0]


def _sample_layer(x, c_kc, c_vc, c_ks, c_vs, c_kw, c_vw, s_conv, s_ssm, page_table, wts, rel_bias):
    bs, s_new, _ = x.shape
    n = bs * s_new
    n_pages = page_table.shape[1]
    past = n_pages * PAGE_SIZE
    (q, kc, vc, ks, vs, kw, vw, _, vsb, kwb, vwb, z, xbc, gd, _, _, _) = _project(x.reshape(n, D_MODEL), wts, s_new)
    r3 = lambda a: a.reshape(bs, s_new, a.shape[-1])
    padq = lambda a, rows: jnp.pad(r3(a), ((0, 0), (0, rows - s_new), (0, 0)))
    n_phys = c_kc.shape[0]
    rows = PAGE_SIZE // CMP_STRIDE
    kcmp, vcmp = _compress_sample(page_table, c_kc.reshape(n_phys, rows, CMP_STRIDE * D_KV),
                                  c_vc.reshape(n_phys, rows, CMP_STRIDE * D_KV), wts)
    n_blk_pad = past // CMP_STRIDE
    (cmp_tab, sel_tab, new_tab, win_tab), pos = _sample_tables(rel_bias, past, s_new, n_blk_pad)
    n_sel = past // SEL_BLOCK
    ovr = jnp.asarray(_override(pos, n_sel)[None])
    qp = padq(q, TQS)
    o_cmp, pen = _cmp_select(qp, kcmp, vcmp, cmp_tab, _sel_matrix(n_blk_pad, n_sel), ovr,
                             tq=TQS, n_sel=n_sel, k_top=min(SEL_TOPN - 1, n_sel))
    blk_of_key = np.arange(past)[None, :] // SEL_BLOCK
    eall = jnp.asarray((np.arange(LANES)[:, None] == blk_of_key).astype(np.float32), BF16)
    o_att, kw_new, vw_new = _sample_attn(
        page_table, qp, pen, o_cmp, padq(gd, TQS),
        c_ks.reshape(n_phys, PAGE_SIZE, D_KV), c_vs.reshape(n_phys, PAGE_SIZE, D_KV),
        padq(ks.astype(BF16), LANES), padq(vsb, LANES),
        c_kw.reshape(bs, WINDOW, D_KV), c_vw.reshape(bs, WINDOW, D_KV),
        padq(kwb, LANES), padq(vwb, LANES), padq(kw, 8), padq(vw, 8),
        eall, sel_tab, new_tab, win_tab, _gate_expand(), n_new=s_new)
    carry = jnp.pad(s_conv, ((0, 0), (8 - (SSD_CONV - 1), 0), (0, 0)))
    h0 = s_ssm.reshape(bs, 512, SSD_STATE)
    h0 = jnp.concatenate([h0, h0], axis=-1)
    y_ssd, h_fin = _ssd(padq(xbc, SSD_CHUNK), padq(z, SSD_CHUNK), padq(gd, SSD_CHUNK), carry, h0, wts, n_valid=s_new)
    y = _finish(x.reshape(n, D_MODEL), o_att[:, :s_new].reshape(n, 512), y_ssd[:, :s_new].reshape(n, 512), wts)
    kv4 = lambda a: a.reshape(bs, -1, ATT_KV_HEADS, HEAD_DIM)
    ssm = h_fin.reshape(bs, SSD_HEADS, 64, SSD_STATE)
    conv_state = jnp.concatenate([s_conv, r3(xbc)], axis=1)[:, s_new:]
    return y.reshape(bs, s_new, D_MODEL), (kv4(kc), kv4(vc), kv4(ks), kv4(vs), kv4(kw_new), kv4(vw_new), conv_state, ssm)


def kernel(x_prompt, x_sample, cache_k_cmp, cache_v_cmp, cache_k_sel, cache_v_sel, cache_k_win, cache_v_win,
           state_conv, state_ssm, page_table, norm_mix, w_in, q_norm, k_norm, cmp_pe, cmp_w1, cmp_w2, rel_bias,
           conv_w, conv_b, dt_bias, a_log, d_skip, ssd_norm, w_out, norm_ffn, w_gate, w_up, w_down):
    depth = w_in.shape[0]
    y_p, y_s = x_prompt, x_sample
    p_states, s_states = [], []
    for l in range(depth):
        wts = _prep_weights(norm_mix[l], w_in[l], q_norm[l], k_norm[l], cmp_pe[l], cmp_w1[l], cmp_w2[l],
                            conv_w[l], conv_b[l], dt_bias[l], a_log[l], d_skip[l], ssd_norm[l], w_out[l],
                            norm_ffn[l], w_gate[l], w_up[l], w_down[l])
        y_p, st_p = _prompt_layer(y_p, wts, rel_bias)
        y_s, st_s = _sample_layer(y_s, cache_k_cmp[l], cache_v_cmp[l], cache_k_sel[l], cache_v_sel[l],
                                  cache_k_win[l], cache_v_win[l], state_conv[l], state_ssm[l], page_table,
                                  wts, rel_bias)
        p_states.append(st_p)
        s_states.append(st_s)
    p_out = [jnp.stack(a) for a in zip(*p_states)]
    s_out = [jnp.stack(a) for a in zip(*s_states)]
    return (y_p, y_s, *p_out, *s_out)
```

```python
import functools
import math

import numpy as np
import jax
import jax.numpy as jnp
from jax import lax
from jax.experimental import pallas as pl
from jax.experimental.pallas import tpu as pltpu

F32 = jnp.float32
BF16 = jnp.bfloat16

D_MODEL = 1024
HEAD_DIM = 64
ATT_HEADS = 8
ATT_KV_HEADS = 2
ATT_REP = ATT_HEADS // ATT_KV_HEADS
CMP_BLOCK = 32
CMP_STRIDE = 16
CMP_HIDDEN = 64
SEL_BLOCK = 64
SEL_TOPN = 16
WINDOW = 512
N_BUCKETS = 32
MAX_DISTANCE = 128
PAGE_SIZE = 128
SSD_HEADS = 8
SSD_STATE = 64
SSD_CONV = 4
SSD_CHUNK = 128
D_ATT = ATT_HEADS * HEAD_DIM
D_SSD = SSD_HEADS * 64
D_KV = ATT_KV_HEADS * HEAD_DIM
D_CONV = D_SSD + 2 * 2 * SSD_STATE
D_FF = ((8 * D_MODEL // 3 + 255) // 256) * 256
OFF_KV = D_ATT
OFF_GATE = OFF_KV + 6 * D_KV
OFF_Z = OFF_GATE + 3 * ATT_HEADS
OFF_XBC = OFF_Z + D_SSD
OFF_DT = OFF_XBC + D_CONV
D_IN = OFF_DT + SSD_HEADS
RMS_EPS = 1e-6
NEG = -1e30

C_Q, C_KV, C_Z, C_XBC, C_GD = 0, 512, 1280, 1792, 2560
D_INR = 2688
DT_LANE = 24
LANES = 128
TQ = 256
TQS = 16
VMEM_LIMIT = 48 * 1024 * 1024


def _dot(a, b):
    return jnp.dot(a, b, preferred_element_type=F32)


def _dot_nt(a, b):
    return lax.dot_general(a, b, (((1,), (1,)), ((), ())), preferred_element_type=F32)


def _split_dot(a, b, parts):
    acc = None
    rem = a
    for _ in range(parts):
        piece = rem.astype(BF16)
        rem = rem - piece.astype(F32)
        t = _dot(piece, b)
        acc = t if acc is None else acc + t
    return acc


def _silu(x):
    return x * (1.0 / (1.0 + jnp.exp(-x)))


def _params(sem=None):
    kw = dict(vmem_limit_bytes=VMEM_LIMIT)
    if sem is not None:
        kw["dimension_semantics"] = sem
    return pltpu.CompilerParams(**kw)


KV_NAMES = ("kc", "vc", "ks", "vs", "kw", "vw")
PROJ_OUTPUTS = {
    "prompt": ([(k + "_t", "seq", 128, F32) for k in KV_NAMES]
               + [("kc_b", "row", 128, BF16), ("vc_b", "row", 128, BF16), ("ksa", "row", 256, BF16),
                  ("kw_b", "row", 128, BF16), ("q_t", "col", 512, BF16), ("vs_tb", "col", 128, BF16),
                  ("vw_tb", "col", 128, BF16)]),
    "sample": ([(k, "row", 128, F32) for k in KV_NAMES]
               + [("q", "row", 512, BF16), ("ks_b", "row", 128, BF16), ("vs_b", "row", 128, BF16),
                  ("kw_b", "row", 128, BF16), ("vw_b", "row", 128, BF16),
                  ("kw_t", "col", 128, F32), ("vw_t", "col", 128, F32)]),
}
PROJ_COMMON = [("z", "row", 512, F32), ("xbc", "row", 768, F32), ("gd", "row", 128, F32)]


def _proj_kernel(x_ref, gmix_ref, w_ref, qg_ref, kg_ref, bq_ref, bk_ref, *out_refs, names, tm, t_len):
    out = dict(zip(names, out_refs))
    x = x_ref[...]
    ms = jnp.mean(x * x, axis=-1, keepdims=True)
    u = (x * lax.rsqrt(ms + RMS_EPS) * gmix_ref[...]).astype(BF16)

    def proj(lo, hi):
        return _dot(u, w_ref[:, lo:hi])

    def headnorm(v, b_ref, g_ref):
        msq = _dot((v * v).astype(BF16), b_ref[...])
        return v * lax.rsqrt(msq + RMS_EPS) * g_ref[...]

    def put(name, value):
        if name in out:
            ref = out[name]
            ref[...] = value().astype(ref.dtype).reshape(ref.shape)

    q = headnorm(proj(C_Q, C_Q + 512), bq_ref, qg_ref)
    put("q", lambda: q)
    put("q_t", lambda: q.T)
    kv = {}
    for i, name in enumerate(KV_NAMES):
        v = proj(C_KV + 128 * i, C_KV + 128 * (i + 1))
        kv[name] = headnorm(v, bk_ref, kg_ref) if name in ("ks", "kw") else v
    for name, v in kv.items():
        put(name, lambda v=v: v)
        put(name + "_b", lambda v=v: v)
        put(name + "_t", lambda v=v: v.T)
        put(name + "_tb", lambda v=v: v.T)
    if "ksa" in out:
        row = pl.program_id(0) * tm + lax.broadcasted_iota(jnp.int32, (tm, LANES), 0)
        blk = (row % t_len) // SEL_BLOCK
        lane = lax.broadcasted_iota(jnp.int32, (tm, LANES), 1)
        out["ksa"][:, 0:128] = kv["ks"].astype(BF16)
        out["ksa"][:, 128:256] = jnp.where(lane == blk, 1.0, 0.0).astype(BF16)
    put("z", lambda: proj(C_Z, C_Z + 512))
    put("xbc", lambda: proj(C_XBC, C_XBC + 768))
    put("gd", lambda: proj(C_GD, C_GD + 128))


def _project(x2d, wts, t_len, mode):
    n = x2d.shape[0]
    tm = min(512, n)
    per_seq = max(t_len // tm, 1)
    full = lambda a: pl.BlockSpec(a.shape, lambda i: (0,) * a.ndim)
    ins = [x2d, wts["gmix"], wts["w_in"], wts["qg"], wts["kg"], wts["bq"], wts["bk"]]
    outs = PROJ_OUTPUTS[mode] + PROJ_COMMON
    specs, shapes = [], []
    for _, layout, w, dt in outs:
        if layout == "row":
            specs.append(pl.BlockSpec((tm, w), lambda i: (i, 0)))
            shapes.append(jax.ShapeDtypeStruct((n, w), dt))
        elif layout == "col":
            specs.append(pl.BlockSpec((w, tm), lambda i: (0, i)))
            shapes.append(jax.ShapeDtypeStruct((w, n), dt))
        else:
            specs.append(pl.BlockSpec((1, w, tm), lambda i: (i // per_seq, 0, i % per_seq)))
            shapes.append(jax.ShapeDtypeStruct((n // t_len, w, t_len), dt))
    names = tuple(name for name, _, _, _ in outs)
    res = pl.pallas_call(
        functools.partial(_proj_kernel, names=names, tm=tm, t_len=t_len),
        grid=(n // tm,),
        in_specs=[pl.BlockSpec((tm, D_MODEL), lambda i: (i, 0))] + [full(a) for a in ins[1:]],
        out_specs=specs,
        out_shape=shapes,
        compiler_params=_params(("parallel",)),
        name="proj",
    )(*ins)
    return dict(zip(names, res))


def _bucket_np(dist):
    n = np.maximum(dist, 0)
    max_exact = N_BUCKETS // 2
    nf = np.maximum(n, 1).astype(np.float64)
    large = max_exact + (np.log(nf / max_exact) / math.log(MAX_DISTANCE / max_exact)
                         * (N_BUCKETS - max_exact)).astype(np.int64)
    large = np.minimum(large, N_BUCKETS - 1)
    return np.where(n < max_exact, n, large).astype(np.int32)


def _idx_table(dist, valid):
    return np.where(valid, _bucket_np(dist), -1).astype(np.int32)


def _table_kernel(rb_ref, idx_ref, out_ref):
    h = pl.program_id(0) * ATT_REP + pl.program_id(2)
    idx = idx_ref[0]
    far = rb_ref[N_BUCKETS - 1, h]
    acc = jnp.zeros(idx.shape, F32)
    for b in range(N_BUCKETS - 1):
        acc = jnp.where(idx == b, rb_ref[b, h] - far, acc)
    out_ref[...] = jnp.where(idx < 0, NEG, acc).reshape(out_ref.shape)


def _bias_tables(rel_bias, idx, stack_cols=False):
    k, r, c = idx.shape
    if stack_cols:
        return pl.pallas_call(
            _table_kernel,
            grid=(ATT_KV_HEADS, k, ATT_REP),
            in_specs=[pl.BlockSpec(memory_space=pltpu.SMEM),
                      pl.BlockSpec((1, r, c), lambda g, kk, rr: (kk, 0, 0))],
            out_specs=pl.BlockSpec((1, 1, r, c), lambda g, kk, rr: (g, kk, 0, rr)),
            out_shape=jax.ShapeDtypeStruct((ATT_KV_HEADS, k, r, ATT_REP * c), F32),
            name="bias_table_t",
        )(rel_bias, jnp.asarray(idx))
    out = pl.pallas_call(
        _table_kernel,
        grid=(ATT_KV_HEADS, k, ATT_REP),
        in_specs=[pl.BlockSpec(memory_space=pltpu.SMEM),
                  pl.BlockSpec((1, r, c), lambda g, kk, rr: (kk, 0, 0))],
        out_specs=pl.BlockSpec((1, 1, 1, r, c), lambda g, kk, rr: (g, kk, rr, 0, 0)),
        out_shape=jax.ShapeDtypeStruct((ATT_KV_HEADS, k, ATT_REP, r, c), F32),
        name="bias_table",
    )(rel_bias, jnp.asarray(idx))
    return out.reshape(ATT_KV_HEADS, k, ATT_REP * r, c)


def _compress_core(x, w1, pe, w2):
    return _compress_tail(_dot(x.astype(BF16), w1), w1, pe, w2)


def _compress_tail(u, w1, pe, w2):
    n_sub = u.shape[0]
    upe = _dot(pe, w1)
    nxt = pltpu.roll(u[:, 128:256], n_sub - 1, 0)
    pre = u[:, 0:128] + nxt + upe[0:1, 0:128] + upe[1:2, 128:256]
    return _dot(_silu(pre).astype(BF16), w2)


def _knorm(v, bk, kg):
    msq = _dot((v * v).astype(BF16), bk)
    return v * lax.rsqrt(msq + RMS_EPS) * kg


def _compress_prompt_kernel(kc_ref, vc_ref, w1k_ref, w1v_ref, pek_ref, pev_ref, w2k_ref, w2v_ref,
                            kg_ref, bk_ref, ko_ref, vo_ref):
    kc = _compress_core(kc_ref[0], w1k_ref[...], pek_ref[...], w2k_ref[...])
    ko_ref[0] = _knorm(kc, bk_ref[...], kg_ref[...]).astype(BF16)
    vo_ref[0] = _compress_core(vc_ref[0], w1v_ref[...], pev_ref[...], w2v_ref[...]).astype(BF16)


def _compress_prompt(kc, vc, wts):
    b, n_sub, w = kc.shape
    full = lambda a: pl.BlockSpec(a.shape, lambda i: (0,) * a.ndim)
    consts = [wts["w1k"], wts["w1v"], wts["pek"], wts["pev"], wts["w2k"], wts["w2v"], wts["kg"], wts["bk"]]
    blk = pl.BlockSpec((1, n_sub, w), lambda i: (i, 0, 0))
    oblk = pl.BlockSpec((1, n_sub, 128), lambda i: (i, 0, 0))
    return pl.pallas_call(
        _compress_prompt_kernel,
        grid=(b,),
        in_specs=[blk, blk] + [full(a) for a in consts],
        out_specs=[oblk, oblk],
        out_shape=[jax.ShapeDtypeStruct((b, n_sub, 128), BF16)] * 2,
        compiler_params=_params(("parallel",)),
        name="compress_prompt",
    )(kc, vc, *consts)


def _page_copy(cache_ref, page, buf_ref, slot, p, sem_ref, lane_major):
    if lane_major:
        dst = buf_ref.at[slot, :, pl.ds(pl.multiple_of(p * PAGE_SIZE, PAGE_SIZE), PAGE_SIZE)]
    else:
        dst = buf_ref.at[slot, p]
    return pltpu.make_async_copy(cache_ref.at[page], dst, sem_ref.at[slot])


def _gather_start(pt_ref, b, slot, caches, bufs, sems, n_pages, lane_major):
    def body(p, carry):
        page = pt_ref[b, p]
        for cache_ref, buf_ref, sem_ref in zip(caches, bufs, sems):
            _page_copy(cache_ref, page, buf_ref, slot, p, sem_ref, lane_major).start()
        return carry
    lax.fori_loop(0, n_pages, body, 0)


def _gather_wait(slot, caches, bufs, sems, n_pages, lane_major):
    def body(p, carry):
        for cache_ref, buf_ref, sem_ref in zip(caches, bufs, sems):
            _page_copy(cache_ref, 0, buf_ref, slot, p, sem_ref, lane_major).wait()
        return carry
    lax.fori_loop(0, n_pages, body, 0)


def _gather_pipeline(pt_ref, caches, bufs, sems, n_pages, lane_major):
    b = pl.program_id(0)
    nb = pl.num_programs(0)
    slot = b % 2

    @pl.when(b == 0)
    def _():
        _gather_start(pt_ref, 0, 0, caches, bufs, sems, n_pages, lane_major)

    @pl.when(b + 1 < nb)
    def _():
        _gather_start(pt_ref, b + 1, 1 - slot, caches, bufs, sems, n_pages, lane_major)

    _gather_wait(slot, caches, bufs, sems, n_pages, lane_major)
    return slot


def _compress_paged(buf, slot, rows_s, w1, pe, w2, n_pages):
    def body(p, carry):
        rows_s[pl.ds(pl.multiple_of(p * PAGE_SIZE, PAGE_SIZE), PAGE_SIZE), :] = buf[slot, p].T
        return carry
    lax.fori_loop(0, n_pages, body, 0)
    n_sub = n_pages * PAGE_SIZE // CMP_STRIDE
    u = None
    for s in range(CMP_STRIDE):
        xs = rows_s[pl.ds(s, n_sub, stride=CMP_STRIDE), :].astype(BF16)
        t = _dot(xs, w1[s * D_KV:(s + 1) * D_KV, :])
        u = t if u is None else u + t
    return _compress_tail(u, w1, pe, w2)


def _compress_sample_kernel(pt_ref, ck_ref, cv_ref, w1k_ref, w1v_ref, pek_ref, pev_ref, w2k_ref, w2v_ref,
                            kg_ref, bk_ref, ko_ref, vo_ref, kbuf, vbuf, rows_s, ksem, vsem, *, n_pages):
    slot = _gather_pipeline(pt_ref, (ck_ref, cv_ref), (kbuf, vbuf), (ksem, vsem), n_pages, lane_major=False)
    kc = _compress_paged(kbuf, slot, rows_s, w1k_ref[...], pek_ref[...], w2k_ref[...], n_pages)
    ko_ref[0] = _knorm(kc, bk_ref[...], kg_ref[...]).astype(BF16)
    vo_ref[0] = _compress_paged(vbuf, slot, rows_s, w1v_ref[...], pev_ref[...], w2v_ref[...], n_pages).astype(BF16)


def _compress_sample(page_table, ck, cv, wts):
    bs, n_pages = page_table.shape
    n_sub = n_pages * PAGE_SIZE // CMP_STRIDE
    full = lambda a: pl.BlockSpec(a.shape, lambda i, pt: (0,) * a.ndim)
    consts = [wts["w1k"], wts["w1v"], wts["pek"], wts["pev"], wts["w2k"], wts["w2v"], wts["kg"], wts["bk"]]
    anyspec = pl.BlockSpec(memory_space=pl.ANY)
    oblk = pl.BlockSpec((1, n_sub, 128), lambda i, pt: (i, 0, 0))
    pages = pltpu.VMEM((2, n_pages, D_KV, PAGE_SIZE), F32)
    return pl.pallas_call(
        functools.partial(_compress_sample_kernel, n_pages=n_pages),
        grid_spec=pltpu.PrefetchScalarGridSpec(
            num_scalar_prefetch=1,
            grid=(bs,),
            in_specs=[anyspec, anyspec] + [full(a) for a in consts],
            out_specs=[oblk, oblk],
            scratch_shapes=[pages, pages, pltpu.VMEM((n_pages * PAGE_SIZE, D_KV), F32),
                            pltpu.SemaphoreType.DMA((2,)), pltpu.SemaphoreType.DMA((2,))]),
        out_shape=[jax.ShapeDtypeStruct((bs, n_sub, 128), BF16)] * 2,
        compiler_params=_params(("arbitrary",)),
        name="compress_sample",
    )(page_table, ck, cv, *consts)


def _group_queries(q, g, tq):
    lane = lax.broadcasted_iota(jnp.int32, (tq, LANES), 1)
    mine = (lane >= g * HEAD_DIM) & (lane < (g + 1) * HEAD_DIM)
    zero = jnp.zeros((tq, LANES), q.dtype)
    return jnp.concatenate([jnp.where(mine, q[:, r * 128:(r + 1) * 128], zero) for r in range(ATT_REP)], axis=0)


def _pack_heads(o, g, tq):
    lane = lax.broadcasted_iota(jnp.int32, (tq, LANES), 1)
    first = g == 0
    chunks = []
    for k in range(ATT_REP // 2):
        a = o[2 * k * tq:(2 * k + 1) * tq]
        b = o[(2 * k + 1) * tq:(2 * k + 2) * tq]
        lo = jnp.where(first, a, pltpu.roll(a, HEAD_DIM, 1))
        hi = jnp.where(first, pltpu.roll(b, HEAD_DIM, 1), b)
        chunks.append(jnp.where(lane < HEAD_DIM, lo, hi))
    return jnp.concatenate(chunks, axis=1)


def _cmp_select_kernel(q_ref, kc_ref, vc_ref, tab_ref, msel_ref, ovr_ref, o_ref, pen_ref, *, tq, n_sel, k_top):
    q = q_ref[0]
    kc = kc_ref[0]
    vc = vc_ref[0]
    ovr = ovr_ref[0]
    lane = lax.broadcasted_iota(jnp.int32, (tq, LANES), 1)
    outs = []
    for g in range(ATT_KV_HEADS):
        tab = tab_ref[g, 0]
        s = _dot_nt(_group_queries(q, g, tq), kc) + tab
        m = jnp.max(s, axis=-1, keepdims=True)
        e = jnp.where(tab > 0.5 * NEG, jnp.exp(s - m), 0.0)
        p = e / jnp.maximum(jnp.sum(e, axis=-1, keepdims=True), 1e-30)
        outs.append(_dot(p.astype(BF16), vc))
        imp = p[0:tq]
        for r in range(1, ATT_REP):
            imp = imp + p[r * tq:(r + 1) * tq]
        score = _split_dot(imp, msel_ref[...], 3)
        score = jnp.where(ovr == 0.0, score, ovr)
        rank = jnp.zeros((tq, LANES), F32)
        for j in range(n_sel):
            col = score[:, j:j + 1]
            beats = (col > score) | ((col == score) & (lane > j))
            rank = rank + jnp.where(beats, 1.0, 0.0)
        pen_ref[0, :, g * 128:(g + 1) * 128] = jnp.where(rank < k_top, 0.0, NEG).astype(BF16)
    for g in range(ATT_KV_HEADS):
        o_ref[0, :, g * 256:(g + 1) * 256] = _pack_heads(outs[g], g, tq)


def _cmp_select(q, kc, vc, tab, msel, ovr, *, tq, n_sel, k_top):
    b, t, _ = q.shape
    nb = kc.shape[1]
    n_qt = t // tq
    return pl.pallas_call(
        functools.partial(_cmp_select_kernel, tq=tq, n_sel=n_sel, k_top=k_top),
        grid=(n_qt, b),
        in_specs=[pl.BlockSpec((1, tq, 512), lambda qi, bi: (bi, qi, 0)),
                  pl.BlockSpec((1, nb, 128), lambda qi, bi: (bi, 0, 0)),
                  pl.BlockSpec((1, nb, 128), lambda qi, bi: (bi, 0, 0)),
                  pl.BlockSpec((ATT_KV_HEADS, 1, ATT_REP * tq, nb), lambda qi, bi: (0, qi, 0, 0)),
                  pl.BlockSpec(msel.shape, lambda qi, bi: (0, 0)),
                  pl.BlockSpec((1, tq, 128), lambda qi, bi: (qi, 0, 0))],
        out_specs=[pl.BlockSpec((1, tq, 512), lambda qi, bi: (bi, qi, 0)),
                   pl.BlockSpec((1, tq, 256), lambda qi, bi: (bi, qi, 0))],
        out_shape=[jax.ShapeDtypeStruct((b, t, 512), F32), jax.ShapeDtypeStruct((b, t, 256), BF16)],
        compiler_params=_params(("parallel", "parallel")),
        name="cmp_select",
    )(q, kc, vc, tab, msel, ovr)


def _cmp_select_t_kernel(qt_ref, kc_ref, vc_ref, tab_ref, msel_ref, ovr_ref, o_ref, pen_ref,
                         *, tq, n_sel, k_top):
    qt = qt_ref[...]
    kc = kc_ref[0]
    vct = vc_ref[0].astype(F32).T.astype(BF16)
    ovr = ovr_ref[0, 0:n_sel, :]
    row = lax.broadcasted_iota(jnp.int32, (n_sel, tq), 0)
    for g in range(ATT_KV_HEADS):
        tab = tab_ref[g, 0]
        s = _dot(kc, _group_queries_t(qt, g, tq)) + tab
        m = jnp.max(s, axis=0, keepdims=True)
        e = jnp.where(tab > 0.5 * NEG, jnp.exp(s - m), 0.0)
        p = e / jnp.maximum(jnp.sum(e, axis=0, keepdims=True), 1e-30)
        o_ref[0, :, g * 256:(g + 1) * 256] = _unpack_heads_t(_dot(vct, p.astype(BF16)), g, tq)
        imp = p[:, 0:tq]
        for r in range(1, ATT_REP):
            imp = imp + p[:, r * tq:(r + 1) * tq]
        score = _split_dot_left(msel_ref[...], imp)[0:n_sel, :]
        score = jnp.where(ovr == 0.0, score, ovr)
        rank = jnp.zeros((n_sel, tq), F32)
        for j in range(n_sel):
            cand = score[j:j + 1, :]
            beats = (cand > score) | ((cand == score) & (row > j))
            rank = rank + jnp.where(beats, 1.0, 0.0)
        pen_ref[0, g, 0:n_sel, :] = jnp.where(rank < k_top, 0.0, NEG).astype(BF16)
        pen_ref[0, g, n_sel:LANES, :] = jnp.zeros((LANES - n_sel, tq), BF16)


def _cmp_select_t(qt, kc, vc, tab, msel_t, ovr_t, *, tq, n_sel, k_top):
    b, nb, _ = kc.shape
    n_qt = tab.shape[1]
    t = n_qt * tq
    return pl.pallas_call(
        functools.partial(_cmp_select_t_kernel, tq=tq, n_sel=n_sel, k_top=k_top),
        grid=(n_qt, b),
        in_specs=[pl.BlockSpec((512, tq), lambda qi, bi: (0, bi * n_qt + qi)),
                  pl.BlockSpec((1, nb, 128), lambda qi, bi: (bi, 0, 0)),
                  pl.BlockSpec((1, nb, 128), lambda qi, bi: (bi, 0, 0)),
                  pl.BlockSpec((ATT_KV_HEADS, 1, nb, ATT_REP * tq), lambda qi, bi: (0, qi, 0, 0)),
                  pl.BlockSpec(msel_t.shape, lambda qi, bi: (0, 0)),
                  pl.BlockSpec((1, 128, tq), lambda qi, bi: (qi, 0, 0))],
        out_specs=[pl.BlockSpec((1, tq, 512), lambda qi, bi: (bi, qi, 0)),
                   pl.BlockSpec((1, ATT_KV_HEADS, 128, tq), lambda qi, bi: (bi, 0, 0, qi))],
        out_shape=[jax.ShapeDtypeStruct((b, t, 512), F32), jax.ShapeDtypeStruct((b, ATT_KV_HEADS, 128, t), BF16)],
        compiler_params=_params(("parallel", "parallel")),
        name="cmp_select_t",
    )(qt, kc, vc, tab, msel_t, ovr_t)


def _flash_init(m_ref, l_ref, acc_ref):
    m_ref[...] = jnp.full(m_ref.shape, NEG, F32)
    l_ref[...] = jnp.zeros(l_ref.shape, F32)
    acc_ref[...] = jnp.zeros(acc_ref.shape, F32)


def _flash_step_t(s_t, v_t, m_ref, l_ref, acc_ref):
    m_old = m_ref[...]
    m_new = jnp.maximum(m_old, jnp.max(s_t, axis=0, keepdims=True))
    alpha = jnp.exp(m_old - m_new)
    p = jnp.exp(s_t - m_new)
    l_ref[...] = alpha * l_ref[...] + jnp.sum(p, axis=0, keepdims=True)
    acc_ref[...] = alpha * acc_ref[...] + _dot(v_t, p.astype(BF16))
    m_ref[...] = m_new


def _group_queries_t(qt, g, tq):
    row = lax.broadcasted_iota(jnp.int32, (LANES, tq), 0)
    mine = (row >= g * HEAD_DIM) & (row < (g + 1) * HEAD_DIM)
    zero = jnp.zeros((LANES, tq), qt.dtype)
    return jnp.concatenate([jnp.where(mine, qt[r * 128:(r + 1) * 128, :], zero) for r in range(ATT_REP)], axis=1)


def _unpack_heads_t(o_t, g, tq):
    rows = jnp.concatenate([o_t[:, r * tq:(r + 1) * tq].T for r in range(ATT_REP)], axis=0)
    return _pack_heads(rows, g, tq)


def _gate_chunks(gd, gexp):
    sig = 1.0 / (1.0 + jnp.exp(-gd))
    return _split_dot(sig, gexp, 3)


def _combine(gx, o_cmp, o_sel, o_win):
    return gx[:, 0:256] * o_cmp + gx[:, 256:512] * o_sel + gx[:, 512:768] * o_win


def _prompt_attn_kernel(qt_ref, pen_ref, oc_ref, gd_ref, ksa_ref, vst_ref, kw_ref, vwt_ref, a_ref, gexp_ref,
                        o_ref, m_ref, l_ref, acc_ref, qa_s, *, tq):
    qi = pl.program_id(1)
    g = pl.program_id(2)
    qa_s[0:128, :] = _group_queries_t(qt_ref[...], g, tq)
    qa_s[128:256, :] = jnp.concatenate([pen_ref[0, 0]] * ATT_REP, axis=1)

    def sel_tile(j, bias):
        start = pl.multiple_of(j * tq, tq)
        s = _dot(ksa_ref[0, pl.ds(start, tq), :], qa_s[...])
        if bias is not None:
            s = s + bias
        _flash_step_t(s, vst_ref[:, pl.ds(start, tq)], m_ref, l_ref, acc_ref)

    def win_tile(j, bias):
        start = pl.multiple_of(j * tq, tq)
        s = _dot(kw_ref[0, pl.ds(start, tq), :], qa_s[0:128, :]) + bias
        _flash_step_t(s, vwt_ref[:, pl.ds(start, tq)], m_ref, l_ref, acc_ref)

    def near(tile_fn, n_tiles):
        def body(i, carry):
            tile_fn(qi - i, a_ref[0, i])
            return carry
        lax.fori_loop(0, jnp.minimum(qi + 1, n_tiles), body, 0)

    _flash_init(m_ref, l_ref, acc_ref)
    near(sel_tile, 2)

    def far(j, carry):
        sel_tile(j, None)
        return carry
    lax.fori_loop(0, qi - 1, far, 0)
    o_sel = _unpack_heads_t(acc_ref[...] / l_ref[...], g, tq)

    _flash_init(m_ref, l_ref, acc_ref)
    near(win_tile, 3)
    o_win = _unpack_heads_t(acc_ref[...] / l_ref[...], g, tq)

    gx = _gate_chunks(gd_ref[0], gexp_ref[g])
    o_ref[0] = _combine(gx, oc_ref[0], o_sel, o_win).astype(BF16)


def _prompt_attn(qt, pen_t, o_cmp, gd, ksa, vst, kwb, vwt, atab, gexp, *, tq):
    b, t, _ = ksa.shape
    n_qt = t // tq
    gblk = lambda w: pl.BlockSpec((1, tq, w), lambda bi, qi, g: (bi, qi, g))
    seq = lambda w: pl.BlockSpec((1, t, w), lambda bi, qi, g: (bi, 0, 0))
    seq_t = pl.BlockSpec((128, t), lambda bi, qi, g: (0, bi))
    return pl.pallas_call(
        functools.partial(_prompt_attn_kernel, tq=tq),
        grid=(b, n_qt, ATT_KV_HEADS),
        in_specs=[pl.BlockSpec((512, tq), lambda bi, qi, g: (0, bi * n_qt + qi)),
                  pl.BlockSpec((1, 1, 128, tq), lambda bi, qi, g: (bi, g, 0, qi)),
                  gblk(256),
                  pl.BlockSpec((1, tq, 128), lambda bi, qi, g: (bi, qi, 0)),
                  seq(256), seq_t, seq(128), seq_t,
                  pl.BlockSpec((1,) + atab.shape[1:], lambda bi, qi, g: (g, 0, 0, 0)),
                  pl.BlockSpec(gexp.shape, lambda bi, qi, g: (0, 0, 0))],
        out_specs=gblk(256),
        out_shape=jax.ShapeDtypeStruct((b, t, 512), BF16),
        scratch_shapes=[pltpu.VMEM((1, ATT_REP * tq), F32), pltpu.VMEM((1, ATT_REP * tq), F32),
                        pltpu.VMEM((128, ATT_REP * tq), F32), pltpu.VMEM((256, ATT_REP * tq), BF16)],
        compiler_params=_params(("parallel", "arbitrary", "arbitrary")),
        name="prompt_attn",
    )(qt, pen_t, o_cmp, gd, ksa, vst, kwb, vwt, atab, gexp)


def _softmax_two(s_a, s_b, vt_a, v_b):
    m = jnp.maximum(jnp.max(s_a, axis=-1, keepdims=True), jnp.max(s_b, axis=-1, keepdims=True))
    p_a = jnp.exp(s_a - m)
    p_b = jnp.exp(s_b - m)
    l = jnp.sum(p_a, axis=-1, keepdims=True) + jnp.sum(p_b, axis=-1, keepdims=True)
    return (_dot_nt(p_a.astype(BF16), vt_a) + _dot(p_b.astype(BF16), v_b)) / l


def _slide_window(buf_t, new_t, n_new):
    w = buf_t.shape[1]
    rolled = pltpu.roll(buf_t, w - n_new, 1)
    lane = lax.broadcasted_iota(jnp.int32, (LANES, LANES), 1)
    last = jnp.where(lane >= LANES - n_new, new_t, rolled[:, w - LANES:w])
    return jnp.concatenate([rolled[:, 0:w - LANES], last], axis=1)


def _sample_attn_kernel(pt_ref, q_ref, pen_ref, oc_ref, gd_ref, ck_ref, cv_ref, ksn_ref, vsn_ref,
                        cw_ref, cvw_ref, kwn_ref, vwn_ref, kwt_ref, vwt_ref, eall_ref, tsel_ref, tnew_ref,
                        twin_ref, gexp_ref, o_ref, kwo_ref, vwo_ref, kbuf, vbuf, ksem, vsem,
                        *, n_pages, n_new):
    slot = _gather_pipeline(pt_ref, (ck_ref, cv_ref), (kbuf, vbuf), (ksem, vsem), n_pages, lane_major=True)
    tq = TQS
    q = q_ref[0]
    pen = pen_ref[0]
    kb = kbuf[slot].astype(BF16)
    vb = vbuf[slot].astype(BF16)
    kwb = cw_ref[0].astype(BF16)
    vwb = cvw_ref[0].astype(BF16)
    o_sel, o_win = [], []
    for g in range(ATT_KV_HEADS):
        qg = _group_queries(q, g, tq)
        pg = jnp.concatenate([pen[:, g * 128:(g + 1) * 128]] * ATT_REP, axis=0)
        s_past = _dot(qg, kb) + _dot(pg, eall_ref[...]) + tsel_ref[g, 0]
        s_new = _dot_nt(qg, ksn_ref[0]) + tnew_ref[g, 0]
        o_sel.append(_softmax_two(s_past, s_new, vb, vsn_ref[0]))
        s_buf = _dot(qg, kwb) + twin_ref[g, 0]
        s_new = _dot_nt(qg, kwn_ref[0]) + tnew_ref[g, 0]
        o_win.append(_softmax_two(s_buf, s_new, vwb, vwn_ref[0]))
    for g in range(ATT_KV_HEADS):
        gx = _gate_chunks(gd_ref[0], gexp_ref[g])
        o = _combine(gx, oc_ref[0, :, g * 256:(g + 1) * 256], _pack_heads(o_sel[g], g, tq), _pack_heads(o_win[g], g, tq))
        o_ref[0, :, g * 256:(g + 1) * 256] = o.astype(BF16)
    kwo_ref[0] = _slide_window(cw_ref[0], kwt_ref[0], n_new)
    vwo_ref[0] = _slide_window(cvw_ref[0], vwt_ref[0], n_new)


def _sample_attn(page_table, q, pen, o_cmp, gd, ck, cv, ksn, vsn, cw, cvw, kwn, vwn, kwt, vwt,
                 eall, tsel, tnew, twin, gexp, *, n_new):
    bs, n_pages = page_table.shape
    past = n_pages * PAGE_SIZE
    per_b = lambda a: pl.BlockSpec((1,) + a.shape[1:], lambda i, pt: (i,) + (0,) * (a.ndim - 1))
    full = lambda a: pl.BlockSpec(a.shape, lambda i, pt: (0,) * a.ndim)
    anyspec = pl.BlockSpec(memory_space=pl.ANY)
    ins = [q, pen, o_cmp, gd, ck, cv, ksn, vsn, cw, cvw, kwn, vwn, kwt, vwt, eall, tsel, tnew, twin, gexp]
    specs = [per_b(q), per_b(pen), per_b(o_cmp), per_b(gd), anyspec, anyspec, per_b(ksn), per_b(vsn),
             per_b(cw), per_b(cvw), per_b(kwn), per_b(vwn), per_b(kwt), per_b(vwt),
             full(eall), full(tsel), full(tnew), full(twin), full(gexp)]
    win = pl.BlockSpec((1, 128, WINDOW), lambda i, pt: (i, 0, 0))
    return pl.pallas_call(
        functools.partial(_sample_attn_kernel, n_pages=n_pages, n_new=n_new),
        grid_spec=pltpu.PrefetchScalarGridSpec(
            num_scalar_prefetch=1,
            grid=(bs,),
            in_specs=specs,
            out_specs=[pl.BlockSpec((1, TQS, 512), lambda i, pt: (i, 0, 0)), win, win],
            scratch_shapes=[pltpu.VMEM((2, 128, past), F32), pltpu.VMEM((2, 128, past), F32),
                            pltpu.SemaphoreType.DMA((2,)), pltpu.SemaphoreType.DMA((2,))]),
        out_shape=[jax.ShapeDtypeStruct((bs, TQS, 512), BF16),
                   jax.ShapeDtypeStruct((bs, 128, WINDOW), F32), jax.ShapeDtypeStruct((bs, 128, WINDOW), F32)],
        compiler_params=_params(("arbitrary",)),
        name="sample_attn",
    )(page_table, *ins)


def _lane_pair(cols, h0, h1, rows):
    lane = lax.broadcasted_iota(jnp.int32, (rows, LANES), 1)
    a = jnp.broadcast_to(cols[:, h0:h0 + 1], (rows, LANES))
    b = jnp.broadcast_to(cols[:, h1:h1 + 1], (rows, LANES))
    return jnp.where(lane < HEAD_DIM, a, b)


def _ssd_kernel(xbc_ref, z_ref, gd_ref, carry_ref, h0_ref, cw_ref, cb_ref, dtb_ref, arow_ref, dskip_ref,
                gnorm_ref, ltri_ref, y_ref, hout_ref, h_s, xfull, *, rows, n_valid):
    c = pl.program_id(1)
    halo = 8

    @pl.when(c == 0)
    def _():
        h_s[...] = h0_ref[0]
        xfull[0:halo, :] = carry_ref[0]

    xfull[halo:halo + rows, :] = xbc_ref[0]
    conv = cb_ref[...]
    for k in range(SSD_CONV):
        conv = conv + xfull[pl.ds(halo - (SSD_CONV - 1) + k, rows), :] * cw_ref[k:k + 1, :]
    tail = xfull[rows:rows + halo, :]
    xfull[0:halo, :] = tail
    xc = _silu(conv)
    xs = xc[:, 0:D_SSD]
    bm = xc[:, D_SSD:D_SSD + 128]
    cm = xc[:, D_SSD + 128:D_SSD + 256]

    lane = lax.broadcasted_iota(jnp.int32, (rows, LANES), 1)
    rowi = lax.broadcasted_iota(jnp.int32, (rows, LANES), 0)
    t = gd_ref[0] + dtb_ref[...]
    sp = jnp.maximum(t, 0.0) + jnp.log(1.0 + jnp.exp(-jnp.abs(t)))
    dt = jnp.where((lane >= DT_LANE) & (lane < DT_LANE + SSD_HEADS) & (rowi < n_valid), sp, 0.0)
    a = dt * arow_ref[...]
    acum = _split_dot_left(ltri_ref[...], a)
    acum_t = acum.T
    a_last = acum[rows - 1:rows, :]
    to_end = jnp.exp(a_last - acum)
    eac = jnp.exp(acum)
    dec = jnp.exp(a_last)

    li = lax.broadcasted_iota(jnp.int32, (rows, rows), 0)
    si = lax.broadcasted_iota(jnp.int32, (rows, rows), 1)
    causal = li >= si
    bmb = bm.astype(BF16)
    rowp = lax.broadcasted_iota(jnp.int32, (LANES, LANES), 0)
    cbs = []
    cmask = []
    for g in range(2):
        cg = jnp.where((lane >= g * 64) & (lane < (g + 1) * 64), cm, 0.0).astype(BF16)
        cmask.append(cg)
        cbs.append(_dot_nt(cg, bmb))
    for k in range(SSD_HEADS // 2):
        g = k // 2
        h0, h1 = DT_LANE + 2 * k, DT_LANE + 2 * k + 1
        xs_p = xs[:, k * 128:(k + 1) * 128]
        xdt = xs_p * _lane_pair(dt, h0, h1, rows)
        xdt_b = xdt.astype(BF16)
        ys = []
        for h in (h0, h1):
            seg = jnp.broadcast_to(acum[:, h:h + 1], (rows, rows)) - acum_t[h:h + 1, :]
            decay = jnp.where(causal, jnp.exp(jnp.where(causal, seg, 0.0)), 0.0)
            ys.append(_dot((cbs[g] * decay).astype(BF16), xdt_b))
        y = jnp.where(lane < HEAD_DIM, ys[0], ys[1])
        hp = h_s[k * 128:(k + 1) * 128, :]
        y = y + _dot_nt(cmask[g], hp.astype(BF16)) * _lane_pair(eac, h0, h1, rows)
        y = y + dskip_ref[:, k * 128:(k + 1) * 128] * xs_p
        xw = xdt * _lane_pair(to_end, h0, h1, rows)
        st = _dot(xw.T.astype(BF16), bmb)
        dfac = jnp.where(rowp < HEAD_DIM, dec[:, h0:h0 + 1], dec[:, h1:h1 + 1])
        h_s[k * 128:(k + 1) * 128, :] = hp * dfac + st
        xfull_y = y * _silu(z_ref[0, :, k * 128:(k + 1) * 128])
        y_ref[0, :, k * 128:(k + 1) * 128] = xfull_y.astype(y_ref.dtype)

    for g in range(2):
        yg = y_ref[0, :, g * 256:(g + 1) * 256].astype(F32)
        ms = jnp.mean(yg * yg, axis=-1, keepdims=True)
        y_ref[0, :, g * 256:(g + 1) * 256] = (yg * lax.rsqrt(ms + RMS_EPS)
                                               * gnorm_ref[:, g * 256:(g + 1) * 256]).astype(y_ref.dtype)

    @pl.when(c == pl.num_programs(1) - 1)
    def _():
        half = D_SSD // 2
        hout_ref[0, 0:half, :] = h_s[0:half, 0:SSD_STATE]
        hout_ref[0, half:D_SSD, :] = h_s[half:D_SSD, SSD_STATE:2 * SSD_STATE]


def _split_dot_left(tri, a):
    acc = None
    rem = a
    for _ in range(3):
        piece = rem.astype(BF16)
        rem = rem - piece.astype(F32)
        t = _dot(tri, piece)
        acc = t if acc is None else acc + t
    return acc


def _ssd(xbc, z, gd, carry, h0, wts, *, n_valid):
    b, l, _ = xbc.shape
    rows = SSD_CHUNK
    nc = l // rows
    blk = lambda w: pl.BlockSpec((1, rows, w), lambda bi, ci: (bi, ci, 0))
    per_b = lambda a: pl.BlockSpec((1,) + a.shape[1:], lambda bi, ci: (bi,) + (0,) * (a.ndim - 1))
    full = lambda a: pl.BlockSpec(a.shape, lambda bi, ci: (0,) * a.ndim)
    consts = [wts["conv_w"], wts["conv_b"], wts["dtb"], wts["arow"], wts["dskip"], wts["gnorm"], wts["ltri"]]
    return pl.pallas_call(
        functools.partial(_ssd_kernel, rows=rows, n_valid=n_valid),
        grid=(b, nc),
        in_specs=[blk(768), blk(512), blk(128), per_b(carry), per_b(h0)] + [full(a) for a in consts],
        out_specs=[blk(512), pl.BlockSpec((1, D_SSD, SSD_STATE), lambda bi, ci: (bi, 0, 0))],
        out_shape=[jax.ShapeDtypeStruct((b, l, 512), F32), jax.ShapeDtypeStruct((b, D_SSD, SSD_STATE), F32)],
        scratch_shapes=[pltpu.VMEM((512, 128), F32), pltpu.VMEM((rows + 8, 768), F32)],
        compiler_params=_params(("parallel", "arbitrary")),
        name="ssd",
    )(xbc, z, gd, carry, h0, *consts)


def _finish_kernel(x_ref, oa_ref, ys_ref, woa_ref, wos_ref, gf_ref, wg_ref, wu_ref, wd_ref,
                   y_ref, h_s, u_s, acc_s):
    f = pl.program_id(1)

    @pl.when(f == 0)
    def _():
        h = x_ref[...] + _dot(oa_ref[...], woa_ref[...]) + _dot(ys_ref[...].astype(BF16), wos_ref[...])
        h_s[...] = h
        ms = jnp.mean(h * h, axis=-1, keepdims=True)
        u_s[...] = (h * lax.rsqrt(ms + RMS_EPS) * gf_ref[...]).astype(BF16)
        acc_s[...] = jnp.zeros(acc_s.shape, F32)

    u = u_s[...]
    act = _silu(_dot(u, wg_ref[...])) * _dot(u, wu_ref[...])
    acc_s[...] += _dot(act.astype(BF16), wd_ref[...])

    @pl.when(f == pl.num_programs(1) - 1)
    def _():
        y_ref[...] = h_s[...] + acc_s[...]


def _finish(x2d, o_att, y_ssd, wts):
    n = x2d.shape[0]
    tm = min(512, n)
    nf = 2
    tf = D_FF // nf
    row = lambda w: pl.BlockSpec((tm, w), lambda i, f: (i, 0))
    full = lambda a: pl.BlockSpec(a.shape, lambda i, f: (0,) * a.ndim)
    return pl.pallas_call(
        _finish_kernel,
        grid=(n // tm, nf),
        in_specs=[row(D_MODEL), row(512), row(512), full(wts["wo_att"]), full(wts["wo_ssd"]), full(wts["gffn"]),
                  pl.BlockSpec((D_MODEL, tf), lambda i, f: (0, f)),
                  pl.BlockSpec((D_MODEL, tf), lambda i, f: (0, f)),
                  pl.BlockSpec((tf, D_MODEL), lambda i, f: (f, 0))],
        out_specs=row(D_MODEL),
        out_shape=jax.ShapeDtypeStruct((n, D_MODEL), F32),
        scratch_shapes=[pltpu.VMEM((tm, D_MODEL), F32), pltpu.VMEM((tm, D_MODEL), BF16),
                        pltpu.VMEM((tm, D_MODEL), F32)],
        compiler_params=_params(("parallel", "arbitrary")),
        name="finish",
    )(x2d, o_att, y_ssd, wts["wo_att"], wts["wo_ssd"], wts["gffn"], wts["w_gate"], wts["w_up"], wts["w_down"])


def _pair_perm():
    cols = []
    for r in range(ATT_REP):
        cols += list(range(r * HEAD_DIM, (r + 1) * HEAD_DIM))
        cols += list(range((ATT_REP + r) * HEAD_DIM, (ATT_REP + r + 1) * HEAD_DIM))
    return np.asarray(cols, np.int32)


def _block_ones(n, blk):
    i = np.arange(n)
    return (i[:, None] // blk == i[None, :] // blk).astype(np.float32) / blk


def _prep_weights(norm_mix, w_in, q_norm, k_norm, cmp_pe, cmp_w1, cmp_w2, conv_w, conv_b, dt_bias, a_log,
                  d_skip, ssd_norm, w_out, norm_ffn, w_gate, w_up, w_down):
    perm = _pair_perm()
    w = w_in
    gd = jnp.concatenate([w[:, OFF_GATE:OFF_Z], w[:, OFF_DT:D_IN],
                          jnp.zeros((D_MODEL, 128 - 3 * ATT_HEADS - SSD_HEADS), w.dtype)], axis=1)
    w_r = jnp.concatenate([w[:, :D_ATT][:, perm], w[:, OFF_KV:OFF_GATE], w[:, OFF_Z:OFF_XBC],
                           w[:, OFF_XBC:OFF_DT], gd], axis=1).astype(BF16)
    wts = dict(
        gmix=norm_mix.reshape(1, D_MODEL), w_in=w_r,
        qg=(jnp.tile(q_norm, ATT_HEADS) * (HEAD_DIM ** -0.5)).reshape(1, D_ATT),
        kg=jnp.tile(k_norm, ATT_KV_HEADS).reshape(1, D_KV),
        bq=jnp.asarray(_block_ones(D_ATT, HEAD_DIM), BF16), bk=jnp.asarray(_block_ones(D_KV, HEAD_DIM), BF16))

    def w1_big(w1):
        w1r = w1.reshape(2, CMP_STRIDE, HEAD_DIM, CMP_HIDDEN)
        eye = jnp.eye(ATT_KV_HEADS, dtype=w1.dtype)
        big = jnp.einsum("jsdh,gk->sgdjkh", w1r, eye)
        return big.reshape(CMP_STRIDE * D_KV, 2 * D_KV).astype(BF16)

    def pe_rows(pe):
        per = jnp.broadcast_to(pe.reshape(2, CMP_STRIDE, 1, HEAD_DIM), (2, CMP_STRIDE, ATT_KV_HEADS, HEAD_DIM))
        per = per.reshape(2, CMP_STRIDE * D_KV)
        return jnp.concatenate([per, jnp.zeros((6, CMP_STRIDE * D_KV), pe.dtype)], axis=0).astype(BF16)

    def w2_big(w2):
        eye = jnp.eye(ATT_KV_HEADS, dtype=w2.dtype)
        return jnp.einsum("hd,gk->ghkd", w2, eye).reshape(D_KV, D_KV).astype(BF16)

    wts.update(w1k=w1_big(cmp_w1[0]), w1v=w1_big(cmp_w1[1]), pek=pe_rows(cmp_pe[0]), pev=pe_rows(cmp_pe[1]),
               w2k=w2_big(cmp_w2[0]), w2v=w2_big(cmp_w2[1]))

    pad_lanes = lambda v: jnp.zeros((1, LANES), F32).at[0, DT_LANE:DT_LANE + SSD_HEADS].set(v)
    ltri = np.tril(np.ones((SSD_CHUNK, SSD_CHUNK), np.float32))
    wts.update(conv_w=jnp.concatenate([conv_w, jnp.zeros((4, D_CONV), F32)], axis=0), conv_b=conv_b.reshape(1, D_CONV),
               dtb=pad_lanes(dt_bias), arow=pad_lanes(-jnp.exp(a_log)),
               dskip=jnp.repeat(d_skip, 64).reshape(1, D_SSD), gnorm=ssd_norm.reshape(1, D_SSD),
               ltri=jnp.asarray(ltri, BF16))
    wts.update(wo_att=w_out[:D_ATT].astype(BF16), wo_ssd=w_out[D_ATT:].astype(BF16),
               gffn=norm_ffn.reshape(1, D_MODEL), w_gate=w_gate.astype(BF16), w_up=w_up.astype(BF16),
               w_down=w_down.astype(BF16))
    return wts


def _gate_expand():
    m = np.zeros((ATT_KV_HEADS, LANES, 3 * ATT_REP * HEAD_DIM), np.float32)
    for g in range(ATT_KV_HEADS):
        for r in range(ATT_REP):
            for br in range(3):
                c0 = br * ATT_REP * HEAD_DIM + r * HEAD_DIM
                m[g, g * 3 * ATT_REP + r * 3 + br, c0:c0 + HEAD_DIM] = 1.0
    return jnp.asarray(m, BF16)


def _sel_matrix(n_blk_pad, n_sel):
    m = np.zeros((n_blk_pad, LANES), np.float32)
    for n in range(n_blk_pad - 1):
        for j in {n // 4, (n + 1) // 4}:
            if j < n_sel:
                m[n, j] = 1.0
    return jnp.asarray(m, BF16)


def _override(q_pos, n_sel):
    j = np.arange(LANES)[None, :]
    cur = (q_pos // SEL_BLOCK)[:, None]
    forced = (j == 0) | (j == cur) | (j == cur - 1)
    ovr = np.where(forced, 1e30, np.where(j <= cur, 0.0, -1e30))
    ovr = np.where(j < n_sel, ovr, -1e30)
    return ovr.astype(np.float32)


def _prompt_tables(rel_bias, t):
    n_qt = t // TQ
    pos = np.arange(t)
    nb = t // CMP_STRIDE
    e = CMP_STRIDE * np.arange(nb) + (CMP_BLOCK - 1)
    dist = pos[None, :] - e[:, None]
    cmp_idx = _idx_table(dist, (dist >= 0) & (np.arange(nb)[:, None] < nb - 1))
    cmp_idx = cmp_idx.reshape(nb, n_qt, TQ).transpose(1, 0, 2)
    i = np.arange(TQ)[None, :]
    j = np.arange(TQ)[:, None]
    diag = _idx_table(i - j, i >= j)
    prev = _idx_table(TQ + i - j, np.ones((TQ, TQ), bool))
    prev2 = _idx_table(2 * TQ + i - j, (2 * TQ + i - j) < WINDOW)
    att_idx = np.stack([diag, prev, prev2])
    return _bias_tables(rel_bias, cmp_idx, stack_cols=True), _bias_tables(rel_bias, att_idx, stack_cols=True)


def _prompt_layer(x, wts, rel_bias):
    b, t, _ = x.shape
    n = b * t
    p = _project(x.reshape(n, D_MODEL), wts, t, "prompt")
    n_sub = t // CMP_STRIDE
    sub = lambda a: a.reshape(b, n_sub, CMP_STRIDE * D_KV)
    kcmp, vcmp = _compress_prompt(sub(p["kc_b"]), sub(p["vc_b"]), wts)
    cmp_tab, att_tab = _prompt_tables(rel_bias, t)
    n_sel = t // SEL_BLOCK
    ovr_t = jnp.asarray(_override(np.arange(t), n_sel).reshape(t // TQ, TQ, LANES).transpose(0, 2, 1))
    r3 = lambda a: a.reshape(b, t, a.shape[-1])
    o_cmp, pen_t = _cmp_select_t(p["q_t"], kcmp, vcmp, cmp_tab, _sel_matrix(n_sub, n_sel).T, ovr_t,
                                 tq=TQ, n_sel=n_sel, k_top=min(SEL_TOPN, n_sel))
    o_att = _prompt_attn(p["q_t"], pen_t, o_cmp, r3(p["gd"]), r3(p["ksa"]), p["vs_tb"], r3(p["kw_b"]), p["vw_tb"],
                         att_tab, _gate_expand(), tq=TQ)
    carry = jnp.zeros((b, 8, D_CONV), F32)
    h0 = jnp.zeros((b, 512, 128), F32)
    y_ssd, h_fin = _ssd(r3(p["xbc"]), r3(p["z"]), r3(p["gd"]), carry, h0, wts, n_valid=SSD_CHUNK)
    y = _finish(x.reshape(n, D_MODEL), o_att.reshape(n, 512), y_ssd.reshape(n, 512), wts)
    wb = min(WINDOW, t)
    kv_out = {k: _token_major(p[k + "_t"]) for k in KV_NAMES}
    ssm = h_fin.reshape(b, SSD_HEADS, 64, SSD_STATE)
    return y.reshape(b, t, D_MODEL), (kv_out["kc"], kv_out["vc"], kv_out["ks"], kv_out["vs"],
                                      kv_out["kw"][:, t - wb:], kv_out["vw"][:, t - wb:],
                                      r3(p["xbc"])[:, t - (SSD_CONV - 1):], ssm)


def _token_major(a_t):
    b, _, t = a_t.shape
    return jnp.swapaxes(a_t, 1, 2).reshape(b, t, ATT_KV_HEADS, HEAD_DIM)


def _feature_major(a):
    return jnp.swapaxes(a.reshape(a.shape[:-2] + (D_KV,)), -1, -2)


def _sample_tables(rel_bias, past, s_new, n_blk_pad):
    s = np.minimum(np.arange(TQS), s_new - 1)[:, None]
    pos = past + s
    nidx = np.arange(n_blk_pad)[None, :]
    e = CMP_STRIDE * nidx + (CMP_BLOCK - 1)
    cmp_idx = _idx_table(pos - e, (e <= pos) & (nidx < n_blk_pad - 1))[None]
    key = np.arange(past)[None, :]
    sel_idx = _idx_table(pos - key, np.ones((TQS, past), bool))[None]
    jn = np.arange(LANES)[None, :]
    new_idx = _idx_table(s - jn, (jn <= s) & (jn < s_new))[None]
    wi = np.arange(WINDOW)[None, :]
    wdist = pos - (past - WINDOW + wi)
    win_idx = _idx_table(wdist, (wdist >= 0) & (wdist < WINDOW))[None]
    tabs = [_bias_tables(rel_bias, t) for t in (cmp_idx, sel_idx, new_idx, win_idx)]
    return tabs, pos[:, 0]


def _sample_layer(x, c_kc, c_vc, c_ks, c_vs, c_kw, c_vw, s_conv, s_ssm, page_table, wts, rel_bias):
    bs, s_new, _ = x.shape
    n = bs * s_new
    n_pages = page_table.shape[1]
    past = n_pages * PAGE_SIZE
    p = _project(x.reshape(n, D_MODEL), wts, s_new, "sample")
    r3 = lambda a: a.reshape(bs, s_new, a.shape[-1])
    padq = lambda a, rows: jnp.pad(r3(a), ((0, 0), (0, rows - s_new), (0, 0)))
    new_t = lambda a_t: jnp.pad(jnp.swapaxes(a_t.reshape(D_KV, bs, s_new), 0, 1), ((0, 0), (0, 0), (LANES - s_new, 0)))
    kcmp, vcmp = _compress_sample(page_table, _feature_major(c_kc), _feature_major(c_vc), wts)
    n_blk_pad = past // CMP_STRIDE
    (cmp_tab, sel_tab, new_tab, win_tab), pos = _sample_tables(rel_bias, past, s_new, n_blk_pad)
    n_sel = past // SEL_BLOCK
    ovr = jnp.asarray(_override(pos, n_sel)[None])
    qp = padq(p["q"], TQS)
    o_cmp, pen = _cmp_select(qp, kcmp, vcmp, cmp_tab, _sel_matrix(n_blk_pad, n_sel), ovr,
                             tq=TQS, n_sel=n_sel, k_top=min(SEL_TOPN - 1, n_sel))
    blk_of_key = np.arange(past)[None, :] // SEL_BLOCK
    eall = jnp.asarray((np.arange(LANES)[:, None] == blk_of_key).astype(np.float32), BF16)
    o_att, kw_new, vw_new = _sample_attn(
        page_table, qp, pen, o_cmp, padq(p["gd"], TQS),
        _feature_major(c_ks), _feature_major(c_vs), padq(p["ks_b"], LANES), padq(p["vs_b"], LANES),
        _feature_major(c_kw), _feature_major(c_vw), padq(p["kw_b"], LANES), padq(p["vw_b"], LANES),
        new_t(p["kw_t"]), new_t(p["vw_t"]),
        eall, sel_tab, new_tab, win_tab, _gate_expand(), n_new=s_new)
    carry = jnp.pad(s_conv, ((0, 0), (8 - (SSD_CONV - 1), 0), (0, 0)))
    h0 = s_ssm.reshape(bs, 512, SSD_STATE)
    h0 = jnp.concatenate([h0, h0], axis=-1)
    y_ssd, h_fin = _ssd(padq(p["xbc"], SSD_CHUNK), padq(p["z"], SSD_CHUNK), padq(p["gd"], SSD_CHUNK), carry, h0, wts,
                        n_valid=s_new)
    y = _finish(x.reshape(n, D_MODEL), o_att[:, :s_new].reshape(n, 512), y_ssd[:, :s_new].reshape(n, 512), wts)
    kv4 = lambda a: a.reshape(bs, -1, ATT_KV_HEADS, HEAD_DIM)
    ssm = h_fin.reshape(bs, SSD_HEADS, 64, SSD_STATE)
    conv_state = jnp.concatenate([s_conv, r3(p["xbc"])], axis=1)[:, s_new:]
    return y.reshape(bs, s_new, D_MODEL), (kv4(p["kc"]), kv4(p["vc"]), kv4(p["ks"]), kv4(p["vs"]),
                                           _token_major(kw_new), _token_major(vw_new), conv_state, ssm)


def kernel(x_prompt, x_sample, cache_k_cmp, cache_v_cmp, cache_k_sel, cache_v_sel, cache_k_win, cache_v_win,
           state_conv, state_ssm, page_table, norm_mix, w_in, q_norm, k_norm, cmp_pe, cmp_w1, cmp_w2, rel_bias,
           conv_w, conv_b, dt_bias, a_log, d_skip, ssd_norm, w_out, norm_ffn, w_gate, w_up, w_down):
    depth = w_in.shape[0]
    y_p, y_s = x_prompt, x_sample
    p_states, s_states = [], []
    for l in range(depth):
        wts = _prep_weights(norm_mix[l], w_in[l], q_norm[l], k_norm[l], cmp_pe[l], cmp_w1[l], cmp_w2[l],
                            conv_w[l], conv_b[l], dt_bias[l], a_log[l], d_skip[l], ssd_norm[l], w_out[l],
                            norm_ffn[l], w_gate[l], w_up[l], w_down[l])
        y_p, st_p = _prompt_layer(y_p, wts, rel_bias)
        y_s, st_s = _sample_layer(y_s, cache_k_cmp[l], cache_v_cmp[l], cache_k_sel[l], cache_v_sel[l],
                                  cache_k_win[l], cache_v_win[l], state_conv[l], state_ssm[l], page_table,
                                  wts, rel_bias)
        p_states.append(st_p)
        s_states.append(st_s)
    p_out = [jnp.stack(a) for a in zip(*p_states)]
    s_out = [jnp.stack(a) for a in zip(*s_states)]
    return (y_p, y_s, *p_out, *s_out)
```

```python
import functools
import math

import numpy as np
import jax
import jax.numpy as jnp
from jax import lax
from jax.experimental import pallas as pl
from jax.experimental.pallas import tpu as pltpu

F32 = jnp.float32
BF16 = jnp.bfloat16

D_MODEL = 1024
HEAD_DIM = 64
ATT_HEADS = 8
ATT_KV_HEADS = 2
ATT_REP = ATT_HEADS // ATT_KV_HEADS
CMP_BLOCK = 32
CMP_STRIDE = 16
CMP_HIDDEN = 64
SEL_BLOCK = 64
SEL_TOPN = 16
WINDOW = 512
N_BUCKETS = 32
MAX_DISTANCE = 128
PAGE_SIZE = 128
SSD_HEADS = 8
SSD_STATE = 64
SSD_CONV = 4
SSD_CHUNK = 128
D_ATT = ATT_HEADS * HEAD_DIM
D_SSD = SSD_HEADS * 64
D_KV = ATT_KV_HEADS * HEAD_DIM
D_CONV = D_SSD + 2 * 2 * SSD_STATE
D_FF = ((8 * D_MODEL // 3 + 255) // 256) * 256
OFF_KV = D_ATT
OFF_GATE = OFF_KV + 6 * D_KV
OFF_Z = OFF_GATE + 3 * ATT_HEADS
OFF_XBC = OFF_Z + D_SSD
OFF_DT = OFF_XBC + D_CONV
D_IN = OFF_DT + SSD_HEADS
RMS_EPS = 1e-6
NEG = -1e30

C_Q, C_KV, C_Z, C_XBC, C_GD = 0, 512, 1280, 1792, 2560
D_INR = 2688
DT_LANE = 24
LANES = 128
TQ = 256
TQS = 16
VMEM_LIMIT = 48 * 1024 * 1024


def _dot(a, b):
    return jnp.dot(a, b, preferred_element_type=F32)


def _dot_nt(a, b):
    return lax.dot_general(a, b, (((1,), (1,)), ((), ())), preferred_element_type=F32)


def _split_dot(a, b, parts):
    acc = None
    rem = a
    for _ in range(parts):
        piece = rem.astype(BF16)
        rem = rem - piece.astype(F32)
        t = _dot(piece, b)
        acc = t if acc is None else acc + t
    return acc


def _silu(x):
    return x * (1.0 / (1.0 + jnp.exp(-x)))


def _params(sem=None):
    kw = dict(vmem_limit_bytes=VMEM_LIMIT)
    if sem is not None:
        kw["dimension_semantics"] = sem
    return pltpu.CompilerParams(**kw)


KV_NAMES = ("kc", "vc", "ks", "vs", "kw", "vw")
PROJ_OUTPUTS = {
    "prompt": ([(k + "_t", "seq", 128, F32) for k in KV_NAMES]
               + [("kc_b", "row", 128, BF16), ("vc_b", "row", 128, BF16), ("ksa", "row", 256, BF16),
                  ("kw_b", "row", 128, BF16), ("q_t", "col", 512, BF16), ("vs_tb", "col", 128, BF16),
                  ("vw_tb", "col", 128, BF16)]),
    "sample": ([(k, "row", 128, F32) for k in KV_NAMES]
               + [("q", "row", 512, BF16), ("ks_b", "row", 128, BF16), ("vs_b", "row", 128, BF16),
                  ("kw_b", "row", 128, BF16), ("vw_b", "row", 128, BF16),
                  ("kw_t", "col", 128, F32), ("vw_t", "col", 128, F32)]),
}
PROJ_COMMON = [("z", "row", 512, F32), ("xbc", "row", 768, F32), ("gd", "row", 128, F32)]


def _proj_kernel(x_ref, gmix_ref, w_ref, qg_ref, kg_ref, bq_ref, bk_ref, *out_refs, names, tm, t_len):
    out = dict(zip(names, out_refs))
    x = x_ref[...]
    ms = jnp.mean(x * x, axis=-1, keepdims=True)
    u = (x * lax.rsqrt(ms + RMS_EPS) * gmix_ref[...]).astype(BF16)

    def proj(lo, hi):
        return _dot(u, w_ref[:, lo:hi])

    def headnorm(v, b_ref, g_ref):
        msq = _dot((v * v).astype(BF16), b_ref[...])
        return v * lax.rsqrt(msq + RMS_EPS) * g_ref[...]

    def put(name, value):
        if name in out:
            ref = out[name]
            ref[...] = value().astype(ref.dtype).reshape(ref.shape)

    q = headnorm(proj(C_Q, C_Q + 512), bq_ref, qg_ref)
    put("q", lambda: q)
    put("q_t", lambda: q.T)
    kv = {}
    for i, name in enumerate(KV_NAMES):
        v = proj(C_KV + 128 * i, C_KV + 128 * (i + 1))
        kv[name] = headnorm(v, bk_ref, kg_ref) if name in ("ks", "kw") else v
    for name, v in kv.items():
        put(name, lambda v=v: v)
        put(name + "_b", lambda v=v: v)
        put(name + "_t", lambda v=v: v.T)
        put(name + "_tb", lambda v=v: v.T)
    if "ksa" in out:
        row = pl.program_id(0) * tm + lax.broadcasted_iota(jnp.int32, (tm, LANES), 0)
        blk = (row % t_len) // SEL_BLOCK
        lane = lax.broadcasted_iota(jnp.int32, (tm, LANES), 1)
        out["ksa"][:, 0:128] = kv["ks"].astype(BF16)
        out["ksa"][:, 128:256] = jnp.where(lane == blk, 1.0, 0.0).astype(BF16)
    put("z", lambda: proj(C_Z, C_Z + 512))
    put("xbc", lambda: proj(C_XBC, C_XBC + 768))
    put("gd", lambda: proj(C_GD, C_GD + 128))


def _project(x2d, wts, t_len, mode):
    n = x2d.shape[0]
    tm = min(512, n)
    per_seq = max(t_len // tm, 1)
    full = lambda a: pl.BlockSpec(a.shape, lambda i: (0,) * a.ndim)
    ins = [x2d, wts["gmix"], wts["w_in"], wts["qg"], wts["kg"], wts["bq"], wts["bk"]]
    outs = PROJ_OUTPUTS[mode] + PROJ_COMMON
    specs, shapes = [], []
    for _, layout, w, dt in outs:
        if layout == "row":
            specs.append(pl.BlockSpec((tm, w), lambda i: (i, 0)))
            shapes.append(jax.ShapeDtypeStruct((n, w), dt))
        elif layout == "col":
            specs.append(pl.BlockSpec((w, tm), lambda i: (0, i)))
            shapes.append(jax.ShapeDtypeStruct((w, n), dt))
        else:
            specs.append(pl.BlockSpec((1, w, tm), lambda i: (i // per_seq, 0, i % per_seq)))
            shapes.append(jax.ShapeDtypeStruct((n // t_len, w, t_len), dt))
    names = tuple(name for name, _, _, _ in outs)
    res = pl.pallas_call(
        functools.partial(_proj_kernel, names=names, tm=tm, t_len=t_len),
        grid=(n // tm,),
        in_specs=[pl.BlockSpec((tm, D_MODEL), lambda i: (i, 0))] + [full(a) for a in ins[1:]],
        out_specs=specs,
        out_shape=shapes,
        compiler_params=_params(("parallel",)),
        name="proj",
    )(*ins)
    return dict(zip(names, res))


def _bucket_np(dist):
    n = np.maximum(dist, 0)
    max_exact = N_BUCKETS // 2
    nf = np.maximum(n, 1).astype(np.float64)
    large = max_exact + (np.log(nf / max_exact) / math.log(MAX_DISTANCE / max_exact)
                         * (N_BUCKETS - max_exact)).astype(np.int64)
    large = np.minimum(large, N_BUCKETS - 1)
    return np.where(n < max_exact, n, large).astype(np.int32)


def _idx_table(dist, valid):
    return np.where(valid, _bucket_np(dist), -1).astype(np.int32)


def _table_kernel(rb_ref, idx_ref, out_ref):
    h = pl.program_id(0) * ATT_REP + pl.program_id(2)
    idx = idx_ref[0]
    far = rb_ref[N_BUCKETS - 1, h]
    acc = jnp.zeros(idx.shape, F32)
    for b in range(N_BUCKETS - 1):
        acc = jnp.where(idx == b, rb_ref[b, h] - far, acc)
    out_ref[...] = jnp.where(idx < 0, NEG, acc).reshape(out_ref.shape)


def _bias_tables(rel_bias, idx, stack_cols=False):
    k, r, c = idx.shape
    if stack_cols:
        return pl.pallas_call(
            _table_kernel,
            grid=(ATT_KV_HEADS, k, ATT_REP),
            in_specs=[pl.BlockSpec(memory_space=pltpu.SMEM),
                      pl.BlockSpec((1, r, c), lambda g, kk, rr: (kk, 0, 0))],
            out_specs=pl.BlockSpec((1, 1, r, c), lambda g, kk, rr: (g, kk, 0, rr)),
            out_shape=jax.ShapeDtypeStruct((ATT_KV_HEADS, k, r, ATT_REP * c), F32),
            name="bias_table_t",
        )(rel_bias, jnp.asarray(idx))
    out = pl.pallas_call(
        _table_kernel,
        grid=(ATT_KV_HEADS, k, ATT_REP),
        in_specs=[pl.BlockSpec(memory_space=pltpu.SMEM),
                  pl.BlockSpec((1, r, c), lambda g, kk, rr: (kk, 0, 0))],
        out_specs=pl.BlockSpec((1, 1, 1, r, c), lambda g, kk, rr: (g, kk, rr, 0, 0)),
        out_shape=jax.ShapeDtypeStruct((ATT_KV_HEADS, k, ATT_REP, r, c), F32),
        name="bias_table",
    )(rel_bias, jnp.asarray(idx))
    return out.reshape(ATT_KV_HEADS, k, ATT_REP * r, c)


def _compress_core(x, w1, pe, w2):
    return _compress_tail(_dot(x.astype(BF16), w1), w1, pe, w2)


def _compress_tail(u, w1, pe, w2):
    n_sub = u.shape[0]
    upe = _dot(pe, w1)
    nxt = pltpu.roll(u[:, 128:256], n_sub - 1, 0)
    pre = u[:, 0:128] + nxt + upe[0:1, 0:128] + upe[1:2, 128:256]
    return _dot(_silu(pre).astype(BF16), w2)


def _knorm(v, bk, kg):
    msq = _dot((v * v).astype(BF16), bk)
    return v * lax.rsqrt(msq + RMS_EPS) * kg


def _compress_prompt_kernel(kc_ref, vc_ref, w1k_ref, w1v_ref, pek_ref, pev_ref, w2k_ref, w2v_ref,
                            kg_ref, bk_ref, ko_ref, vo_ref):
    kc = _compress_core(kc_ref[0], w1k_ref[...], pek_ref[...], w2k_ref[...])
    ko_ref[0] = _knorm(kc, bk_ref[...], kg_ref[...]).astype(BF16)
    vo_ref[0] = _compress_core(vc_ref[0], w1v_ref[...], pev_ref[...], w2v_ref[...]).astype(BF16)


def _compress_prompt(kc, vc, wts):
    b, n_sub, w = kc.shape
    full = lambda a: pl.BlockSpec(a.shape, lambda i: (0,) * a.ndim)
    consts = [wts["w1k"], wts["w1v"], wts["pek"], wts["pev"], wts["w2k"], wts["w2v"], wts["kg"], wts["bk"]]
    blk = pl.BlockSpec((1, n_sub, w), lambda i: (i, 0, 0))
    oblk = pl.BlockSpec((1, n_sub, 128), lambda i: (i, 0, 0))
    return pl.pallas_call(
        _compress_prompt_kernel,
        grid=(b,),
        in_specs=[blk, blk] + [full(a) for a in consts],
        out_specs=[oblk, oblk],
        out_shape=[jax.ShapeDtypeStruct((b, n_sub, 128), BF16)] * 2,
        compiler_params=_params(("parallel",)),
        name="compress_prompt",
    )(kc, vc, *consts)


def _page_copy(cache_ref, page, buf_ref, slot, p, sem_ref, lane_major):
    if lane_major:
        dst = buf_ref.at[slot, :, pl.ds(pl.multiple_of(p * PAGE_SIZE, PAGE_SIZE), PAGE_SIZE)]
    else:
        dst = buf_ref.at[slot, p]
    return pltpu.make_async_copy(cache_ref.at[page], dst, sem_ref.at[slot])


def _gather_start(pt_ref, b, slot, caches, bufs, sems, n_pages, lane_major):
    def body(p, carry):
        page = pt_ref[b, p]
        for cache_ref, buf_ref, sem_ref in zip(caches, bufs, sems):
            _page_copy(cache_ref, page, buf_ref, slot, p, sem_ref, lane_major).start()
        return carry
    lax.fori_loop(0, n_pages, body, 0)


def _gather_wait(slot, caches, bufs, sems, n_pages, lane_major):
    def body(p, carry):
        for cache_ref, buf_ref, sem_ref in zip(caches, bufs, sems):
            _page_copy(cache_ref, 0, buf_ref, slot, p, sem_ref, lane_major).wait()
        return carry
    lax.fori_loop(0, n_pages, body, 0)


def _gather_pipeline(pt_ref, caches, bufs, sems, n_pages, lane_major):
    b = pl.program_id(0)
    nb = pl.num_programs(0)
    slot = b % 2

    @pl.when(b == 0)
    def _():
        _gather_start(pt_ref, 0, 0, caches, bufs, sems, n_pages, lane_major)

    @pl.when(b + 1 < nb)
    def _():
        _gather_start(pt_ref, b + 1, 1 - slot, caches, bufs, sems, n_pages, lane_major)

    _gather_wait(slot, caches, bufs, sems, n_pages, lane_major)
    return slot


def _compress_paged(buf, slot, rows_s, perm, w1, pe, w2, n_pages):
    groups = PAGE_SIZE // CMP_STRIDE

    def body(i, carry):
        pair = buf[slot, pl.ds(2 * i, 2)].reshape(2 * D_KV, PAGE_SIZE)
        rows = _dot_nt(perm, pair.astype(BF16))
        for half in range(2):
            start = pl.multiple_of((2 * i + half) * groups, groups)
            for s in range(CMP_STRIDE):
                rows_s[s // 2, pl.ds(start, groups), (s % 2) * D_KV:(s % 2 + 1) * D_KV] = (
                    rows[s * groups:(s + 1) * groups, half * D_KV:(half + 1) * D_KV])
        return carry
    lax.fori_loop(0, n_pages // 2, body, 0, unroll=4)
    u = None
    for j in range(CMP_STRIDE // 2):
        t = _dot(rows_s[j].astype(BF16), w1[2 * j * D_KV:(2 * j + 2) * D_KV, :])
        u = t if u is None else u + t
    return _compress_tail(u, w1, pe, w2)


def _compress_sample_kernel(pt_ref, ck_ref, cv_ref, perm_ref, w1k_ref, w1v_ref, pek_ref, pev_ref, w2k_ref, w2v_ref,
                            kg_ref, bk_ref, ko_ref, vo_ref, kbuf, vbuf, rows_s, ksem, vsem, *, n_pages):
    slot = _gather_pipeline(pt_ref, (ck_ref, cv_ref), (kbuf, vbuf), (ksem, vsem), n_pages, lane_major=False)
    perm = perm_ref[...]
    kc = _compress_paged(kbuf, slot, rows_s, perm, w1k_ref[...], pek_ref[...], w2k_ref[...], n_pages)
    ko_ref[0] = _knorm(kc, bk_ref[...], kg_ref[...]).astype(BF16)
    vo_ref[0] = _compress_paged(vbuf, slot, rows_s, perm, w1v_ref[...], pev_ref[...], w2v_ref[...], n_pages).astype(BF16)


def _compress_sample(page_table, ck, cv, wts):
    bs, n_pages = page_table.shape
    n_sub = n_pages * PAGE_SIZE // CMP_STRIDE
    full = lambda a: pl.BlockSpec(a.shape, lambda i, pt: (0,) * a.ndim)
    groups = PAGE_SIZE // CMP_STRIDE
    r = np.arange(PAGE_SIZE)
    perm = jnp.asarray((np.arange(PAGE_SIZE)[None, :] == (CMP_STRIDE * (r % groups) + r // groups)[:, None])
                       .astype(np.float32), BF16)
    consts = [perm, wts["w1k"], wts["w1v"], wts["pek"], wts["pev"], wts["w2k"], wts["w2v"], wts["kg"], wts["bk"]]
    anyspec = pl.BlockSpec(memory_space=pl.ANY)
    oblk = pl.BlockSpec((1, n_sub, 128), lambda i, pt: (i, 0, 0))
    pages = pltpu.VMEM((2, n_pages, D_KV, PAGE_SIZE), F32)
    return pl.pallas_call(
        functools.partial(_compress_sample_kernel, n_pages=n_pages),
        grid_spec=pltpu.PrefetchScalarGridSpec(
            num_scalar_prefetch=1,
            grid=(bs,),
            in_specs=[anyspec, anyspec] + [full(a) for a in consts],
            out_specs=[oblk, oblk],
            scratch_shapes=[pages, pages, pltpu.VMEM((CMP_STRIDE // 2, n_sub, 2 * D_KV), F32),
                            pltpu.SemaphoreType.DMA((2,)), pltpu.SemaphoreType.DMA((2,))]),
        out_shape=[jax.ShapeDtypeStruct((bs, n_sub, 128), BF16)] * 2,
        compiler_params=_params(("arbitrary",)),
        name="compress_sample",
    )(page_table, ck, cv, *consts)


def _group_queries(q, g, tq):
    lane = lax.broadcasted_iota(jnp.int32, (tq, LANES), 1)
    mine = (lane >= g * HEAD_DIM) & (lane < (g + 1) * HEAD_DIM)
    zero = jnp.zeros((tq, LANES), q.dtype)
    return jnp.concatenate([jnp.where(mine, q[:, r * 128:(r + 1) * 128], zero) for r in range(ATT_REP)], axis=0)


def _pack_heads(o, g, tq):
    lane = lax.broadcasted_iota(jnp.int32, (tq, LANES), 1)
    first = g == 0
    chunks = []
    for k in range(ATT_REP // 2):
        a = o[2 * k * tq:(2 * k + 1) * tq]
        b = o[(2 * k + 1) * tq:(2 * k + 2) * tq]
        lo = jnp.where(first, a, pltpu.roll(a, HEAD_DIM, 1))
        hi = jnp.where(first, pltpu.roll(b, HEAD_DIM, 1), b)
        chunks.append(jnp.where(lane < HEAD_DIM, lo, hi))
    return jnp.concatenate(chunks, axis=1)


def _cmp_select_kernel(q_ref, kc_ref, vc_ref, tab_ref, msel_ref, ovr_ref, o_ref, pen_ref, *, tq, n_sel, k_top):
    q = q_ref[0]
    kc = kc_ref[0]
    vc = vc_ref[0]
    ovr = ovr_ref[0]
    lane = lax.broadcasted_iota(jnp.int32, (tq, LANES), 1)
    outs = []
    for g in range(ATT_KV_HEADS):
        tab = tab_ref[g, 0]
        s = _dot_nt(_group_queries(q, g, tq), kc) + tab
        m = jnp.max(s, axis=-1, keepdims=True)
        e = jnp.where(tab > 0.5 * NEG, jnp.exp(s - m), 0.0)
        p = e / jnp.maximum(jnp.sum(e, axis=-1, keepdims=True), 1e-30)
        outs.append(_dot(p.astype(BF16), vc))
        imp = p[0:tq]
        for r in range(1, ATT_REP):
            imp = imp + p[r * tq:(r + 1) * tq]
        score = _split_dot(imp, msel_ref[...], 3)
        score = jnp.where(ovr == 0.0, score, ovr)
        rank = jnp.zeros((tq, LANES), F32)
        for j in range(n_sel):
            col = score[:, j:j + 1]
            beats = (col > score) | ((col == score) & (lane > j))
            rank = rank + jnp.where(beats, 1.0, 0.0)
        pen_ref[0, :, g * 128:(g + 1) * 128] = jnp.where(rank < k_top, 0.0, NEG).astype(BF16)
    for g in range(ATT_KV_HEADS):
        o_ref[0, :, g * 256:(g + 1) * 256] = _pack_heads(outs[g], g, tq)


def _cmp_select(q, kc, vc, tab, msel, ovr, *, tq, n_sel, k_top):
    b, t, _ = q.shape
    nb = kc.shape[1]
    n_qt = t // tq
    return pl.pallas_call(
        functools.partial(_cmp_select_kernel, tq=tq, n_sel=n_sel, k_top=k_top),
        grid=(n_qt, b),
        in_specs=[pl.BlockSpec((1, tq, 512), lambda qi, bi: (bi, qi, 0)),
                  pl.BlockSpec((1, nb, 128), lambda qi, bi: (bi, 0, 0)),
                  pl.BlockSpec((1, nb, 128), lambda qi, bi: (bi, 0, 0)),
                  pl.BlockSpec((ATT_KV_HEADS, 1, ATT_REP * tq, nb), lambda qi, bi: (0, qi, 0, 0)),
                  pl.BlockSpec(msel.shape, lambda qi, bi: (0, 0)),
                  pl.BlockSpec((1, tq, 128), lambda qi, bi: (qi, 0, 0))],
        out_specs=[pl.BlockSpec((1, tq, 512), lambda qi, bi: (bi, qi, 0)),
                   pl.BlockSpec((1, tq, 256), lambda qi, bi: (bi, qi, 0))],
        out_shape=[jax.ShapeDtypeStruct((b, t, 512), F32), jax.ShapeDtypeStruct((b, t, 256), BF16)],
        compiler_params=_params(("parallel", "parallel")),
        name="cmp_select",
    )(q, kc, vc, tab, msel, ovr)


def _cmp_select_t_kernel(qt_ref, kc_ref, vc_ref, tab_ref, msel_ref, ovr_ref, o_ref, pen_ref,
                         *, tq, n_sel, k_top):
    qt = qt_ref[...]
    kc = kc_ref[0]
    vct = vc_ref[0].astype(F32).T.astype(BF16)
    ovr = ovr_ref[0, 0:n_sel, :]
    row = lax.broadcasted_iota(jnp.int32, (n_sel, tq), 0)
    for g in range(ATT_KV_HEADS):
        tab = tab_ref[g, 0]
        s = _dot(kc, _group_queries_t(qt, g, tq)) + tab
        m = jnp.max(s, axis=0, keepdims=True)
        e = jnp.where(tab > 0.5 * NEG, jnp.exp(s - m), 0.0)
        p = e / jnp.maximum(jnp.sum(e, axis=0, keepdims=True), 1e-30)
        o_ref[0, :, g * 256:(g + 1) * 256] = _unpack_heads_t(_dot(vct, p.astype(BF16)), g, tq)
        imp = p[:, 0:tq]
        for r in range(1, ATT_REP):
            imp = imp + p[:, r * tq:(r + 1) * tq]
        score = _split_dot_left(msel_ref[...], imp)[0:n_sel, :]
        score = jnp.where(ovr == 0.0, score, ovr)
        rank = jnp.zeros((n_sel, tq), F32)
        for j in range(n_sel):
            cand = score[j:j + 1, :]
            beats = (cand > score) | ((cand == score) & (row > j))
            rank = rank + jnp.where(beats, 1.0, 0.0)
        pen_ref[0, g, 0:n_sel, :] = jnp.where(rank < k_top, 0.0, NEG).astype(BF16)
        pen_ref[0, g, n_sel:LANES, :] = jnp.zeros((LANES - n_sel, tq), BF16)


def _cmp_select_t(qt, kc, vc, tab, msel_t, ovr_t, *, tq, n_sel, k_top):
    b, nb, _ = kc.shape
    n_qt = tab.shape[1]
    t = n_qt * tq
    return pl.pallas_call(
        functools.partial(_cmp_select_t_kernel, tq=tq, n_sel=n_sel, k_top=k_top),
        grid=(n_qt, b),
        in_specs=[pl.BlockSpec((512, tq), lambda qi, bi: (0, bi * n_qt + qi)),
                  pl.BlockSpec((1, nb, 128), lambda qi, bi: (bi, 0, 0)),
                  pl.BlockSpec((1, nb, 128), lambda qi, bi: (bi, 0, 0)),
                  pl.BlockSpec((ATT_KV_HEADS, 1, nb, ATT_REP * tq), lambda qi, bi: (0, qi, 0, 0)),
                  pl.BlockSpec(msel_t.shape, lambda qi, bi: (0, 0)),
                  pl.BlockSpec((1, 128, tq), lambda qi, bi: (qi, 0, 0))],
        out_specs=[pl.BlockSpec((1, tq, 512), lambda qi, bi: (bi, qi, 0)),
                   pl.BlockSpec((1, ATT_KV_HEADS, 128, tq), lambda qi, bi: (bi, 0, 0, qi))],
        out_shape=[jax.ShapeDtypeStruct((b, t, 512), F32), jax.ShapeDtypeStruct((b, ATT_KV_HEADS, 128, t), BF16)],
        compiler_params=_params(("parallel", "parallel")),
        name="cmp_select_t",
    )(qt, kc, vc, tab, msel_t, ovr_t)


def _flash_init(m_ref, l_ref, acc_ref):
    m_ref[...] = jnp.full(m_ref.shape, NEG, F32)
    l_ref[...] = jnp.zeros(l_ref.shape, F32)
    acc_ref[...] = jnp.zeros(acc_ref.shape, F32)


def _flash_tile_t(k, qa_ref, q_rows, v_t, bias, m_ref, l_ref, acc_ref):
    cols = slice(0, qa_ref.shape[1])
    s = _dot(k, qa_ref[0:q_rows, cols])
    if bias is not None:
        s = s + bias(cols)
    m_old = m_ref[...]
    m_new = jnp.maximum(m_old, jnp.max(s, axis=0, keepdims=True))
    alpha = jnp.exp(m_old - m_new)
    p = jnp.exp(s - m_new)
    l_ref[...] = alpha * l_ref[...] + jnp.sum(p, axis=0, keepdims=True)
    acc_ref[...] = alpha * acc_ref[...] + _dot(v_t, p.astype(BF16))
    m_ref[...] = m_new


def _group_queries_t(qt, g, tq):
    row = lax.broadcasted_iota(jnp.int32, (LANES, tq), 0)
    mine = (row >= g * HEAD_DIM) & (row < (g + 1) * HEAD_DIM)
    zero = jnp.zeros((LANES, tq), qt.dtype)
    return jnp.concatenate([jnp.where(mine, qt[r * 128:(r + 1) * 128, :], zero) for r in range(ATT_REP)], axis=1)


def _unpack_heads_t(o_t, g, tq):
    rows = jnp.concatenate([o_t[:, r * tq:(r + 1) * tq].T for r in range(ATT_REP)], axis=0)
    return _pack_heads(rows, g, tq)


def _gate_chunks(gd, gexp):
    sig = 1.0 / (1.0 + jnp.exp(-gd))
    return _split_dot(sig, gexp, 3)


def _combine(gx, o_cmp, o_sel, o_win):
    return gx[:, 0:256] * o_cmp + gx[:, 256:512] * o_sel + gx[:, 512:768] * o_win


def _prompt_attn_kernel(qt_ref, pen_ref, oc_ref, gd_ref, ksa_ref, vst_ref, kw_ref, vwt_ref, a_ref, gexp_ref,
                        o_ref, m_ref, l_ref, acc_ref, qa_s, *, tq):
    qi = pl.program_id(1)
    g = pl.program_id(2)
    qa_s[0:128, :] = _group_queries_t(qt_ref[...], g, tq)
    qa_s[128:256, :] = jnp.concatenate([pen_ref[0, 0]] * ATT_REP, axis=1)

    def sel_tile(j, bias):
        start = pl.multiple_of(j * tq, tq)
        _flash_tile_t(ksa_ref[0, pl.ds(start, tq), :], qa_s, 256, vst_ref[:, pl.ds(start, tq)], bias,
                      m_ref, l_ref, acc_ref)

    def win_tile(j, bias):
        start = pl.multiple_of(j * tq, tq)
        _flash_tile_t(kw_ref[0, pl.ds(start, tq), :], qa_s, 128, vwt_ref[:, pl.ds(start, tq)], bias,
                      m_ref, l_ref, acc_ref)

    def near(tile_fn, n_tiles):
        def body(i, carry):
            tile_fn(qi - i, lambda cols: a_ref[0, i, :, cols])
            return carry
        lax.fori_loop(0, jnp.minimum(qi + 1, n_tiles), body, 0)

    _flash_init(m_ref, l_ref, acc_ref)
    near(sel_tile, 2)

    def far(j, carry):
        sel_tile(j, None)
        return carry
    lax.fori_loop(0, qi - 1, far, 0)
    o_sel = _unpack_heads_t(acc_ref[...] / l_ref[...], g, tq)

    _flash_init(m_ref, l_ref, acc_ref)
    near(win_tile, 3)
    o_win = _unpack_heads_t(acc_ref[...] / l_ref[...], g, tq)

    gx = _gate_chunks(gd_ref[0], gexp_ref[g])
    o_ref[0] = _combine(gx, oc_ref[0], o_sel, o_win).astype(BF16)


def _prompt_attn(qt, pen_t, o_cmp, gd, ksa, vst, kwb, vwt, atab, gexp, *, tq):
    b, t, _ = ksa.shape
    n_qt = t // tq
    gblk = lambda w: pl.BlockSpec((1, tq, w), lambda bi, qi, g: (bi, qi, g))
    seq = lambda w: pl.BlockSpec((1, t, w), lambda bi, qi, g: (bi, 0, 0))
    seq_t = pl.BlockSpec((128, t), lambda bi, qi, g: (0, bi))
    return pl.pallas_call(
        functools.partial(_prompt_attn_kernel, tq=tq),
        grid=(b, n_qt, ATT_KV_HEADS),
        in_specs=[pl.BlockSpec((512, tq), lambda bi, qi, g: (0, bi * n_qt + qi)),
                  pl.BlockSpec((1, 1, 128, tq), lambda bi, qi, g: (bi, g, 0, qi)),
                  gblk(256),
                  pl.BlockSpec((1, tq, 128), lambda bi, qi, g: (bi, qi, 0)),
                  seq(256), seq_t, seq(128), seq_t,
                  pl.BlockSpec((1,) + atab.shape[1:], lambda bi, qi, g: (g, 0, 0, 0)),
                  pl.BlockSpec(gexp.shape, lambda bi, qi, g: (0, 0, 0))],
        out_specs=gblk(256),
        out_shape=jax.ShapeDtypeStruct((b, t, 512), BF16),
        scratch_shapes=[pltpu.VMEM((1, ATT_REP * tq), F32), pltpu.VMEM((1, ATT_REP * tq), F32),
                        pltpu.VMEM((128, ATT_REP * tq), F32), pltpu.VMEM((256, ATT_REP * tq), BF16)],
        compiler_params=_params(("parallel", "arbitrary", "arbitrary")),
        name="prompt_attn",
    )(qt, pen_t, o_cmp, gd, ksa, vst, kwb, vwt, atab, gexp)


def _softmax_two(s_a, s_b, vt_a, v_b):
    m = jnp.maximum(jnp.max(s_a, axis=-1, keepdims=True), jnp.max(s_b, axis=-1, keepdims=True))
    p_a = jnp.exp(s_a - m)
    p_b = jnp.exp(s_b - m)
    l = jnp.sum(p_a, axis=-1, keepdims=True) + jnp.sum(p_b, axis=-1, keepdims=True)
    return (_dot_nt(p_a.astype(BF16), vt_a) + _dot(p_b.astype(BF16), v_b)) / l


def _slide_window(buf_t, new_t, n_new):
    w = buf_t.shape[1]
    rolled = pltpu.roll(buf_t, w - n_new, 1)
    lane = lax.broadcasted_iota(jnp.int32, (LANES, LANES), 1)
    last = jnp.where(lane >= LANES - n_new, new_t, rolled[:, w - LANES:w])
    return jnp.concatenate([rolled[:, 0:w - LANES], last], axis=1)


def _sample_attn_kernel(pt_ref, q_ref, pen_ref, oc_ref, gd_ref, ck_ref, cv_ref, ksn_ref, vsn_ref,
                        cw_ref, cvw_ref, kwn_ref, vwn_ref, kwt_ref, vwt_ref, eall_ref, tsel_ref, tnew_ref,
                        twin_ref, gexp_ref, o_ref, kwo_ref, vwo_ref, kbuf, vbuf, ksem, vsem,
                        *, n_pages, n_new):
    slot = _gather_pipeline(pt_ref, (ck_ref, cv_ref), (kbuf, vbuf), (ksem, vsem), n_pages, lane_major=True)
    tq = TQS
    q = q_ref[0]
    pen = pen_ref[0]
    kb = kbuf[slot].astype(BF16)
    vb = vbuf[slot].astype(BF16)
    kwb = cw_ref[0].astype(BF16)
    vwb = cvw_ref[0].astype(BF16)
    o_sel, o_win = [], []
    for g in range(ATT_KV_HEADS):
        qg = _group_queries(q, g, tq)
        pg = jnp.concatenate([pen[:, g * 128:(g + 1) * 128]] * ATT_REP, axis=0)
        s_past = _dot(qg, kb) + _dot(pg, eall_ref[...]) + tsel_ref[g, 0]
        s_new = _dot_nt(qg, ksn_ref[0]) + tnew_ref[g, 0]
        o_sel.append(_softmax_two(s_past, s_new, vb, vsn_ref[0]))
        s_buf = _dot(qg, kwb) + twin_ref[g, 0]
        s_new = _dot_nt(qg, kwn_ref[0]) + tnew_ref[g, 0]
        o_win.append(_softmax_two(s_buf, s_new, vwb, vwn_ref[0]))
    for g in range(ATT_KV_HEADS):
        gx = _gate_chunks(gd_ref[0], gexp_ref[g])
        o = _combine(gx, oc_ref[0, :, g * 256:(g + 1) * 256], _pack_heads(o_sel[g], g, tq), _pack_heads(o_win[g], g, tq))
        o_ref[0, :, g * 256:(g + 1) * 256] = o.astype(BF16)
    kwo_ref[0] = _slide_window(cw_ref[0], kwt_ref[0], n_new)
    vwo_ref[0] = _slide_window(cvw_ref[0], vwt_ref[0], n_new)


def _sample_attn(page_table, q, pen, o_cmp, gd, ck, cv, ksn, vsn, cw, cvw, kwn, vwn, kwt, vwt,
                 eall, tsel, tnew, twin, gexp, *, n_new):
    bs, n_pages = page_table.shape
    past = n_pages * PAGE_SIZE
    per_b = lambda a: pl.BlockSpec((1,) + a.shape[1:], lambda i, pt: (i,) + (0,) * (a.ndim - 1))
    full = lambda a: pl.BlockSpec(a.shape, lambda i, pt: (0,) * a.ndim)
    anyspec = pl.BlockSpec(memory_space=pl.ANY)
    ins = [q, pen, o_cmp, gd, ck, cv, ksn, vsn, cw, cvw, kwn, vwn, kwt, vwt, eall, tsel, tnew, twin, gexp]
    specs = [per_b(q), per_b(pen), per_b(o_cmp), per_b(gd), anyspec, anyspec, per_b(ksn), per_b(vsn),
             per_b(cw), per_b(cvw), per_b(kwn), per_b(vwn), per_b(kwt), per_b(vwt),
             full(eall), full(tsel), full(tnew), full(twin), full(gexp)]
    win = pl.BlockSpec((1, 128, WINDOW), lambda i, pt: (i, 0, 0))
    return pl.pallas_call(
        functools.partial(_sample_attn_kernel, n_pages=n_pages, n_new=n_new),
        grid_spec=pltpu.PrefetchScalarGridSpec(
            num_scalar_prefetch=1,
            grid=(bs,),
            in_specs=specs,
            out_specs=[pl.BlockSpec((1, TQS, 512), lambda i, pt: (i, 0, 0)), win, win],
            scratch_shapes=[pltpu.VMEM((2, 128, past), F32), pltpu.VMEM((2, 128, past), F32),
                            pltpu.SemaphoreType.DMA((2,)), pltpu.SemaphoreType.DMA((2,))]),
        out_shape=[jax.ShapeDtypeStruct((bs, TQS, 512), BF16),
                   jax.ShapeDtypeStruct((bs, 128, WINDOW), F32), jax.ShapeDtypeStruct((bs, 128, WINDOW), F32)],
        compiler_params=_params(("arbitrary",)),
        name="sample_attn",
    )(page_table, *ins)


def _lane_pair(cols, h0, h1, rows):
    lane = lax.broadcasted_iota(jnp.int32, (rows, LANES), 1)
    a = jnp.broadcast_to(cols[:, h0:h0 + 1], (rows, LANES))
    b = jnp.broadcast_to(cols[:, h1:h1 + 1], (rows, LANES))
    return jnp.where(lane < HEAD_DIM, a, b)


def _ssd_kernel(xbc_ref, z_ref, gd_ref, carry_ref, h0_ref, cw_ref, cb_ref, dtb_ref, arow_ref, dskip_ref,
                gnorm_ref, ltri_ref, y_ref, hout_ref, h_s, xfull, *, rows, n_valid):
    c = pl.program_id(1)
    halo = 8

    @pl.when(c == 0)
    def _():
        h_s[...] = h0_ref[0]
        xfull[0:halo, :] = carry_ref[0]

    xfull[halo:halo + rows, :] = xbc_ref[0]
    conv = cb_ref[...]
    for k in range(SSD_CONV):
        conv = conv + xfull[pl.ds(halo - (SSD_CONV - 1) + k, rows), :] * cw_ref[k:k + 1, :]
    tail = xfull[rows:rows + halo, :]
    xfull[0:halo, :] = tail
    xc = _silu(conv)
    xs = xc[:, 0:D_SSD]
    bm = xc[:, D_SSD:D_SSD + 128]
    cm = xc[:, D_SSD + 128:D_SSD + 256]

    lane = lax.broadcasted_iota(jnp.int32, (rows, LANES), 1)
    rowi = lax.broadcasted_iota(jnp.int32, (rows, LANES), 0)
    t = gd_ref[0] + dtb_ref[...]
    sp = jnp.maximum(t, 0.0) + jnp.log(1.0 + jnp.exp(-jnp.abs(t)))
    dt = jnp.where((lane >= DT_LANE) & (lane < DT_LANE + SSD_HEADS) & (rowi < n_valid), sp, 0.0)
    a = dt * arow_ref[...]
    acum = _split_dot_left(ltri_ref[...], a)
    acum_t = acum.T
    a_last = acum[rows - 1:rows, :]
    to_end = jnp.exp(a_last - acum)
    eac = jnp.exp(acum)
    dec = jnp.exp(a_last)

    li = lax.broadcasted_iota(jnp.int32, (rows, rows), 0)
    si = lax.broadcasted_iota(jnp.int32, (rows, rows), 1)
    causal = li >= si
    bmb = bm.astype(BF16)
    rowp = lax.broadcasted_iota(jnp.int32, (LANES, LANES), 0)
    cbs = []
    cmask = []
    for g in range(2):
        cg = jnp.where((lane >= g * 64) & (lane < (g + 1) * 64), cm, 0.0).astype(BF16)
        cmask.append(cg)
        cbs.append(_dot_nt(cg, bmb))
    for k in range(SSD_HEADS // 2):
        g = k // 2
        h0, h1 = DT_LANE + 2 * k, DT_LANE + 2 * k + 1
        xs_p = xs[:, k * 128:(k + 1) * 128]
        xdt = xs_p * _lane_pair(dt, h0, h1, rows)
        xdt_b = xdt.astype(BF16)
        ys = []
        for h in (h0, h1):
            seg = jnp.broadcast_to(acum[:, h:h + 1], (rows, rows)) - acum_t[h:h + 1, :]
            decay = jnp.where(causal, jnp.exp(jnp.where(causal, seg, 0.0)), 0.0)
            ys.append(_dot((cbs[g] * decay).astype(BF16), xdt_b))
        y = jnp.where(lane < HEAD_DIM, ys[0], ys[1])
        hp = h_s[k * 128:(k + 1) * 128, :]
        y = y + _dot_nt(cmask[g], hp.astype(BF16)) * _lane_pair(eac, h0, h1, rows)
        y = y + dskip_ref[:, k * 128:(k + 1) * 128] * xs_p
        xw = xdt * _lane_pair(to_end, h0, h1, rows)
        st = _dot(xw.T.astype(BF16), bmb)
        dfac = jnp.where(rowp < HEAD_DIM, dec[:, h0:h0 + 1], dec[:, h1:h1 + 1])
        h_s[k * 128:(k + 1) * 128, :] = hp * dfac + st
        xfull_y = y * _silu(z_ref[0, :, k * 128:(k + 1) * 128])
        y_ref[0, :, k * 128:(k + 1) * 128] = xfull_y.astype(y_ref.dtype)

    for g in range(2):
        yg = y_ref[0, :, g * 256:(g + 1) * 256].astype(F32)
        ms = jnp.mean(yg * yg, axis=-1, keepdims=True)
        y_ref[0, :, g * 256:(g + 1) * 256] = (yg * lax.rsqrt(ms + RMS_EPS)
                                               * gnorm_ref[:, g * 256:(g + 1) * 256]).astype(y_ref.dtype)

    @pl.when(c == pl.num_programs(1) - 1)
    def _():
        half = D_SSD // 2
        hout_ref[0, 0:half, :] = h_s[0:half, 0:SSD_STATE]
        hout_ref[0, half:D_SSD, :] = h_s[half:D_SSD, SSD_STATE:2 * SSD_STATE]


def _split_dot_left(tri, a):
    acc = None
    rem = a
    for _ in range(3):
        piece = rem.astype(BF16)
        rem = rem - piece.astype(F32)
        t = _dot(tri, piece)
        acc = t if acc is None else acc + t
    return acc


def _ssd(xbc, z, gd, carry, h0, wts, *, n_valid):
    b, l, _ = xbc.shape
    rows = SSD_CHUNK
    nc = l // rows
    blk = lambda w: pl.BlockSpec((1, rows, w), lambda bi, ci: (bi, ci, 0))
    per_b = lambda a: pl.BlockSpec((1,) + a.shape[1:], lambda bi, ci: (bi,) + (0,) * (a.ndim - 1))
    full = lambda a: pl.BlockSpec(a.shape, lambda bi, ci: (0,) * a.ndim)
    consts = [wts["conv_w"], wts["conv_b"], wts["dtb"], wts["arow"], wts["dskip"], wts["gnorm"], wts["ltri"]]
    return pl.pallas_call(
        functools.partial(_ssd_kernel, rows=rows, n_valid=n_valid),
        grid=(b, nc),
        in_specs=[blk(768), blk(512), blk(128), per_b(carry), per_b(h0)] + [full(a) for a in consts],
        out_specs=[blk(512), pl.BlockSpec((1, D_SSD, SSD_STATE), lambda bi, ci: (bi, 0, 0))],
        out_shape=[jax.ShapeDtypeStruct((b, l, 512), F32), jax.ShapeDtypeStruct((b, D_SSD, SSD_STATE), F32)],
        scratch_shapes=[pltpu.VMEM((512, 128), F32), pltpu.VMEM((rows + 8, 768), F32)],
        compiler_params=_params(("parallel", "arbitrary")),
        name="ssd",
    )(xbc, z, gd, carry, h0, *consts)


def _finish_kernel(x_ref, oa_ref, ys_ref, woa_ref, wos_ref, gf_ref, wg_ref, wu_ref, wd_ref,
                   y_ref, h_s, u_s, acc_s):
    f = pl.program_id(1)

    @pl.when(f == 0)
    def _():
        h = x_ref[...] + _dot(oa_ref[...], woa_ref[...]) + _dot(ys_ref[...].astype(BF16), wos_ref[...])
        h_s[...] = h
        ms = jnp.mean(h * h, axis=-1, keepdims=True)
        u_s[...] = (h * lax.rsqrt(ms + RMS_EPS) * gf_ref[...]).astype(BF16)
        acc_s[...] = jnp.zeros(acc_s.shape, F32)

    u = u_s[...]
    act = _silu(_dot(u, wg_ref[...])) * _dot(u, wu_ref[...])
    acc_s[...] += _dot(act.astype(BF16), wd_ref[...])

    @pl.when(f == pl.num_programs(1) - 1)
    def _():
        y_ref[...] = h_s[...] + acc_s[...]


def _finish(x2d, o_att, y_ssd, wts):
    n = x2d.shape[0]
    tm = min(512, n)
    nf = 2
    tf = D_FF // nf
    row = lambda w: pl.BlockSpec((tm, w), lambda i, f: (i, 0))
    full = lambda a: pl.BlockSpec(a.shape, lambda i, f: (0,) * a.ndim)
    return pl.pallas_call(
        _finish_kernel,
        grid=(n // tm, nf),
        in_specs=[row(D_MODEL), row(512), row(512), full(wts["wo_att"]), full(wts["wo_ssd"]), full(wts["gffn"]),
                  pl.BlockSpec((D_MODEL, tf), lambda i, f: (0, f)),
                  pl.BlockSpec((D_MODEL, tf), lambda i, f: (0, f)),
                  pl.BlockSpec((tf, D_MODEL), lambda i, f: (f, 0))],
        out_specs=row(D_MODEL),
        out_shape=jax.ShapeDtypeStruct((n, D_MODEL), F32),
        scratch_shapes=[pltpu.VMEM((tm, D_MODEL), F32), pltpu.VMEM((tm, D_MODEL), BF16),
                        pltpu.VMEM((tm, D_MODEL), F32)],
        compiler_params=_params(("parallel", "arbitrary")),
        name="finish",
    )(x2d, o_att, y_ssd, wts["wo_att"], wts["wo_ssd"], wts["gffn"], wts["w_gate"], wts["w_up"], wts["w_down"])


def _pair_perm():
    cols = []
    for r in range(ATT_REP):
        cols += list(range(r * HEAD_DIM, (r + 1) * HEAD_DIM))
        cols += list(range((ATT_REP + r) * HEAD_DIM, (ATT_REP + r + 1) * HEAD_DIM))
    return np.asarray(cols, np.int32)


def _block_ones(n, blk):
    i = np.arange(n)
    return (i[:, None] // blk == i[None, :] // blk).astype(np.float32) / blk


def _prep_weights(norm_mix, w_in, q_norm, k_norm, cmp_pe, cmp_w1, cmp_w2, conv_w, conv_b, dt_bias, a_log,
                  d_skip, ssd_norm, w_out, norm_ffn, w_gate, w_up, w_down):
    perm = _pair_perm()
    w = w_in
    gd = jnp.concatenate([w[:, OFF_GATE:OFF_Z], w[:, OFF_DT:D_IN],
                          jnp.zeros((D_MODEL, 128 - 3 * ATT_HEADS - SSD_HEADS), w.dtype)], axis=1)
    w_r = jnp.concatenate([w[:, :D_ATT][:, perm], w[:, OFF_KV:OFF_GATE], w[:, OFF_Z:OFF_XBC],
                           w[:, OFF_XBC:OFF_DT], gd], axis=1).astype(BF16)
    wts = dict(
        gmix=norm_mix.reshape(1, D_MODEL), w_in=w_r,
        qg=(jnp.tile(q_norm, ATT_HEADS) * (HEAD_DIM ** -0.5)).reshape(1, D_ATT),
        kg=jnp.tile(k_norm, ATT_KV_HEADS).reshape(1, D_KV),
        bq=jnp.asarray(_block_ones(D_ATT, HEAD_DIM), BF16), bk=jnp.asarray(_block_ones(D_KV, HEAD_DIM), BF16))

    def w1_big(w1):
        w1r = w1.reshape(2, CMP_STRIDE, HEAD_DIM, CMP_HIDDEN)
        eye = jnp.eye(ATT_KV_HEADS, dtype=w1.dtype)
        big = jnp.einsum("jsdh,gk->sgdjkh", w1r, eye)
        return big.reshape(CMP_STRIDE * D_KV, 2 * D_KV).astype(BF16)

    def pe_rows(pe):
        per = jnp.broadcast_to(pe.reshape(2, CMP_STRIDE, 1, HEAD_DIM), (2, CMP_STRIDE, ATT_KV_HEADS, HEAD_DIM))
        per = per.reshape(2, CMP_STRIDE * D_KV)
        return jnp.concatenate([per, jnp.zeros((6, CMP_STRIDE * D_KV), pe.dtype)], axis=0).astype(BF16)

    def w2_big(w2):
        eye = jnp.eye(ATT_KV_HEADS, dtype=w2.dtype)
        return jnp.einsum("hd,gk->ghkd", w2, eye).reshape(D_KV, D_KV).astype(BF16)

    wts.update(w1k=w1_big(cmp_w1[0]), w1v=w1_big(cmp_w1[1]), pek=pe_rows(cmp_pe[0]), pev=pe_rows(cmp_pe[1]),
               w2k=w2_big(cmp_w2[0]), w2v=w2_big(cmp_w2[1]))

    pad_lanes = lambda v: jnp.zeros((1, LANES), F32).at[0, DT_LANE:DT_LANE + SSD_HEADS].set(v)
    ltri = np.tril(np.ones((SSD_CHUNK, SSD_CHUNK), np.float32))
    wts.update(conv_w=jnp.concatenate([conv_w, jnp.zeros((4, D_CONV), F32)], axis=0), conv_b=conv_b.reshape(1, D_CONV),
               dtb=pad_lanes(dt_bias), arow=pad_lanes(-jnp.exp(a_log)),
               dskip=jnp.repeat(d_skip, 64).reshape(1, D_SSD), gnorm=ssd_norm.reshape(1, D_SSD),
               ltri=jnp.asarray(ltri, BF16))
    wts.update(wo_att=w_out[:D_ATT].astype(BF16), wo_ssd=w_out[D_ATT:].astype(BF16),
               gffn=norm_ffn.reshape(1, D_MODEL), w_gate=w_gate.astype(BF16), w_up=w_up.astype(BF16),
               w_down=w_down.astype(BF16))
    return wts


def _gate_expand():
    m = np.zeros((ATT_KV_HEADS, LANES, 3 * ATT_REP * HEAD_DIM), np.float32)
    for g in range(ATT_KV_HEADS):
        for r in range(ATT_REP):
            for br in range(3):
                c0 = br * ATT_REP * HEAD_DIM + r * HEAD_DIM
                m[g, g * 3 * ATT_REP + r * 3 + br, c0:c0 + HEAD_DIM] = 1.0
    return jnp.asarray(m, BF16)


def _sel_matrix(n_blk_pad, n_sel):
    m = np.zeros((n_blk_pad, LANES), np.float32)
    for n in range(n_blk_pad - 1):
        for j in {n // 4, (n + 1) // 4}:
            if j < n_sel:
                m[n, j] = 1.0
    return jnp.asarray(m, BF16)


def _override(q_pos, n_sel):
    j = np.arange(LANES)[None, :]
    cur = (q_pos // SEL_BLOCK)[:, None]
    forced = (j == 0) | (j == cur) | (j == cur - 1)
    ovr = np.where(forced, 1e30, np.where(j <= cur, 0.0, -1e30))
    ovr = np.where(j < n_sel, ovr, -1e30)
    return ovr.astype(np.float32)


def _prompt_tables(rel_bias, t):
    n_qt = t // TQ
    pos = np.arange(t)
    nb = t // CMP_STRIDE
    e = CMP_STRIDE * np.arange(nb) + (CMP_BLOCK - 1)
    dist = pos[None, :] - e[:, None]
    cmp_idx = _idx_table(dist, (dist >= 0) & (np.arange(nb)[:, None] < nb - 1))
    cmp_idx = cmp_idx.reshape(nb, n_qt, TQ).transpose(1, 0, 2)
    i = np.arange(TQ)[None, :]
    j = np.arange(TQ)[:, None]
    diag = _idx_table(i - j, i >= j)
    prev = _idx_table(TQ + i - j, np.ones((TQ, TQ), bool))
    prev2 = _idx_table(2 * TQ + i - j, (2 * TQ + i - j) < WINDOW)
    att_idx = np.stack([diag, prev, prev2])
    return _bias_tables(rel_bias, cmp_idx, stack_cols=True), _bias_tables(rel_bias, att_idx, stack_cols=True)


def _prompt_layer(x, wts, rel_bias):
    b, t, _ = x.shape
    n = b * t
    p = _project(x.reshape(n, D_MODEL), wts, t, "prompt")
    n_sub = t // CMP_STRIDE
    sub = lambda a: a.reshape(b, n_sub, CMP_STRIDE * D_KV)
    kcmp, vcmp = _compress_prompt(sub(p["kc_b"]), sub(p["vc_b"]), wts)
    cmp_tab, att_tab = _prompt_tables(rel_bias, t)
    n_sel = t // SEL_BLOCK
    ovr_t = jnp.asarray(_override(np.arange(t), n_sel).reshape(t // TQ, TQ, LANES).transpose(0, 2, 1))
    r3 = lambda a: a.reshape(b, t, a.shape[-1])
    o_cmp, pen_t = _cmp_select_t(p["q_t"], kcmp, vcmp, cmp_tab, _sel_matrix(n_sub, n_sel).T, ovr_t,
                                 tq=TQ, n_sel=n_sel, k_top=min(SEL_TOPN, n_sel))
    o_att = _prompt_attn(p["q_t"], pen_t, o_cmp, r3(p["gd"]), r3(p["ksa"]), p["vs_tb"], r3(p["kw_b"]), p["vw_tb"],
                         att_tab, _gate_expand(), tq=TQ)
    carry = jnp.zeros((b, 8, D_CONV), F32)
    h0 = jnp.zeros((b, 512, 128), F32)
    y_ssd, h_fin = _ssd(r3(p["xbc"]), r3(p["z"]), r3(p["gd"]), carry, h0, wts, n_valid=SSD_CHUNK)
    y = _finish(x.reshape(n, D_MODEL), o_att.reshape(n, 512), y_ssd.reshape(n, 512), wts)
    wb = min(WINDOW, t)
    kv_out = {k: _token_major(p[k + "_t"]) for k in KV_NAMES}
    ssm = h_fin.reshape(b, SSD_HEADS, 64, SSD_STATE)
    return y.reshape(b, t, D_MODEL), (kv_out["kc"], kv_out["vc"], kv_out["ks"], kv_out["vs"],
                                      kv_out["kw"][:, t - wb:], kv_out["vw"][:, t - wb:],
                                      r3(p["xbc"])[:, t - (SSD_CONV - 1):], ssm)


def _token_major(a_t):
    b, _, t = a_t.shape
    return jnp.swapaxes(a_t, 1, 2).reshape(b, t, ATT_KV_HEADS, HEAD_DIM)


def _feature_major(a):
    return jnp.swapaxes(a.reshape(a.shape[:-2] + (D_KV,)), -1, -2)


def _sample_tables(rel_bias, past, s_new, n_blk_pad):
    s = np.minimum(np.arange(TQS), s_new - 1)[:, None]
    pos = past + s
    nidx = np.arange(n_blk_pad)[None, :]
    e = CMP_STRIDE * nidx + (CMP_BLOCK - 1)
    cmp_idx = _idx_table(pos - e, (e <= pos) & (nidx < n_blk_pad - 1))[None]
    key = np.arange(past)[None, :]
    sel_idx = _idx_table(pos - key, np.ones((TQS, past), bool))[None]
    jn = np.arange(LANES)[None, :]
    new_idx = _idx_table(s - jn, (jn <= s) & (jn < s_new))[None]
    wi = np.arange(WINDOW)[None, :]
    wdist = pos - (past - WINDOW + wi)
    win_idx = _idx_table(wdist, (wdist >= 0) & (wdist < WINDOW))[None]
    tabs = [_bias_tables(rel_bias, t) for t in (cmp_idx, sel_idx, new_idx, win_idx)]
    return tabs, pos[:, 0]


def _sample_layer(x, c_kc, c_vc, c_ks, c_vs, c_kw, c_vw, s_conv, s_ssm, page_table, wts, rel_bias):
    bs, s_new, _ = x.shape
    n = bs * s_new
    n_pages = page_table.shape[1]
    past = n_pages * PAGE_SIZE
    p = _project(x.reshape(n, D_MODEL), wts, s_new, "sample")
    r3 = lambda a: a.reshape(bs, s_new, a.shape[-1])
    padq = lambda a, rows: jnp.pad(r3(a), ((0, 0), (0, rows - s_new), (0, 0)))
    new_t = lambda a_t: jnp.pad(jnp.swapaxes(a_t.reshape(D_KV, bs, s_new), 0, 1), ((0, 0), (0, 0), (LANES - s_new, 0)))
    kcmp, vcmp = _compress_sample(page_table, _feature_major(c_kc), _feature_major(c_vc), wts)
    n_blk_pad = past // CMP_STRIDE
    (cmp_tab, sel_tab, new_tab, win_tab), pos = _sample_tables(rel_bias, past, s_new, n_blk_pad)
    n_sel = past // SEL_BLOCK
    ovr = jnp.asarray(_override(pos, n_sel)[None])
    qp = padq(p["q"], TQS)
    o_cmp, pen = _cmp_select(qp, kcmp, vcmp, cmp_tab, _sel_matrix(n_blk_pad, n_sel), ovr,
                             tq=TQS, n_sel=n_sel, k_top=min(SEL_TOPN - 1, n_sel))
    blk_of_key = np.arange(past)[None, :] // SEL_BLOCK
    eall = jnp.asarray((np.arange(LANES)[:, None] == blk_of_key).astype(np.float32), BF16)
    o_att, kw_new, vw_new = _sample_attn(
        page_table, qp, pen, o_cmp, padq(p["gd"], TQS),
        _feature_major(c_ks), _feature_major(c_vs), padq(p["ks_b"], LANES), padq(p["vs_b"], LANES),
        _feature_major(c_kw), _feature_major(c_vw), padq(p["kw_b"], LANES), padq(p["vw_b"], LANES),
        new_t(p["kw_t"]), new_t(p["vw_t"]),
        eall, sel_tab, new_tab, win_tab, _gate_expand(), n_new=s_new)
    carry = jnp.pad(s_conv, ((0, 0), (8 - (SSD_CONV - 1), 0), (0, 0)))
    h0 = s_ssm.reshape(bs, 512, SSD_STATE)
    h0 = jnp.concatenate([h0, h0], axis=-1)
    y_ssd, h_fin = _ssd(padq(p["xbc"], SSD_CHUNK), padq(p["z"], SSD_CHUNK), padq(p["gd"], SSD_CHUNK), carry, h0, wts,
                        n_valid=s_new)
    y = _finish(x.reshape(n, D_MODEL), o_att[:, :s_new].reshape(n, 512), y_ssd[:, :s_new].reshape(n, 512), wts)
    kv4 = lambda a: a.reshape(bs, -1, ATT_KV_HEADS, HEAD_DIM)
    ssm = h_fin.reshape(bs, SSD_HEADS, 64, SSD_STATE)
    conv_state = jnp.concatenate([s_conv, r3(p["xbc"])], axis=1)[:, s_new:]
    return y.reshape(bs, s_new, D_MODEL), (kv4(p["kc"]), kv4(p["vc"]), kv4(p["ks"]), kv4(p["vs"]),
                                           _token_major(kw_new), _token_major(vw_new), conv_state, ssm)


def kernel(x_prompt, x_sample, cache_k_cmp, cache_v_cmp, cache_k_sel, cache_v_sel, cache_k_win, cache_v_win,
           state_conv, state_ssm, page_table, norm_mix, w_in, q_norm, k_norm, cmp_pe, cmp_w1, cmp_w2, rel_bias,
           conv_w, conv_b, dt_bias, a_log, d_skip, ssd_norm, w_out, norm_ffn, w_gate, w_up, w_down):
    depth = w_in.shape[0]
    y_p, y_s = x_prompt, x_sample
    p_states, s_states = [], []
    for l in range(depth):
        wts = _prep_weights(norm_mix[l], w_in[l], q_norm[l], k_norm[l], cmp_pe[l], cmp_w1[l], cmp_w2[l],
                            conv_w[l], conv_b[l], dt_bias[l], a_log[l], d_skip[l], ssd_norm[l], w_out[l],
                            norm_ffn[l], w_gate[l], w_up[l], w_down[l])
        y_p, st_p = _prompt_layer(y_p, wts, rel_bias)
        y_s, st_s = _sample_layer(y_s, cache_k_cmp[l], cache_v_cmp[l], cache_k_sel[l], cache_v_sel[l],
                                  cache_k_win[l], cache_v_win[l], state_conv[l], state_ssm[l], page_table,
                                  wts, rel_bias)
        p_states.append(st_p)
        s_states.append(st_s)
    p_out = [jnp.stack(a) for a in zip(*p_states)]
    s_out = [jnp.stack(a) for a in zip(*s_states)]
    return (y_p, y_s, *p_out, *s_out)
```

```python
import functools
import math

import numpy as np
import jax
import jax.numpy as jnp
from jax import lax
from jax.experimental import pallas as pl
from jax.experimental.pallas import tpu as pltpu

F32 = jnp.float32
BF16 = jnp.bfloat16

D_MODEL = 1024
HEAD_DIM = 64
ATT_HEADS = 8
ATT_KV_HEADS = 2
ATT_REP = ATT_HEADS // ATT_KV_HEADS
CMP_BLOCK = 32
CMP_STRIDE = 16
CMP_HIDDEN = 64
SEL_BLOCK = 64
SEL_TOPN = 16
WINDOW = 512
N_BUCKETS = 32
MAX_DISTANCE = 128
PAGE_SIZE = 128
SSD_HEADS = 8
SSD_STATE = 64
SSD_CONV = 4
SSD_CHUNK = 128
D_ATT = ATT_HEADS * HEAD_DIM
D_SSD = SSD_HEADS * 64
D_KV = ATT_KV_HEADS * HEAD_DIM
D_CONV = D_SSD + 2 * 2 * SSD_STATE
D_FF = ((8 * D_MODEL // 3 + 255) // 256) * 256
OFF_KV = D_ATT
OFF_GATE = OFF_KV + 6 * D_KV
OFF_Z = OFF_GATE + 3 * ATT_HEADS
OFF_XBC = OFF_Z + D_SSD
OFF_DT = OFF_XBC + D_CONV
D_IN = OFF_DT + SSD_HEADS
RMS_EPS = 1e-6
NEG = -1e30

C_Q, C_KV, C_Z, C_XBC, C_GD = 0, 512, 1280, 1792, 2560
D_INR = 2688
DT_LANE = 24
LANES = 128
TQ = 256
TQS = 16
VMEM_LIMIT = 48 * 1024 * 1024


def _dot(a, b):
    return jnp.dot(a, b, preferred_element_type=F32)


def _dot_nt(a, b):
    return lax.dot_general(a, b, (((1,), (1,)), ((), ())), preferred_element_type=F32)


def _split_dot(a, b, parts):
    acc = None
    rem = a
    for _ in range(parts):
        piece = rem.astype(BF16)
        rem = rem - piece.astype(F32)
        t = _dot(piece, b)
        acc = t if acc is None else acc + t
    return acc


def _silu(x):
    return x * (1.0 / (1.0 + jnp.exp(-x)))


def _params(sem=None):
    kw = dict(vmem_limit_bytes=VMEM_LIMIT)
    if sem is not None:
        kw["dimension_semantics"] = sem
    return pltpu.CompilerParams(**kw)


KV_NAMES = ("kc", "vc", "ks", "vs", "kw", "vw")
PROJ_OUTPUTS = {
    "prompt": ([(k + "_t", "seq", 128, F32) for k in KV_NAMES]
               + [("kc_b", "row", 128, BF16), ("vc_b", "row", 128, BF16), ("ksa", "row", 256, BF16),
                  ("kw_b", "row", 128, BF16), ("q_t", "col", 512, BF16), ("vs_tb", "col", 128, BF16),
                  ("vw_tb", "col", 128, BF16)]),
    "sample": ([(k, "row", 128, F32) for k in KV_NAMES]
               + [("q", "row", 512, BF16), ("ks_b", "row", 128, BF16), ("vs_b", "row", 128, BF16),
                  ("kw_b", "row", 128, BF16), ("vw_b", "row", 128, BF16),
                  ("kw_t", "col", 128, F32), ("vw_t", "col", 128, F32)]),
}
PROJ_COMMON = [("z", "row", 512, F32), ("xbc", "row", 768, F32), ("gd", "row", 128, F32)]


def _proj_kernel(x_ref, gmix_ref, w_ref, qg_ref, kg_ref, bq_ref, bk_ref, *out_refs, names, tm, t_len):
    out = dict(zip(names, out_refs))
    x = x_ref[...]
    ms = jnp.mean(x * x, axis=-1, keepdims=True)
    u = (x * lax.rsqrt(ms + RMS_EPS) * gmix_ref[...]).astype(BF16)

    def proj(lo, hi):
        return _dot(u, w_ref[:, lo:hi])

    def headnorm(v, b_ref, g_ref):
        msq = _dot((v * v).astype(BF16), b_ref[...])
        return v * lax.rsqrt(msq + RMS_EPS) * g_ref[...]

    def put(name, value):
        if name in out:
            ref = out[name]
            ref[...] = value().astype(ref.dtype).reshape(ref.shape)

    q = headnorm(proj(C_Q, C_Q + 512), bq_ref, qg_ref)
    put("q", lambda: q)
    put("q_t", lambda: q.T)
    kv = {}
    for i, name in enumerate(KV_NAMES):
        v = proj(C_KV + 128 * i, C_KV + 128 * (i + 1))
        kv[name] = headnorm(v, bk_ref, kg_ref) if name in ("ks", "kw") else v
    for name, v in kv.items():
        put(name, lambda v=v: v)
        put(name + "_b", lambda v=v: v)
        put(name + "_t", lambda v=v: v.T)
        put(name + "_tb", lambda v=v: v.T)
    if "ksa" in out:
        row = pl.program_id(0) * tm + lax.broadcasted_iota(jnp.int32, (tm, LANES), 0)
        blk = (row % t_len) // SEL_BLOCK
        lane = lax.broadcasted_iota(jnp.int32, (tm, LANES), 1)
        out["ksa"][:, 0:128] = kv["ks"].astype(BF16)
        out["ksa"][:, 128:256] = jnp.where(lane == blk, 1.0, 0.0).astype(BF16)
    put("z", lambda: proj(C_Z, C_Z + 512))
    put("xbc", lambda: proj(C_XBC, C_XBC + 768))
    put("gd", lambda: proj(C_GD, C_GD + 128))


def _project(x2d, wts, t_len, mode):
    n = x2d.shape[0]
    tm = min(512, n)
    per_seq = max(t_len // tm, 1)
    full = lambda a: pl.BlockSpec(a.shape, lambda i: (0,) * a.ndim)
    ins = [x2d, wts["gmix"], wts["w_in"], wts["qg"], wts["kg"], wts["bq"], wts["bk"]]
    outs = PROJ_OUTPUTS[mode] + PROJ_COMMON
    specs, shapes = [], []
    for _, layout, w, dt in outs:
        if layout == "row":
            specs.append(pl.BlockSpec((tm, w), lambda i: (i, 0)))
            shapes.append(jax.ShapeDtypeStruct((n, w), dt))
        elif layout == "col":
            specs.append(pl.BlockSpec((w, tm), lambda i: (0, i)))
            shapes.append(jax.ShapeDtypeStruct((w, n), dt))
        else:
            specs.append(pl.BlockSpec((1, w, tm), lambda i: (i // per_seq, 0, i % per_seq)))
            shapes.append(jax.ShapeDtypeStruct((n // t_len, w, t_len), dt))
    names = tuple(name for name, _, _, _ in outs)
    res = pl.pallas_call(
        functools.partial(_proj_kernel, names=names, tm=tm, t_len=t_len),
        grid=(n // tm,),
        in_specs=[pl.BlockSpec((tm, D_MODEL), lambda i: (i, 0))] + [full(a) for a in ins[1:]],
        out_specs=specs,
        out_shape=shapes,
        compiler_params=_params(("parallel",)),
        name="proj",
    )(*ins)
    return dict(zip(names, res))


def _bucket_np(dist):
    n = np.maximum(dist, 0)
    max_exact = N_BUCKETS // 2
    nf = np.maximum(n, 1).astype(np.float64)
    large = max_exact + (np.log(nf / max_exact) / math.log(MAX_DISTANCE / max_exact)
                         * (N_BUCKETS - max_exact)).astype(np.int64)
    large = np.minimum(large, N_BUCKETS - 1)
    return np.where(n < max_exact, n, large).astype(np.int32)


def _idx_table(dist, valid):
    return np.where(valid, _bucket_np(dist), -1).astype(np.int32)


def _table_kernel(rb_ref, idx_ref, out_ref):
    h = pl.program_id(0) * ATT_REP + pl.program_id(2)
    idx = idx_ref[0]
    far = rb_ref[N_BUCKETS - 1, h]
    acc = jnp.zeros(idx.shape, F32)
    for b in range(N_BUCKETS - 1):
        acc = jnp.where(idx == b, rb_ref[b, h] - far, acc)
    out_ref[...] = jnp.where(idx < 0, NEG, acc).reshape(out_ref.shape)


def _bias_tables(rel_bias, idx, stack_cols=False):
    k, r, c = idx.shape
    if stack_cols:
        return pl.pallas_call(
            _table_kernel,
            grid=(ATT_KV_HEADS, k, ATT_REP),
            in_specs=[pl.BlockSpec(memory_space=pltpu.SMEM),
                      pl.BlockSpec((1, r, c), lambda g, kk, rr: (kk, 0, 0))],
            out_specs=pl.BlockSpec((1, 1, r, c), lambda g, kk, rr: (g, kk, 0, rr)),
            out_shape=jax.ShapeDtypeStruct((ATT_KV_HEADS, k, r, ATT_REP * c), F32),
            name="bias_table_t",
        )(rel_bias, jnp.asarray(idx))
    out = pl.pallas_call(
        _table_kernel,
        grid=(ATT_KV_HEADS, k, ATT_REP),
        in_specs=[pl.BlockSpec(memory_space=pltpu.SMEM),
                  pl.BlockSpec((1, r, c), lambda g, kk, rr: (kk, 0, 0))],
        out_specs=pl.BlockSpec((1, 1, 1, r, c), lambda g, kk, rr: (g, kk, rr, 0, 0)),
        out_shape=jax.ShapeDtypeStruct((ATT_KV_HEADS, k, ATT_REP, r, c), F32),
        name="bias_table",
    )(rel_bias, jnp.asarray(idx))
    return out.reshape(ATT_KV_HEADS, k, ATT_REP * r, c)


def _compress_core(x, w1, pe, w2):
    return _compress_tail(_dot(x.astype(BF16), w1), w1, pe, w2)


def _compress_tail(u, w1, pe, w2):
    n_sub = u.shape[0]
    upe = _dot(pe, w1)
    nxt = pltpu.roll(u[:, 128:256], n_sub - 1, 0)
    pre = u[:, 0:128] + nxt + upe[0:1, 0:128] + upe[1:2, 128:256]
    return _dot(_silu(pre).astype(BF16), w2)


def _knorm(v, bk, kg):
    msq = _dot((v * v).astype(BF16), bk)
    return v * lax.rsqrt(msq + RMS_EPS) * kg


def _compress_prompt_kernel(kc_ref, vc_ref, w1k_ref, w1v_ref, pek_ref, pev_ref, w2k_ref, w2v_ref,
                            kg_ref, bk_ref, ko_ref, vo_ref):
    kc = _compress_core(kc_ref[0], w1k_ref[...], pek_ref[...], w2k_ref[...])
    ko_ref[0] = _knorm(kc, bk_ref[...], kg_ref[...]).astype(BF16)
    vo_ref[0] = _compress_core(vc_ref[0], w1v_ref[...], pev_ref[...], w2v_ref[...]).astype(BF16)


def _compress_prompt(kc, vc, wts):
    b, n_sub, w = kc.shape
    full = lambda a: pl.BlockSpec(a.shape, lambda i: (0,) * a.ndim)
    consts = [wts["w1k"], wts["w1v"], wts["pek"], wts["pev"], wts["w2k"], wts["w2v"], wts["kg"], wts["bk"]]
    blk = pl.BlockSpec((1, n_sub, w), lambda i: (i, 0, 0))
    oblk = pl.BlockSpec((1, n_sub, 128), lambda i: (i, 0, 0))
    return pl.pallas_call(
        _compress_prompt_kernel,
        grid=(b,),
        in_specs=[blk, blk] + [full(a) for a in consts],
        out_specs=[oblk, oblk],
        out_shape=[jax.ShapeDtypeStruct((b, n_sub, 128), BF16)] * 2,
        compiler_params=_params(("parallel",)),
        name="compress_prompt",
    )(kc, vc, *consts)


def _page_copy(cache_ref, page, buf_ref, slot, p, sem_ref, lane_major):
    if lane_major:
        dst = buf_ref.at[slot, :, pl.ds(pl.multiple_of(p * PAGE_SIZE, PAGE_SIZE), PAGE_SIZE)]
    else:
        dst = buf_ref.at[slot, p]
    return pltpu.make_async_copy(cache_ref.at[page], dst, sem_ref.at[slot])


def _gather_start(pt_ref, b, slot, caches, bufs, sems, n_pages, lane_major):
    def body(p, carry):
        page = pt_ref[b, p]
        for cache_ref, buf_ref, sem_ref in zip(caches, bufs, sems):
            _page_copy(cache_ref, page, buf_ref, slot, p, sem_ref, lane_major).start()
        return carry
    lax.fori_loop(0, n_pages, body, 0)


def _gather_wait(slot, caches, bufs, sems, n_pages, lane_major):
    def body(p, carry):
        for cache_ref, buf_ref, sem_ref in zip(caches, bufs, sems):
            _page_copy(cache_ref, 0, buf_ref, slot, p, sem_ref, lane_major).wait()
        return carry
    lax.fori_loop(0, n_pages, body, 0)


def _gather_pipeline(pt_ref, caches, bufs, sems, n_pages, lane_major):
    b = pl.program_id(0)
    nb = pl.num_programs(0)
    slot = b % 2

    @pl.when(b == 0)
    def _():
        _gather_start(pt_ref, 0, 0, caches, bufs, sems, n_pages, lane_major)

    @pl.when(b + 1 < nb)
    def _():
        _gather_start(pt_ref, b + 1, 1 - slot, caches, bufs, sems, n_pages, lane_major)

    _gather_wait(slot, caches, bufs, sems, n_pages, lane_major)
    return slot


def _compress_paged(buf, slot, rows_s, perm, w1, pe, w2, n_pages):
    groups = PAGE_SIZE // CMP_STRIDE

    def body(i, carry):
        pair = buf[slot, pl.ds(2 * i, 2)].reshape(2 * D_KV, PAGE_SIZE)
        rows = _dot_nt(perm, pair.astype(BF16))
        for half in range(2):
            start = pl.multiple_of((2 * i + half) * groups, groups)
            for s in range(CMP_STRIDE):
                rows_s[s // 2, pl.ds(start, groups), (s % 2) * D_KV:(s % 2 + 1) * D_KV] = (
                    rows[s * groups:(s + 1) * groups, half * D_KV:(half + 1) * D_KV])
        return carry
    lax.fori_loop(0, n_pages // 2, body, 0, unroll=4)
    u = None
    for j in range(CMP_STRIDE // 2):
        t = _dot(rows_s[j].astype(BF16), w1[2 * j * D_KV:(2 * j + 2) * D_KV, :])
        u = t if u is None else u + t
    return _compress_tail(u, w1, pe, w2)


def _compress_sample_kernel(pt_ref, ck_ref, cv_ref, perm_ref, w1k_ref, w1v_ref, pek_ref, pev_ref, w2k_ref, w2v_ref,
                            kg_ref, bk_ref, ko_ref, vo_ref, kbuf, vbuf, rows_s, ksem, vsem, *, n_pages):
    slot = _gather_pipeline(pt_ref, (ck_ref, cv_ref), (kbuf, vbuf), (ksem, vsem), n_pages, lane_major=False)
    perm = perm_ref[...]
    kc = _compress_paged(kbuf, slot, rows_s, perm, w1k_ref[...], pek_ref[...], w2k_ref[...], n_pages)
    ko_ref[0] = _knorm(kc, bk_ref[...], kg_ref[...]).astype(BF16)
    vo_ref[0] = _compress_paged(vbuf, slot, rows_s, perm, w1v_ref[...], pev_ref[...], w2v_ref[...], n_pages).astype(BF16)


def _compress_sample(page_table, ck, cv, wts):
    bs, n_pages = page_table.shape
    n_sub = n_pages * PAGE_SIZE // CMP_STRIDE
    full = lambda a: pl.BlockSpec(a.shape, lambda i, pt: (0,) * a.ndim)
    groups = PAGE_SIZE // CMP_STRIDE
    r = np.arange(PAGE_SIZE)
    perm = jnp.asarray((np.arange(PAGE_SIZE)[None, :] == (CMP_STRIDE * (r % groups) + r // groups)[:, None])
                       .astype(np.float32), BF16)
    consts = [perm, wts["w1k"], wts["w1v"], wts["pek"], wts["pev"], wts["w2k"], wts["w2v"], wts["kg"], wts["bk"]]
    anyspec = pl.BlockSpec(memory_space=pl.ANY)
    oblk = pl.BlockSpec((1, n_sub, 128), lambda i, pt: (i, 0, 0))
    pages = pltpu.VMEM((2, n_pages, D_KV, PAGE_SIZE), F32)
    return pl.pallas_call(
        functools.partial(_compress_sample_kernel, n_pages=n_pages),
        grid_spec=pltpu.PrefetchScalarGridSpec(
            num_scalar_prefetch=1,
            grid=(bs,),
            in_specs=[anyspec, anyspec] + [full(a) for a in consts],
            out_specs=[oblk, oblk],
            scratch_shapes=[pages, pages, pltpu.VMEM((CMP_STRIDE // 2, n_sub, 2 * D_KV), F32),
                            pltpu.SemaphoreType.DMA((2,)), pltpu.SemaphoreType.DMA((2,))]),
        out_shape=[jax.ShapeDtypeStruct((bs, n_sub, 128), BF16)] * 2,
        compiler_params=_params(("arbitrary",)),
        name="compress_sample",
    )(page_table, ck, cv, *consts)


def _group_queries(q, g, tq):
    lane = lax.broadcasted_iota(jnp.int32, (tq, LANES), 1)
    mine = (lane >= g * HEAD_DIM) & (lane < (g + 1) * HEAD_DIM)
    zero = jnp.zeros((tq, LANES), q.dtype)
    return jnp.concatenate([jnp.where(mine, q[:, r * 128:(r + 1) * 128], zero) for r in range(ATT_REP)], axis=0)


def _pack_heads(o, g, tq):
    lane = lax.broadcasted_iota(jnp.int32, (tq, LANES), 1)
    first = g == 0
    chunks = []
    for k in range(ATT_REP // 2):
        a = o[2 * k * tq:(2 * k + 1) * tq]
        b = o[(2 * k + 1) * tq:(2 * k + 2) * tq]
        lo = jnp.where(first, a, pltpu.roll(a, HEAD_DIM, 1))
        hi = jnp.where(first, pltpu.roll(b, HEAD_DIM, 1), b)
        chunks.append(jnp.where(lane < HEAD_DIM, lo, hi))
    return jnp.concatenate(chunks, axis=1)


def _cmp_select_kernel(q_ref, kc_ref, vc_ref, tab_ref, msel_ref, ovr_ref, o_ref, pen_ref, *, tq, n_sel, k_top):
    q = q_ref[0]
    kc = kc_ref[0]
    vc = vc_ref[0]
    ovr = ovr_ref[0]
    lane = lax.broadcasted_iota(jnp.int32, (tq, LANES), 1)
    outs = []
    for g in range(ATT_KV_HEADS):
        tab = tab_ref[g, 0]
        s = _dot_nt(_group_queries(q, g, tq), kc) + tab
        m = jnp.max(s, axis=-1, keepdims=True)
        e = jnp.where(tab > 0.5 * NEG, jnp.exp(s - m), 0.0)
        p = e / jnp.maximum(jnp.sum(e, axis=-1, keepdims=True), 1e-30)
        outs.append(_dot(p.astype(BF16), vc))
        imp = p[0:tq]
        for r in range(1, ATT_REP):
            imp = imp + p[r * tq:(r + 1) * tq]
        score = _split_dot(imp, msel_ref[...], 3)
        score = jnp.where(ovr == 0.0, score, ovr)
        rank = jnp.zeros((tq, LANES), F32)
        for j in range(n_sel):
            col = score[:, j:j + 1]
            beats = (col > score) | ((col == score) & (lane > j))
            rank = rank + jnp.where(beats, 1.0, 0.0)
        pen_ref[0, :, g * 128:(g + 1) * 128] = jnp.where(rank < k_top, 0.0, NEG).astype(BF16)
    for g in range(ATT_KV_HEADS):
        o_ref[0, :, g * 256:(g + 1) * 256] = _pack_heads(outs[g], g, tq)


def _cmp_select(q, kc, vc, tab, msel, ovr, *, tq, n_sel, k_top):
    b, t, _ = q.shape
    nb = kc.shape[1]
    n_qt = t // tq
    return pl.pallas_call(
        functools.partial(_cmp_select_kernel, tq=tq, n_sel=n_sel, k_top=k_top),
        grid=(n_qt, b),
        in_specs=[pl.BlockSpec((1, tq, 512), lambda qi, bi: (bi, qi, 0)),
                  pl.BlockSpec((1, nb, 128), lambda qi, bi: (bi, 0, 0)),
                  pl.BlockSpec((1, nb, 128), lambda qi, bi: (bi, 0, 0)),
                  pl.BlockSpec((ATT_KV_HEADS, 1, ATT_REP * tq, nb), lambda qi, bi: (0, qi, 0, 0)),
                  pl.BlockSpec(msel.shape, lambda qi, bi: (0, 0)),
                  pl.BlockSpec((1, tq, 128), lambda qi, bi: (qi, 0, 0))],
        out_specs=[pl.BlockSpec((1, tq, 512), lambda qi, bi: (bi, qi, 0)),
                   pl.BlockSpec((1, tq, 256), lambda qi, bi: (bi, qi, 0))],
        out_shape=[jax.ShapeDtypeStruct((b, t, 512), F32), jax.ShapeDtypeStruct((b, t, 256), BF16)],
        compiler_params=_params(("parallel", "parallel")),
        name="cmp_select",
    )(q, kc, vc, tab, msel, ovr)


def _cmp_select_t_kernel(qt_ref, kc_ref, vc_ref, tab_ref, msel_ref, ovr_ref, o_ref, pen_ref,
                         *, tq, n_sel, k_top):
    qt = qt_ref[...]
    kc = kc_ref[0]
    vct = vc_ref[0].astype(F32).T.astype(BF16)
    ovr = ovr_ref[0, 0:n_sel, :]
    row = lax.broadcasted_iota(jnp.int32, (n_sel, tq), 0)
    for g in range(ATT_KV_HEADS):
        tab = tab_ref[g, 0]
        s = _dot(kc, _group_queries_t(qt, g, tq)) + tab
        m = jnp.max(s, axis=0, keepdims=True)
        e = jnp.where(tab > 0.5 * NEG, jnp.exp(s - m), 0.0)
        p = e / jnp.maximum(jnp.sum(e, axis=0, keepdims=True), 1e-30)
        o_ref[0, :, g * 256:(g + 1) * 256] = _unpack_heads_t(_dot(vct, p.astype(BF16)), g, tq)
        imp = p[:, 0:tq]
        for r in range(1, ATT_REP):
            imp = imp + p[:, r * tq:(r + 1) * tq]
        score = _split_dot_left(msel_ref[...], imp)[0:n_sel, :]
        score = jnp.where(ovr == 0.0, score, ovr)
        rank = jnp.zeros((n_sel, tq), F32)
        for j in range(n_sel):
            cand = score[j:j + 1, :]
            beats = (cand > score) | ((cand == score) & (row > j))
            rank = rank + jnp.where(beats, 1.0, 0.0)
        pen_ref[0, g, 0:n_sel, :] = jnp.where(rank < k_top, 0.0, NEG).astype(BF16)
        pen_ref[0, g, n_sel:LANES, :] = jnp.zeros((LANES - n_sel, tq), BF16)


def _cmp_select_t(qt, kc, vc, tab, msel_t, ovr_t, *, tq, n_sel, k_top):
    b, nb, _ = kc.shape
    n_qt = tab.shape[1]
    t = n_qt * tq
    return pl.pallas_call(
        functools.partial(_cmp_select_t_kernel, tq=tq, n_sel=n_sel, k_top=k_top),
        grid=(n_qt, b),
        in_specs=[pl.BlockSpec((512, tq), lambda qi, bi: (0, bi * n_qt + qi)),
                  pl.BlockSpec((1, nb, 128), lambda qi, bi: (bi, 0, 0)),
                  pl.BlockSpec((1, nb, 128), lambda qi, bi: (bi, 0, 0)),
                  pl.BlockSpec((ATT_KV_HEADS, 1, nb, ATT_REP * tq), lambda qi, bi: (0, qi, 0, 0)),
                  pl.BlockSpec(msel_t.shape, lambda qi, bi: (0, 0)),
                  pl.BlockSpec((1, 128, tq), lambda qi, bi: (qi, 0, 0))],
        out_specs=[pl.BlockSpec((1, tq, 512), lambda qi, bi: (bi, qi, 0)),
                   pl.BlockSpec((1, ATT_KV_HEADS, 128, tq), lambda qi, bi: (bi, 0, 0, qi))],
        out_shape=[jax.ShapeDtypeStruct((b, t, 512), F32), jax.ShapeDtypeStruct((b, ATT_KV_HEADS, 128, t), BF16)],
        compiler_params=_params(("parallel", "parallel")),
        name="cmp_select_t",
    )(qt, kc, vc, tab, msel_t, ovr_t)


def _flash_init(m_ref, l_ref, acc_ref):
    m_ref[...] = jnp.full(m_ref.shape, NEG, F32)
    l_ref[...] = jnp.zeros(l_ref.shape, F32)
    acc_ref[...] = jnp.zeros(acc_ref.shape, F32)


def _flash_tile_t(k, qa_ref, q_rows, v_t, bias, m_ref, l_ref, acc_ref):
    cols = slice(0, qa_ref.shape[1])
    s = _dot(k, qa_ref[0:q_rows, cols])
    if bias is not None:
        s = s + bias(cols)
    m_old = m_ref[...]
    m_new = jnp.maximum(m_old, jnp.max(s, axis=0, keepdims=True))
    alpha = jnp.exp(m_old - m_new)
    p = jnp.exp(s - m_new)
    l_ref[...] = alpha * l_ref[...] + jnp.sum(p, axis=0, keepdims=True)
    acc_ref[...] = alpha * acc_ref[...] + _dot(v_t, p.astype(BF16))
    m_ref[...] = m_new


def _group_queries_t(qt, g, tq):
    row = lax.broadcasted_iota(jnp.int32, (LANES, tq), 0)
    mine = (row >= g * HEAD_DIM) & (row < (g + 1) * HEAD_DIM)
    zero = jnp.zeros((LANES, tq), qt.dtype)
    return jnp.concatenate([jnp.where(mine, qt[r * 128:(r + 1) * 128, :], zero) for r in range(ATT_REP)], axis=1)


def _unpack_heads_t(o_t, g, tq):
    rows = jnp.concatenate([o_t[:, r * tq:(r + 1) * tq].T for r in range(ATT_REP)], axis=0)
    return _pack_heads(rows, g, tq)


def _gate_chunks(gd, gexp):
    sig = 1.0 / (1.0 + jnp.exp(-gd))
    return _split_dot(sig, gexp, 3)


def _combine(gx, o_cmp, o_sel, o_win):
    return gx[:, 0:256] * o_cmp + gx[:, 256:512] * o_sel + gx[:, 512:768] * o_win


def _prompt_attn_kernel(qt_ref, pen_ref, oc_ref, gd_ref, ksa_ref, vst_ref, kw_ref, vwt_ref, a_ref, gexp_ref,
                        o_ref, m_ref, l_ref, acc_ref, qa_s, *, tq):
    qi = pl.program_id(1)
    groups = range(ATT_KV_HEADS)
    for g in groups:
        qa_s[g, 0:128, :] = _group_queries_t(qt_ref[...], g, tq)
        qa_s[g, 128:256, :] = jnp.concatenate([pen_ref[0, g]] * ATT_REP, axis=1)

    def sel_tile(start, n_keys, bias):
        k = ksa_ref[0, pl.ds(start, n_keys), :]
        v = vst_ref[:, pl.ds(start, n_keys)]
        for g in groups:
            _flash_tile_t(k, qa_s.at[g], 256, v, None if bias is None else functools.partial(bias, g),
                          m_ref.at[g], l_ref.at[g], acc_ref.at[g])

    def win_tile(start, n_keys, bias):
        k = kw_ref[0, pl.ds(start, n_keys), :]
        v = vwt_ref[:, pl.ds(start, n_keys)]
        for g in groups:
            _flash_tile_t(k, qa_s.at[g], 128, v, functools.partial(bias, g), m_ref.at[g], l_ref.at[g], acc_ref.at[g])

    def near(tile_fn, n_tiles):
        def body(i, carry):
            tile_fn(pl.multiple_of((qi - i) * tq, tq), tq, lambda g, cols: a_ref[g, i, :, cols])
            return carry
        lax.fori_loop(0, jnp.minimum(qi + 1, n_tiles), body, 0)

    _flash_init(m_ref, l_ref, acc_ref)
    near(sel_tile, 2)
    n_far = jnp.maximum(qi - 1, 0)

    def far_pair(i, carry):
        sel_tile(pl.multiple_of(2 * i * tq, tq), 2 * tq, None)
        return carry
    lax.fori_loop(0, n_far // 2, far_pair, 0)

    def far_last(i, carry):
        sel_tile(pl.multiple_of((n_far - 1) * tq, tq), tq, None)
        return carry
    lax.fori_loop(0, n_far % 2, far_last, 0)
    o_sel = [_unpack_heads_t(acc_ref[g] / l_ref[g], g, tq) for g in groups]

    _flash_init(m_ref, l_ref, acc_ref)
    near(win_tile, 3)
    for g in groups:
        o_win = _unpack_heads_t(acc_ref[g] / l_ref[g], g, tq)
        gx = _gate_chunks(gd_ref[0], gexp_ref[g])
        o_ref[0, :, g * 256:(g + 1) * 256] = _combine(gx, oc_ref[0, :, g * 256:(g + 1) * 256], o_sel[g], o_win).astype(BF16)


def _prompt_attn(qt, pen_t, o_cmp, gd, ksa, vst, kwb, vwt, atab, gexp, *, tq):
    b, t, _ = ksa.shape
    n_qt = t // tq
    nq = ATT_REP * tq
    qblk = lambda w: pl.BlockSpec((1, tq, w), lambda bi, qi: (bi, qi, 0))
    seq = lambda w: pl.BlockSpec((1, t, w), lambda bi, qi: (bi, 0, 0))
    seq_t = pl.BlockSpec((128, t), lambda bi, qi: (0, bi))
    full = lambda a: pl.BlockSpec(a.shape, lambda bi, qi: (0,) * a.ndim)
    return pl.pallas_call(
        functools.partial(_prompt_attn_kernel, tq=tq),
        grid=(b, n_qt),
        in_specs=[pl.BlockSpec((512, tq), lambda bi, qi: (0, bi * n_qt + qi)),
                  pl.BlockSpec((1, ATT_KV_HEADS, 128, tq), lambda bi, qi: (bi, 0, 0, qi)),
                  qblk(512), qblk(128), seq(256), seq_t, seq(128), seq_t, full(atab), full(gexp)],
        out_specs=qblk(512),
        out_shape=jax.ShapeDtypeStruct((b, t, 512), BF16),
        scratch_shapes=[pltpu.VMEM((ATT_KV_HEADS, 1, nq), F32), pltpu.VMEM((ATT_KV_HEADS, 1, nq), F32),
                        pltpu.VMEM((ATT_KV_HEADS, 128, nq), F32), pltpu.VMEM((ATT_KV_HEADS, 256, nq), BF16)],
        compiler_params=_params(("parallel", "arbitrary")),
        name="prompt_attn",
    )(qt, pen_t, o_cmp, gd, ksa, vst, kwb, vwt, atab, gexp)


def _softmax_two(s_a, s_b, vt_a, v_b):
    m = jnp.maximum(jnp.max(s_a, axis=-1, keepdims=True), jnp.max(s_b, axis=-1, keepdims=True))
    p_a = jnp.exp(s_a - m)
    p_b = jnp.exp(s_b - m)
    l = jnp.sum(p_a, axis=-1, keepdims=True) + jnp.sum(p_b, axis=-1, keepdims=True)
    return (_dot_nt(p_a.astype(BF16), vt_a) + _dot(p_b.astype(BF16), v_b)) / l


def _slide_window(buf_t, new_t, n_new):
    w = buf_t.shape[1]
    rolled = pltpu.roll(buf_t, w - n_new, 1)
    lane = lax.broadcasted_iota(jnp.int32, (LANES, LANES), 1)
    last = jnp.where(lane >= LANES - n_new, new_t, rolled[:, w - LANES:w])
    return jnp.concatenate([rolled[:, 0:w - LANES], last], axis=1)


def _sample_attn_kernel(pt_ref, q_ref, pen_ref, oc_ref, gd_ref, ck_ref, cv_ref, ksn_ref, vsn_ref,
                        cw_ref, cvw_ref, kwn_ref, vwn_ref, kwt_ref, vwt_ref, eall_ref, tsel_ref, tnew_ref,
                        twin_ref, gexp_ref, o_ref, kwo_ref, vwo_ref, kbuf, vbuf, ke_s, ksem, vsem,
                        *, n_pages, n_new):
    slot = _gather_pipeline(pt_ref, (ck_ref, cv_ref), (kbuf, vbuf), (ksem, vsem), n_pages, lane_major=True)
    tq = TQS
    rows = ATT_REP * tq

    @pl.when(pl.program_id(0) == 0)
    def _():
        ke_s[128:256, :] = eall_ref[...]

    ke_s[0:128, :] = kbuf[slot].astype(BF16)
    q = q_ref[0]
    pen = pen_ref[0]
    qa = jnp.concatenate(
        [jnp.concatenate([_group_queries(q, g, tq), jnp.concatenate([pen[:, g * 128:(g + 1) * 128]] * ATT_REP, axis=0)],
                         axis=1) for g in range(ATT_KV_HEADS)], axis=0)
    qg = qa[:, 0:128]
    stack = lambda t_ref: t_ref[:, 0].reshape(ATT_KV_HEADS * rows, t_ref.shape[-1])
    tnew = stack(tnew_ref)
    s_past = _dot(qa, ke_s[...]) + stack(tsel_ref)
    s_new = _dot_nt(qg, ksn_ref[0]) + tnew
    o_sel = _softmax_two(s_past, s_new, vbuf[slot].astype(BF16), vsn_ref[0])
    s_buf = _dot(qg, cw_ref[0].astype(BF16)) + stack(twin_ref)
    s_new = _dot_nt(qg, kwn_ref[0]) + tnew
    o_win = _softmax_two(s_buf, s_new, cvw_ref[0].astype(BF16), vwn_ref[0])
    for g in range(ATT_KV_HEADS):
        gx = _gate_chunks(gd_ref[0], gexp_ref[g])
        part = slice(g * rows, (g + 1) * rows)
        o = _combine(gx, oc_ref[0, :, g * 256:(g + 1) * 256], _pack_heads(o_sel[part], g, tq), _pack_heads(o_win[part], g, tq))
        o_ref[0, :, g * 256:(g + 1) * 256] = o.astype(BF16)
    kwo_ref[0] = _slide_window(cw_ref[0], kwt_ref[0], n_new)
    vwo_ref[0] = _slide_window(cvw_ref[0], vwt_ref[0], n_new)


def _sample_attn(page_table, q, pen, o_cmp, gd, ck, cv, ksn, vsn, cw, cvw, kwn, vwn, kwt, vwt,
                 eall, tsel, tnew, twin, gexp, *, n_new):
    bs, n_pages = page_table.shape
    past = n_pages * PAGE_SIZE
    per_b = lambda a: pl.BlockSpec((1,) + a.shape[1:], lambda i, pt: (i,) + (0,) * (a.ndim - 1))
    full = lambda a: pl.BlockSpec(a.shape, lambda i, pt: (0,) * a.ndim)
    anyspec = pl.BlockSpec(memory_space=pl.ANY)
    ins = [q, pen, o_cmp, gd, ck, cv, ksn, vsn, cw, cvw, kwn, vwn, kwt, vwt, eall, tsel, tnew, twin, gexp]
    specs = [per_b(q), per_b(pen), per_b(o_cmp), per_b(gd), anyspec, anyspec, per_b(ksn), per_b(vsn),
             per_b(cw), per_b(cvw), per_b(kwn), per_b(vwn), per_b(kwt), per_b(vwt),
             full(eall), full(tsel), full(tnew), full(twin), full(gexp)]
    win = pl.BlockSpec((1, 128, WINDOW), lambda i, pt: (i, 0, 0))
    return pl.pallas_call(
        functools.partial(_sample_attn_kernel, n_pages=n_pages, n_new=n_new),
        grid_spec=pltpu.PrefetchScalarGridSpec(
            num_scalar_prefetch=1,
            grid=(bs,),
            in_specs=specs,
            out_specs=[pl.BlockSpec((1, TQS, 512), lambda i, pt: (i, 0, 0)), win, win],
            scratch_shapes=[pltpu.VMEM((2, 128, past), F32), pltpu.VMEM((2, 128, past), F32),
                            pltpu.VMEM((256, past), BF16),
                            pltpu.SemaphoreType.DMA((2,)), pltpu.SemaphoreType.DMA((2,))]),
        out_shape=[jax.ShapeDtypeStruct((bs, TQS, 512), BF16),
                   jax.ShapeDtypeStruct((bs, 128, WINDOW), F32), jax.ShapeDtypeStruct((bs, 128, WINDOW), F32)],
        compiler_params=_params(("arbitrary",)),
        name="sample_attn",
    )(page_table, *ins)


def _lane_pair(cols, h0, h1, rows):
    lane = lax.broadcasted_iota(jnp.int32, (rows, LANES), 1)
    a = jnp.broadcast_to(cols[:, h0:h0 + 1], (rows, LANES))
    b = jnp.broadcast_to(cols[:, h1:h1 + 1], (rows, LANES))
    return jnp.where(lane < HEAD_DIM, a, b)


def _ssd_kernel(xbc_ref, z_ref, gd_ref, carry_ref, h0_ref, cw_ref, cb_ref, dtb_ref, arow_ref, dskip_ref,
                gnorm_ref, ltri_ref, y_ref, hout_ref, h_s, xfull, *, rows, n_valid):
    c = pl.program_id(1)
    halo = 8

    @pl.when(c == 0)
    def _():
        h_s[...] = h0_ref[0]
        xfull[0:halo, :] = carry_ref[0]

    xfull[halo:halo + rows, :] = xbc_ref[0]
    conv = cb_ref[...]
    for k in range(SSD_CONV):
        conv = conv + xfull[pl.ds(halo - (SSD_CONV - 1) + k, rows), :] * cw_ref[k:k + 1, :]
    tail = xfull[rows:rows + halo, :]
    xfull[0:halo, :] = tail
    xc = _silu(conv)
    xs = xc[:, 0:D_SSD]
    bm = xc[:, D_SSD:D_SSD + 128]
    cm = xc[:, D_SSD + 128:D_SSD + 256]

    lane = lax.broadcasted_iota(jnp.int32, (rows, LANES), 1)
    rowi = lax.broadcasted_iota(jnp.int32, (rows, LANES), 0)
    t = gd_ref[0] + dtb_ref[...]
    sp = jnp.maximum(t, 0.0) + jnp.log(1.0 + jnp.exp(-jnp.abs(t)))
    dt = jnp.where((lane >= DT_LANE) & (lane < DT_LANE + SSD_HEADS) & (rowi < n_valid), sp, 0.0)
    a = dt * arow_ref[...]
    acum = _split_dot_left(ltri_ref[...], a)
    acum_t = acum.T
    a_last = acum[rows - 1:rows, :]
    to_end = jnp.exp(a_last - acum)
    eac = jnp.exp(acum)
    dec = jnp.exp(a_last)

    li = lax.broadcasted_iota(jnp.int32, (rows, rows), 0)
    si = lax.broadcasted_iota(jnp.int32, (rows, rows), 1)
    causal = li >= si
    bmb = bm.astype(BF16)
    rowp = lax.broadcasted_iota(jnp.int32, (LANES, LANES), 0)
    cbs = []
    cmask = []
    for g in range(2):
        cg = jnp.where((lane >= g * 64) & (lane < (g + 1) * 64), cm, 0.0).astype(BF16)
        cmask.append(cg)
        cbs.append(_dot_nt(cg, bmb))
    for k in range(SSD_HEADS // 2):
        g = k // 2
        h0, h1 = DT_LANE + 2 * k, DT_LANE + 2 * k + 1
        xs_p = xs[:, k * 128:(k + 1) * 128]
        xdt = xs_p * _lane_pair(dt, h0, h1, rows)
        xdt_b = xdt.astype(BF16)
        ys = []
        for h in (h0, h1):
            seg = jnp.broadcast_to(acum[:, h:h + 1], (rows, rows)) - acum_t[h:h + 1, :]
            decay = jnp.where(causal, jnp.exp(jnp.where(causal, seg, 0.0)), 0.0)
            ys.append(_dot((cbs[g] * decay).astype(BF16), xdt_b))
        y = jnp.where(lane < HEAD_DIM, ys[0], ys[1])
        hp = h_s[k * 128:(k + 1) * 128, :]
        y = y + _dot_nt(cmask[g], hp.astype(BF16)) * _lane_pair(eac, h0, h1, rows)
        y = y + dskip_ref[:, k * 128:(k + 1) * 128] * xs_p
        xw = xdt * _lane_pair(to_end, h0, h1, rows)
        st = _dot(xw.T.astype(BF16), bmb)
        dfac = jnp.where(rowp < HEAD_DIM, dec[:, h0:h0 + 1], dec[:, h1:h1 + 1])
        h_s[k * 128:(k + 1) * 128, :] = hp * dfac + st
        xfull_y = y * _silu(z_ref[0, :, k * 128:(k + 1) * 128])
        y_ref[0, :, k * 128:(k + 1) * 128] = xfull_y.astype(y_ref.dtype)

    for g in range(2):
        yg = y_ref[0, :, g * 256:(g + 1) * 256].astype(F32)
        ms = jnp.mean(yg * yg, axis=-1, keepdims=True)
        y_ref[0, :, g * 256:(g + 1) * 256] = (yg * lax.rsqrt(ms + RMS_EPS)
                                               * gnorm_ref[:, g * 256:(g + 1) * 256]).astype(y_ref.dtype)

    @pl.when(c == pl.num_programs(1) - 1)
    def _():
        half = D_SSD // 2
        hout_ref[0, 0:half, :] = h_s[0:half, 0:SSD_STATE]
        hout_ref[0, half:D_SSD, :] = h_s[half:D_SSD, SSD_STATE:2 * SSD_STATE]


def _split_dot_left(tri, a):
    acc = None
    rem = a
    for _ in range(3):
        piece = rem.astype(BF16)
        rem = rem - piece.astype(F32)
        t = _dot(tri, piece)
        acc = t if acc is None else acc + t
    return acc


def _ssd(xbc, z, gd, carry, h0, wts, *, n_valid):
    b, l, _ = xbc.shape
    rows = SSD_CHUNK
    nc = l // rows
    blk = lambda w: pl.BlockSpec((1, rows, w), lambda bi, ci: (bi, ci, 0))
    per_b = lambda a: pl.BlockSpec((1,) + a.shape[1:], lambda bi, ci: (bi,) + (0,) * (a.ndim - 1))
    full = lambda a: pl.BlockSpec(a.shape, lambda bi, ci: (0,) * a.ndim)
    consts = [wts["conv_w"], wts["conv_b"], wts["dtb"], wts["arow"], wts["dskip"], wts["gnorm"], wts["ltri"]]
    return pl.pallas_call(
        functools.partial(_ssd_kernel, rows=rows, n_valid=n_valid),
        grid=(b, nc),
        in_specs=[blk(768), blk(512), blk(128), per_b(carry), per_b(h0)] + [full(a) for a in consts],
        out_specs=[blk(512), pl.BlockSpec((1, D_SSD, SSD_STATE), lambda bi, ci: (bi, 0, 0))],
        out_shape=[jax.ShapeDtypeStruct((b, l, 512), F32), jax.ShapeDtypeStruct((b, D_SSD, SSD_STATE), F32)],
        scratch_shapes=[pltpu.VMEM((512, 128), F32), pltpu.VMEM((rows + 8, 768), F32)],
        compiler_params=_params(("parallel", "arbitrary")),
        name="ssd",
    )(xbc, z, gd, carry, h0, *consts)


def _finish_kernel(x_ref, oa_ref, ys_ref, woa_ref, wos_ref, gf_ref, wg_ref, wu_ref, wd_ref,
                   y_ref, h_s, u_s, acc_s):
    f = pl.program_id(1)

    @pl.when(f == 0)
    def _():
        h = x_ref[...] + _dot(oa_ref[...], woa_ref[...]) + _dot(ys_ref[...].astype(BF16), wos_ref[...])
        h_s[...] = h
        ms = jnp.mean(h * h, axis=-1, keepdims=True)
        u_s[...] = (h * lax.rsqrt(ms + RMS_EPS) * gf_ref[...]).astype(BF16)
        acc_s[...] = jnp.zeros(acc_s.shape, F32)

    u = u_s[...]
    act = _silu(_dot(u, wg_ref[...])) * _dot(u, wu_ref[...])
    acc_s[...] += _dot(act.astype(BF16), wd_ref[...])

    @pl.when(f == pl.num_programs(1) - 1)
    def _():
        y_ref[...] = h_s[...] + acc_s[...]


def _finish(x2d, o_att, y_ssd, wts):
    n = x2d.shape[0]
    tm = min(512, n)
    nf = 2
    tf = D_FF // nf
    row = lambda w: pl.BlockSpec((tm, w), lambda i, f: (i, 0))
    full = lambda a: pl.BlockSpec(a.shape, lambda i, f: (0,) * a.ndim)
    return pl.pallas_call(
        _finish_kernel,
        grid=(n // tm, nf),
        in_specs=[row(D_MODEL), row(512), row(512), full(wts["wo_att"]), full(wts["wo_ssd"]), full(wts["gffn"]),
                  pl.BlockSpec((D_MODEL, tf), lambda i, f: (0, f)),
                  pl.BlockSpec((D_MODEL, tf), lambda i, f: (0, f)),
                  pl.BlockSpec((tf, D_MODEL), lambda i, f: (f, 0))],
        out_specs=row(D_MODEL),
        out_shape=jax.ShapeDtypeStruct((n, D_MODEL), F32),
        scratch_shapes=[pltpu.VMEM((tm, D_MODEL), F32), pltpu.VMEM((tm, D_MODEL), BF16),
                        pltpu.VMEM((tm, D_MODEL), F32)],
        compiler_params=_params(("parallel", "arbitrary")),
        name="finish",
    )(x2d, o_att, y_ssd, wts["wo_att"], wts["wo_ssd"], wts["gffn"], wts["w_gate"], wts["w_up"], wts["w_down"])


def _pair_perm():
    cols = []
    for r in range(ATT_REP):
        cols += list(range(r * HEAD_DIM, (r + 1) * HEAD_DIM))
        cols += list(range((ATT_REP + r) * HEAD_DIM, (ATT_REP + r + 1) * HEAD_DIM))
    return np.asarray(cols, np.int32)


def _block_ones(n, blk):
    i = np.arange(n)
    return (i[:, None] // blk == i[None, :] // blk).astype(np.float32) / blk


def _prep_weights(norm_mix, w_in, q_norm, k_norm, cmp_pe, cmp_w1, cmp_w2, conv_w, conv_b, dt_bias, a_log,
                  d_skip, ssd_norm, w_out, norm_ffn, w_gate, w_up, w_down):
    perm = _pair_perm()
    w = w_in
    gd = jnp.concatenate([w[:, OFF_GATE:OFF_Z], w[:, OFF_DT:D_IN],
                          jnp.zeros((D_MODEL, 128 - 3 * ATT_HEADS - SSD_HEADS), w.dtype)], axis=1)
    w_r = jnp.concatenate([w[:, :D_ATT][:, perm], w[:, OFF_KV:OFF_GATE], w[:, OFF_Z:OFF_XBC],
                           w[:, OFF_XBC:OFF_DT], gd], axis=1).astype(BF16)
    wts = dict(
        gmix=norm_mix.reshape(1, D_MODEL), w_in=w_r,
        qg=(jnp.tile(q_norm, ATT_HEADS) * (HEAD_DIM ** -0.5)).reshape(1, D_ATT),
        kg=jnp.tile(k_norm, ATT_KV_HEADS).reshape(1, D_KV),
        bq=jnp.asarray(_block_ones(D_ATT, HEAD_DIM), BF16), bk=jnp.asarray(_block_ones(D_KV, HEAD_DIM), BF16))

    def w1_big(w1):
        w1r = w1.reshape(2, CMP_STRIDE, HEAD_DIM, CMP_HIDDEN)
        eye = jnp.eye(ATT_KV_HEADS, dtype=w1.dtype)
        big = jnp.einsum("jsdh,gk->sgdjkh", w1r, eye)
        return big.reshape(CMP_STRIDE * D_KV, 2 * D_KV).astype(BF16)

    def pe_rows(pe):
        per = jnp.broadcast_to(pe.reshape(2, CMP_STRIDE, 1, HEAD_DIM), (2, CMP_STRIDE, ATT_KV_HEADS, HEAD_DIM))
        per = per.reshape(2, CMP_STRIDE * D_KV)
        return jnp.concatenate([per, jnp.zeros((6, CMP_STRIDE * D_KV), pe.dtype)], axis=0).astype(BF16)

    def w2_big(w2):
        eye = jnp.eye(ATT_KV_HEADS, dtype=w2.dtype)
        return jnp.einsum("hd,gk->ghkd", w2, eye).reshape(D_KV, D_KV).astype(BF16)

    wts.update(w1k=w1_big(cmp_w1[0]), w1v=w1_big(cmp_w1[1]), pek=pe_rows(cmp_pe[0]), pev=pe_rows(cmp_pe[1]),
               w2k=w2_big(cmp_w2[0]), w2v=w2_big(cmp_w2[1]))

    pad_lanes = lambda v: jnp.zeros((1, LANES), F32).at[0, DT_LANE:DT_LANE + SSD_HEADS].set(v)
    ltri = np.tril(np.ones((SSD_CHUNK, SSD_CHUNK), np.float32))
    wts.update(conv_w=jnp.concatenate([conv_w, jnp.zeros((4, D_CONV), F32)], axis=0), conv_b=conv_b.reshape(1, D_CONV),
               dtb=pad_lanes(dt_bias), arow=pad_lanes(-jnp.exp(a_log)),
               dskip=jnp.repeat(d_skip, 64).reshape(1, D_SSD), gnorm=ssd_norm.reshape(1, D_SSD),
               ltri=jnp.asarray(ltri, BF16))
    wts.update(wo_att=w_out[:D_ATT].astype(BF16), wo_ssd=w_out[D_ATT:].astype(BF16),
               gffn=norm_ffn.reshape(1, D_MODEL), w_gate=w_gate.astype(BF16), w_up=w_up.astype(BF16),
               w_down=w_down.astype(BF16))
    return wts


def _gate_expand():
    m = np.zeros((ATT_KV_HEADS, LANES, 3 * ATT_REP * HEAD_DIM), np.float32)
    for g in range(ATT_KV_HEADS):
        for r in range(ATT_REP):
            for br in range(3):
                c0 = br * ATT_REP * HEAD_DIM + r * HEAD_DIM
                m[g, g * 3 * ATT_REP + r * 3 + br, c0:c0 + HEAD_DIM] = 1.0
    return jnp.asarray(m, BF16)


def _sel_matrix(n_blk_pad, n_sel):
    m = np.zeros((n_blk_pad, LANES), np.float32)
    for n in range(n_blk_pad - 1):
        for j in {n // 4, (n + 1) // 4}:
            if j < n_sel:
                m[n, j] = 1.0
    return jnp.asarray(m, BF16)


def _override(q_pos, n_sel):
    j = np.arange(LANES)[None, :]
    cur = (q_pos // SEL_BLOCK)[:, None]
    forced = (j == 0) | (j == cur) | (j == cur - 1)
    ovr = np.where(forced, 1e30, np.where(j <= cur, 0.0, -1e30))
    ovr = np.where(j < n_sel, ovr, -1e30)
    return ovr.astype(np.float32)


def _prompt_tables(rel_bias, t):
    n_qt = t // TQ
    pos = np.arange(t)
    nb = t // CMP_STRIDE
    e = CMP_STRIDE * np.arange(nb) + (CMP_BLOCK - 1)
    dist = pos[None, :] - e[:, None]
    cmp_idx = _idx_table(dist, (dist >= 0) & (np.arange(nb)[:, None] < nb - 1))
    cmp_idx = cmp_idx.reshape(nb, n_qt, TQ).transpose(1, 0, 2)
    i = np.arange(TQ)[None, :]
    j = np.arange(TQ)[:, None]
    diag = _idx_table(i - j, i >= j)
    prev = _idx_table(TQ + i - j, np.ones((TQ, TQ), bool))
    prev2 = _idx_table(2 * TQ + i - j, (2 * TQ + i - j) < WINDOW)
    att_idx = np.stack([diag, prev, prev2])
    return _bias_tables(rel_bias, cmp_idx, stack_cols=True), _bias_tables(rel_bias, att_idx, stack_cols=True)


def _prompt_layer(x, wts, rel_bias):
    b, t, _ = x.shape
    n = b * t
    p = _project(x.reshape(n, D_MODEL), wts, t, "prompt")
    n_sub = t // CMP_STRIDE
    sub = lambda a: a.reshape(b, n_sub, CMP_STRIDE * D_KV)
    kcmp, vcmp = _compress_prompt(sub(p["kc_b"]), sub(p["vc_b"]), wts)
    cmp_tab, att_tab = _prompt_tables(rel_bias, t)
    n_sel = t // SEL_BLOCK
    ovr_t = jnp.asarray(_override(np.arange(t), n_sel).reshape(t // TQ, TQ, LANES).transpose(0, 2, 1))
    r3 = lambda a: a.reshape(b, t, a.shape[-1])
    o_cmp, pen_t = _cmp_select_t(p["q_t"], kcmp, vcmp, cmp_tab, _sel_matrix(n_sub, n_sel).T, ovr_t,
                                 tq=TQ, n_sel=n_sel, k_top=min(SEL_TOPN, n_sel))
    o_att = _prompt_attn(p["q_t"], pen_t, o_cmp, r3(p["gd"]), r3(p["ksa"]), p["vs_tb"], r3(p["kw_b"]), p["vw_tb"],
                         att_tab, _gate_expand(), tq=TQ)
    carry = jnp.zeros((b, 8, D_CONV), F32)
    h0 = jnp.zeros((b, 512, 128), F32)
    y_ssd, h_fin = _ssd(r3(p["xbc"]), r3(p["z"]), r3(p["gd"]), carry, h0, wts, n_valid=SSD_CHUNK)
    y = _finish(x.reshape(n, D_MODEL), o_att.reshape(n, 512), y_ssd.reshape(n, 512), wts)
    wb = min(WINDOW, t)
    kv_out = {k: _token_major(p[k + "_t"]) for k in KV_NAMES}
    ssm = h_fin.reshape(b, SSD_HEADS, 64, SSD_STATE)
    return y.reshape(b, t, D_MODEL), (kv_out["kc"], kv_out["vc"], kv_out["ks"], kv_out["vs"],
                                      kv_out["kw"][:, t - wb:], kv_out["vw"][:, t - wb:],
                                      r3(p["xbc"])[:, t - (SSD_CONV - 1):], ssm)


def _token_major(a_t):
    b, _, t = a_t.shape
    return jnp.swapaxes(a_t, 1, 2).reshape(b, t, ATT_KV_HEADS, HEAD_DIM)


def _feature_major(a):
    return jnp.swapaxes(a.reshape(a.shape[:-2] + (D_KV,)), -1, -2)


def _sample_tables(rel_bias, past, s_new, n_blk_pad):
    s = np.minimum(np.arange(TQS), s_new - 1)[:, None]
    pos = past + s
    nidx = np.arange(n_blk_pad)[None, :]
    e = CMP_STRIDE * nidx + (CMP_BLOCK - 1)
    cmp_idx = _idx_table(pos - e, (e <= pos) & (nidx < n_blk_pad - 1))[None]
    key = np.arange(past)[None, :]
    sel_idx = _idx_table(pos - key, np.ones((TQS, past), bool))[None]
    jn = np.arange(LANES)[None, :]
    new_idx = _idx_table(s - jn, (jn <= s) & (jn < s_new))[None]
    wi = np.arange(WINDOW)[None, :]
    wdist = pos - (past - WINDOW + wi)
    win_idx = _idx_table(wdist, (wdist >= 0) & (wdist < WINDOW))[None]
    tabs = [_bias_tables(rel_bias, t) for t in (cmp_idx, sel_idx, new_idx, win_idx)]
    return tabs, pos[:, 0]


def _sample_layer(x, c_kc, c_vc, c_ks, c_vs, c_kw, c_vw, s_conv, s_ssm, page_table, wts, rel_bias):
    bs, s_new, _ = x.shape
    n = bs * s_new
    n_pages = page_table.shape[1]
    past = n_pages * PAGE_SIZE
    p = _project(x.reshape(n, D_MODEL), wts, s_new, "sample")
    r3 = lambda a: a.reshape(bs, s_new, a.shape[-1])
    padq = lambda a, rows: jnp.pad(r3(a), ((0, 0), (0, rows - s_new), (0, 0)))
    new_t = lambda a_t: jnp.pad(jnp.swapaxes(a_t.reshape(D_KV, bs, s_new), 0, 1), ((0, 0), (0, 0), (LANES - s_new, 0)))
    kcmp, vcmp = _compress_sample(page_table, _feature_major(c_kc), _feature_major(c_vc), wts)
    n_blk_pad = past // CMP_STRIDE
    (cmp_tab, sel_tab, new_tab, win_tab), pos = _sample_tables(rel_bias, past, s_new, n_blk_pad)
    n_sel = past // SEL_BLOCK
    ovr = jnp.asarray(_override(pos, n_sel)[None])
    qp = padq(p["q"], TQS)
    o_cmp, pen = _cmp_select(qp, kcmp, vcmp, cmp_tab, _sel_matrix(n_blk_pad, n_sel), ovr,
                             tq=TQS, n_sel=n_sel, k_top=min(SEL_TOPN - 1, n_sel))
    blk_of_key = np.arange(past)[None, :] // SEL_BLOCK
    eall = jnp.asarray((np.arange(LANES)[:, None] == blk_of_key).astype(np.float32), BF16)
    o_att, kw_new, vw_new = _sample_attn(
        page_table, qp, pen, o_cmp, padq(p["gd"], TQS),
        _feature_major(c_ks), _feature_major(c_vs), padq(p["ks_b"], LANES), padq(p["vs_b"], LANES),
        _feature_major(c_kw), _feature_major(c_vw), padq(p["kw_b"], LANES), padq(p["vw_b"], LANES),
        new_t(p["kw_t"]), new_t(p["vw_t"]),
        eall, sel_tab, new_tab, win_tab, _gate_expand(), n_new=s_new)
    carry = jnp.pad(s_conv, ((0, 0), (8 - (SSD_CONV - 1), 0), (0, 0)))
    h0 = s_ssm.reshape(bs, 512, SSD_STATE)
    h0 = jnp.concatenate([h0, h0], axis=-1)
    y_ssd, h_fin = _ssd(padq(p["xbc"], SSD_CHUNK), padq(p["z"], SSD_CHUNK), padq(p["gd"], SSD_CHUNK), carry, h0, wts,
                        n_valid=s_new)
    y = _finish(x.reshape(n, D_MODEL), o_att[:, :s_new].reshape(n, 512), y_ssd[:, :s_new].reshape(n, 512), wts)
    kv4 = lambda a: a.reshape(bs, -1, ATT_KV_HEADS, HEAD_DIM)
    ssm = h_fin.reshape(bs, SSD_HEADS, 64, SSD_STATE)
    conv_state = jnp.concatenate([s_conv, r3(p["xbc"])], axis=1)[:, s_new:]
    return y.reshape(bs, s_new, D_MODEL), (kv4(p["kc"]), kv4(p["vc"]), kv4(p["ks"]), kv4(p["vs"]),
                                           _token_major(kw_new), _token_major(vw_new), conv_state, ssm)


def kernel(x_prompt, x_sample, cache_k_cmp, cache_v_cmp, cache_k_sel, cache_v_sel, cache_k_win, cache_v_win,
           state_conv, state_ssm, page_table, norm_mix, w_in, q_norm, k_norm, cmp_pe, cmp_w1, cmp_w2, rel_bias,
           conv_w, conv_b, dt_bias, a_log, d_skip, ssd_norm, w_out, norm_ffn, w_gate, w_up, w_down):
    depth = w_in.shape[0]
    y_p, y_s = x_prompt, x_sample
    p_states, s_states = [], []
    for l in range(depth):
        wts = _prep_weights(norm_mix[l], w_in[l], q_norm[l], k_norm[l], cmp_pe[l], cmp_w1[l], cmp_w2[l],
                            conv_w[l], conv_b[l], dt_bias[l], a_log[l], d_skip[l], ssd_norm[l], w_out[l],
                            norm_ffn[l], w_gate[l], w_up[l], w_down[l])
        y_p, st_p = _prompt_layer(y_p, wts, rel_bias)
        y_s, st_s = _sample_layer(y_s, cache_k_cmp[l], cache_v_cmp[l], cache_k_sel[l], cache_v_sel[l],
                                  cache_k_win[l], cache_v_win[l], state_conv[l], state_ssm[l], page_table,
                                  wts, rel_bias)
        p_states.append(st_p)
        s_states.append(st_s)
    p_out = [jnp.stack(a) for a in zip(*p_states)]
    s_out = [jnp.stack(a) for a in zip(*s_states)]
    return (y_p, y_s, *p_out, *s_out)
```

```python
import functools
import math

import numpy as np
import jax
import jax.numpy as jnp
from jax import lax
from jax.experimental import pallas as pl
from jax.experimental.pallas import tpu as pltpu

F32 = jnp.float32
BF16 = jnp.bfloat16

D_MODEL = 1024
HEAD_DIM = 64
ATT_HEADS = 8
ATT_KV_HEADS = 2
ATT_REP = ATT_HEADS // ATT_KV_HEADS
CMP_BLOCK = 32
CMP_STRIDE = 16
CMP_HIDDEN = 64
SEL_BLOCK = 64
SEL_TOPN = 16
WINDOW = 512
N_BUCKETS = 32
MAX_DISTANCE = 128
PAGE_SIZE = 128
SSD_HEADS = 8
SSD_STATE = 64
SSD_CONV = 4
SSD_CHUNK = 128
D_ATT = ATT_HEADS * HEAD_DIM
D_SSD = SSD_HEADS * 64
D_KV = ATT_KV_HEADS * HEAD_DIM
D_CONV = D_SSD + 2 * 2 * SSD_STATE
D_FF = ((8 * D_MODEL // 3 + 255) // 256) * 256
OFF_KV = D_ATT
OFF_GATE = OFF_KV + 6 * D_KV
OFF_Z = OFF_GATE + 3 * ATT_HEADS
OFF_XBC = OFF_Z + D_SSD
OFF_DT = OFF_XBC + D_CONV
D_IN = OFF_DT + SSD_HEADS
RMS_EPS = 1e-6
NEG = -1e30

C_Q, C_KV, C_Z, C_XBC, C_GD = 0, 512, 1280, 1792, 2560
D_INR = 2688
DT_LANE = 24
LANES = 128
TQ = 256
TQS = 16
VMEM_LIMIT = 48 * 1024 * 1024


def _dot(a, b):
    return jnp.dot(a, b, preferred_element_type=F32)


def _dot_nt(a, b):
    return lax.dot_general(a, b, (((1,), (1,)), ((), ())), preferred_element_type=F32)


def _split_dot(a, b, parts):
    acc = None
    rem = a
    for _ in range(parts):
        piece = rem.astype(BF16)
        rem = rem - piece.astype(F32)
        t = _dot(piece, b)
        acc = t if acc is None else acc + t
    return acc


def _silu(x):
    return x * (1.0 / (1.0 + jnp.exp(-x)))


def _params(sem=None):
    kw = dict(vmem_limit_bytes=VMEM_LIMIT)
    if sem is not None:
        kw["dimension_semantics"] = sem
    return pltpu.CompilerParams(**kw)


KV_NAMES = ("kc", "vc", "ks", "vs", "kw", "vw")
PROJ_OUTPUTS = {
    "prompt": ([(k + "_t", "seq", 128, F32) for k in KV_NAMES]
               + [("kc_b", "row", 128, BF16), ("vc_b", "row", 128, BF16), ("ksa", "row", 256, BF16),
                  ("kw_b", "row", 128, BF16), ("q_t", "col", 512, BF16), ("vs_tb", "col", 128, BF16),
                  ("vw_tb", "col", 128, BF16)]),
    "sample": ([(k, "row", 128, F32) for k in KV_NAMES]
               + [("q", "row", 512, BF16), ("ks_b", "row", 128, BF16), ("vs_b", "row", 128, BF16),
                  ("kw_b", "row", 128, BF16), ("vw_b", "row", 128, BF16),
                  ("kw_t", "col", 128, F32), ("vw_t", "col", 128, F32)]),
}
PROJ_COMMON = [("z", "row", 512, F32), ("xbc", "row", 768, F32), ("gd", "row", 128, F32)]


def _proj_kernel(x_ref, gmix_ref, w_ref, qg_ref, kg_ref, bq_ref, bk_ref, *out_refs, names, tm, t_len):
    out = dict(zip(names, out_refs))
    x = x_ref[...]
    ms = jnp.mean(x * x, axis=-1, keepdims=True)
    u = (x * lax.rsqrt(ms + RMS_EPS) * gmix_ref[...]).astype(BF16)

    def proj(lo, hi):
        return _dot(u, w_ref[:, lo:hi])

    def headnorm(v, b_ref, g_ref):
        msq = _dot((v * v).astype(BF16), b_ref[...])
        return v * lax.rsqrt(msq + RMS_EPS) * g_ref[...]

    def put(name, value):
        if name in out:
            ref = out[name]
            ref[...] = value().astype(ref.dtype).reshape(ref.shape)

    q = headnorm(proj(C_Q, C_Q + 512), bq_ref, qg_ref)
    put("q", lambda: q)
    put("q_t", lambda: q.T)
    kv = {}
    for i, name in enumerate(KV_NAMES):
        v = proj(C_KV + 128 * i, C_KV + 128 * (i + 1))
        kv[name] = headnorm(v, bk_ref, kg_ref) if name in ("ks", "kw") else v
    for name, v in kv.items():
        put(name, lambda v=v: v)
        put(name + "_b", lambda v=v: v)
        put(name + "_t", lambda v=v: v.T)
        put(name + "_tb", lambda v=v: v.T)
    if "ksa" in out:
        row = pl.program_id(0) * tm + lax.broadcasted_iota(jnp.int32, (tm, LANES), 0)
        blk = (row % t_len) // SEL_BLOCK
        lane = lax.broadcasted_iota(jnp.int32, (tm, LANES), 1)
        out["ksa"][:, 0:128] = kv["ks"].astype(BF16)
        out["ksa"][:, 128:256] = jnp.where(lane == blk, 1.0, 0.0).astype(BF16)
    put("z", lambda: proj(C_Z, C_Z + 512))
    put("xbc", lambda: proj(C_XBC, C_XBC + 768))
    put("gd", lambda: proj(C_GD, C_GD + 128))


def _project(x2d, wts, t_len, mode):
    n = x2d.shape[0]
    tm = min(512, n)
    per_seq = max(t_len // tm, 1)
    full = lambda a: pl.BlockSpec(a.shape, lambda i: (0,) * a.ndim)
    ins = [x2d, wts["gmix"], wts["w_in"], wts["qg"], wts["kg"], wts["bq"], wts["bk"]]
    outs = PROJ_OUTPUTS[mode] + PROJ_COMMON
    specs, shapes = [], []
    for _, layout, w, dt in outs:
        if layout == "row":
            specs.append(pl.BlockSpec((tm, w), lambda i: (i, 0)))
            shapes.append(jax.ShapeDtypeStruct((n, w), dt))
        elif layout == "col":
            specs.append(pl.BlockSpec((w, tm), lambda i: (0, i)))
            shapes.append(jax.ShapeDtypeStruct((w, n), dt))
        else:
            specs.append(pl.BlockSpec((1, w, tm), lambda i: (i // per_seq, 0, i % per_seq)))
            shapes.append(jax.ShapeDtypeStruct((n // t_len, w, t_len), dt))
    names = tuple(name for name, _, _, _ in outs)
    res = pl.pallas_call(
        functools.partial(_proj_kernel, names=names, tm=tm, t_len=t_len),
        grid=(n // tm,),
        in_specs=[pl.BlockSpec((tm, D_MODEL), lambda i: (i, 0))] + [full(a) for a in ins[1:]],
        out_specs=specs,
        out_shape=shapes,
        compiler_params=_params(("parallel",)),
        name="proj",
    )(*ins)
    return dict(zip(names, res))


def _bucket_np(dist):
    n = np.maximum(dist, 0)
    max_exact = N_BUCKETS // 2
    nf = np.maximum(n, 1).astype(np.float64)
    large = max_exact + (np.log(nf / max_exact) / math.log(MAX_DISTANCE / max_exact)
                         * (N_BUCKETS - max_exact)).astype(np.int64)
    large = np.minimum(large, N_BUCKETS - 1)
    return np.where(n < max_exact, n, large).astype(np.int32)


def _idx_table(dist, valid):
    return np.where(valid, _bucket_np(dist), -1).astype(np.int32)


def _table_kernel(rb_ref, idx_ref, out_ref):
    h = pl.program_id(0) * ATT_REP + pl.program_id(2)
    idx = idx_ref[0]
    far = rb_ref[N_BUCKETS - 1, h]
    acc = jnp.zeros(idx.shape, F32)
    for b in range(N_BUCKETS - 1):
        acc = jnp.where(idx == b, rb_ref[b, h] - far, acc)
    out_ref[...] = jnp.where(idx < 0, NEG, acc).reshape(out_ref.shape)


def _bias_tables(rel_bias, idx, stack_cols=False):
    k, r, c = idx.shape
    if stack_cols:
        return pl.pallas_call(
            _table_kernel,
            grid=(ATT_KV_HEADS, k, ATT_REP),
            in_specs=[pl.BlockSpec(memory_space=pltpu.SMEM),
                      pl.BlockSpec((1, r, c), lambda g, kk, rr: (kk, 0, 0))],
            out_specs=pl.BlockSpec((1, 1, r, c), lambda g, kk, rr: (g, kk, 0, rr)),
            out_shape=jax.ShapeDtypeStruct((ATT_KV_HEADS, k, r, ATT_REP * c), F32),
            name="bias_table_t",
        )(rel_bias, jnp.asarray(idx))
    out = pl.pallas_call(
        _table_kernel,
        grid=(ATT_KV_HEADS, k, ATT_REP),
        in_specs=[pl.BlockSpec(memory_space=pltpu.SMEM),
                  pl.BlockSpec((1, r, c), lambda g, kk, rr: (kk, 0, 0))],
        out_specs=pl.BlockSpec((1, 1, 1, r, c), lambda g, kk, rr: (g, kk, rr, 0, 0)),
        out_shape=jax.ShapeDtypeStruct((ATT_KV_HEADS, k, ATT_REP, r, c), F32),
        name="bias_table",
    )(rel_bias, jnp.asarray(idx))
    return out.reshape(ATT_KV_HEADS, k, ATT_REP * r, c)


def _compress_core(x, w1, pe, w2):
    return _compress_tail(_dot(x.astype(BF16), w1), w1, pe, w2)


def _compress_tail(u, w1, pe, w2):
    n_sub = u.shape[0]
    upe = _dot(pe, w1)
    nxt = pltpu.roll(u[:, 128:256], n_sub - 1, 0)
    pre = u[:, 0:128] + nxt + upe[0:1, 0:128] + upe[1:2, 128:256]
    return _dot(_silu(pre).astype(BF16), w2)


def _knorm(v, bk, kg):
    msq = _dot((v * v).astype(BF16), bk)
    return v * lax.rsqrt(msq + RMS_EPS) * kg


def _compress_prompt_kernel(kc_ref, vc_ref, w1k_ref, w1v_ref, pek_ref, pev_ref, w2k_ref, w2v_ref,
                            kg_ref, bk_ref, ko_ref, vo_ref):
    kc = _compress_core(kc_ref[0], w1k_ref[...], pek_ref[...], w2k_ref[...])
    ko_ref[0] = _knorm(kc, bk_ref[...], kg_ref[...]).astype(BF16)
    vo_ref[0] = _compress_core(vc_ref[0], w1v_ref[...], pev_ref[...], w2v_ref[...]).astype(BF16)


def _compress_prompt(kc, vc, wts):
    b, n_sub, w = kc.shape
    full = lambda a: pl.BlockSpec(a.shape, lambda i: (0,) * a.ndim)
    consts = [wts["w1k"], wts["w1v"], wts["pek"], wts["pev"], wts["w2k"], wts["w2v"], wts["kg"], wts["bk"]]
    blk = pl.BlockSpec((1, n_sub, w), lambda i: (i, 0, 0))
    oblk = pl.BlockSpec((1, n_sub, 128), lambda i: (i, 0, 0))
    return pl.pallas_call(
        _compress_prompt_kernel,
        grid=(b,),
        in_specs=[blk, blk] + [full(a) for a in consts],
        out_specs=[oblk, oblk],
        out_shape=[jax.ShapeDtypeStruct((b, n_sub, 128), BF16)] * 2,
        compiler_params=_params(("parallel",)),
        name="compress_prompt",
    )(kc, vc, *consts)


def _page_copy(cache_ref, page, buf_ref, slot, p, sem_ref, lane_major):
    if lane_major:
        dst = buf_ref.at[slot, :, pl.ds(pl.multiple_of(p * PAGE_SIZE, PAGE_SIZE), PAGE_SIZE)]
    else:
        dst = buf_ref.at[slot, p]
    return pltpu.make_async_copy(cache_ref.at[page], dst, sem_ref.at[slot])


def _gather_start(pt_ref, b, slot, caches, bufs, sems, n_pages, lane_major):
    def body(p, carry):
        page = pt_ref[b, p]
        for cache_ref, buf_ref, sem_ref in zip(caches, bufs, sems):
            _page_copy(cache_ref, page, buf_ref, slot, p, sem_ref, lane_major).start()
        return carry
    lax.fori_loop(0, n_pages, body, 0)


def _gather_wait(slot, caches, bufs, sems, n_pages, lane_major):
    def body(p, carry):
        for cache_ref, buf_ref, sem_ref in zip(caches, bufs, sems):
            _page_copy(cache_ref, 0, buf_ref, slot, p, sem_ref, lane_major).wait()
        return carry
    lax.fori_loop(0, n_pages, body, 0)


def _gather_pipeline(pt_ref, caches, bufs, sems, n_pages, lane_major):
    b = pl.program_id(0)
    nb = pl.num_programs(0)
    slot = b % 2

    @pl.when(b == 0)
    def _():
        _gather_start(pt_ref, 0, 0, caches, bufs, sems, n_pages, lane_major)

    @pl.when(b + 1 < nb)
    def _():
        _gather_start(pt_ref, b + 1, 1 - slot, caches, bufs, sems, n_pages, lane_major)

    _gather_wait(slot, caches, bufs, sems, n_pages, lane_major)
    return slot


def _compress_paged(buf, slot, rows_s, perm, w1, pe, w2, n_pages):
    groups = PAGE_SIZE // CMP_STRIDE

    def body(i, carry):
        pair = buf[slot, pl.ds(2 * i, 2)].reshape(2 * D_KV, PAGE_SIZE)
        rows = _dot_nt(perm, pair.astype(BF16))
        for half in range(2):
            start = pl.multiple_of((2 * i + half) * groups, groups)
            for s in range(CMP_STRIDE):
                rows_s[s // 2, pl.ds(start, groups), (s % 2) * D_KV:(s % 2 + 1) * D_KV] = (
                    rows[s * groups:(s + 1) * groups, half * D_KV:(half + 1) * D_KV])
        return carry
    lax.fori_loop(0, n_pages // 2, body, 0, unroll=4)
    u = None
    for j in range(CMP_STRIDE // 2):
        t = _dot(rows_s[j].astype(BF16), w1[2 * j * D_KV:(2 * j + 2) * D_KV, :])
        u = t if u is None else u + t
    return _compress_tail(u, w1, pe, w2)


def _compress_sample_kernel(pt_ref, ck_ref, cv_ref, perm_ref, w1k_ref, w1v_ref, pek_ref, pev_ref, w2k_ref, w2v_ref,
                            kg_ref, bk_ref, ko_ref, vo_ref, kbuf, vbuf, rows_s, ksem, vsem, *, n_pages):
    slot = _gather_pipeline(pt_ref, (ck_ref, cv_ref), (kbuf, vbuf), (ksem, vsem), n_pages, lane_major=False)
    perm = perm_ref[...]
    kc = _compress_paged(kbuf, slot, rows_s, perm, w1k_ref[...], pek_ref[...], w2k_ref[...], n_pages)
    ko_ref[0] = _knorm(kc, bk_ref[...], kg_ref[...]).astype(BF16)
    vo_ref[0] = _compress_paged(vbuf, slot, rows_s, perm, w1v_ref[...], pev_ref[...], w2v_ref[...], n_pages).astype(BF16)


def _compress_sample(page_table, ck, cv, wts):
    bs, n_pages = page_table.shape
    n_sub = n_pages * PAGE_SIZE // CMP_STRIDE
    full = lambda a: pl.BlockSpec(a.shape, lambda i, pt: (0,) * a.ndim)
    groups = PAGE_SIZE // CMP_STRIDE
    r = np.arange(PAGE_SIZE)
    perm = jnp.asarray((np.arange(PAGE_SIZE)[None, :] == (CMP_STRIDE * (r % groups) + r // groups)[:, None])
                       .astype(np.float32), BF16)
    consts = [perm, wts["w1k"], wts["w1v"], wts["pek"], wts["pev"], wts["w2k"], wts["w2v"], wts["kg"], wts["bk"]]
    anyspec = pl.BlockSpec(memory_space=pl.ANY)
    oblk = pl.BlockSpec((1, n_sub, 128), lambda i, pt: (i, 0, 0))
    pages = pltpu.VMEM((2, n_pages, D_KV, PAGE_SIZE), F32)
    return pl.pallas_call(
        functools.partial(_compress_sample_kernel, n_pages=n_pages),
        grid_spec=pltpu.PrefetchScalarGridSpec(
            num_scalar_prefetch=1,
            grid=(bs,),
            in_specs=[anyspec, anyspec] + [full(a) for a in consts],
            out_specs=[oblk, oblk],
            scratch_shapes=[pages, pages, pltpu.VMEM((CMP_STRIDE // 2, n_sub, 2 * D_KV), F32),
                            pltpu.SemaphoreType.DMA((2,)), pltpu.SemaphoreType.DMA((2,))]),
        out_shape=[jax.ShapeDtypeStruct((bs, n_sub, 128), BF16)] * 2,
        compiler_params=_params(("arbitrary",)),
        name="compress_sample",
    )(page_table, ck, cv, *consts)


def _group_queries(q, g, tq):
    lane = lax.broadcasted_iota(jnp.int32, (tq, LANES), 1)
    mine = (lane >= g * HEAD_DIM) & (lane < (g + 1) * HEAD_DIM)
    zero = jnp.zeros((tq, LANES), q.dtype)
    return jnp.concatenate([jnp.where(mine, q[:, r * 128:(r + 1) * 128], zero) for r in range(ATT_REP)], axis=0)


def _pack_heads(o, g, tq):
    lane = lax.broadcasted_iota(jnp.int32, (tq, LANES), 1)
    first = g == 0
    chunks = []
    for k in range(ATT_REP // 2):
        a = o[2 * k * tq:(2 * k + 1) * tq]
        b = o[(2 * k + 1) * tq:(2 * k + 2) * tq]
        lo = jnp.where(first, a, pltpu.roll(a, HEAD_DIM, 1))
        hi = jnp.where(first, pltpu.roll(b, HEAD_DIM, 1), b)
        chunks.append(jnp.where(lane < HEAD_DIM, lo, hi))
    return jnp.concatenate(chunks, axis=1)


def _cmp_select_kernel(q_ref, kc_ref, vc_ref, tab_ref, msel_ref, ovr_ref, o_ref, pen_ref, *, tq, n_sel, k_top):
    q = q_ref[0]
    kc = kc_ref[0]
    vc = vc_ref[0]
    ovr = ovr_ref[0]
    lane = lax.broadcasted_iota(jnp.int32, (tq, LANES), 1)
    outs = []
    for g in range(ATT_KV_HEADS):
        tab = tab_ref[g, 0]
        s = _dot_nt(_group_queries(q, g, tq), kc) + tab
        m = jnp.max(s, axis=-1, keepdims=True)
        e = jnp.where(tab > 0.5 * NEG, jnp.exp(s - m), 0.0)
        p = e / jnp.maximum(jnp.sum(e, axis=-1, keepdims=True), 1e-30)
        outs.append(_dot(p.astype(BF16), vc))
        imp = p[0:tq]
        for r in range(1, ATT_REP):
            imp = imp + p[r * tq:(r + 1) * tq]
        score = _split_dot(imp, msel_ref[...], 3)
        score = jnp.where(ovr == 0.0, score, ovr)
        rank = jnp.zeros((tq, LANES), F32)
        for j in range(n_sel):
            col = score[:, j:j + 1]
            beats = (col > score) | ((col == score) & (lane > j))
            rank = rank + jnp.where(beats, 1.0, 0.0)
        pen_ref[0, :, g * 128:(g + 1) * 128] = jnp.where(rank < k_top, 0.0, NEG).astype(BF16)
    for g in range(ATT_KV_HEADS):
        o_ref[0, :, g * 256:(g + 1) * 256] = _pack_heads(outs[g], g, tq)


def _cmp_select(q, kc, vc, tab, msel, ovr, *, tq, n_sel, k_top):
    b, t, _ = q.shape
    nb = kc.shape[1]
    n_qt = t // tq
    return pl.pallas_call(
        functools.partial(_cmp_select_kernel, tq=tq, n_sel=n_sel, k_top=k_top),
        grid=(n_qt, b),
        in_specs=[pl.BlockSpec((1, tq, 512), lambda qi, bi: (bi, qi, 0)),
                  pl.BlockSpec((1, nb, 128), lambda qi, bi: (bi, 0, 0)),
                  pl.BlockSpec((1, nb, 128), lambda qi, bi: (bi, 0, 0)),
                  pl.BlockSpec((ATT_KV_HEADS, 1, ATT_REP * tq, nb), lambda qi, bi: (0, qi, 0, 0)),
                  pl.BlockSpec(msel.shape, lambda qi, bi: (0, 0)),
                  pl.BlockSpec((1, tq, 128), lambda qi, bi: (qi, 0, 0))],
        out_specs=[pl.BlockSpec((1, tq, 512), lambda qi, bi: (bi, qi, 0)),
                   pl.BlockSpec((1, tq, 256), lambda qi, bi: (bi, qi, 0))],
        out_shape=[jax.ShapeDtypeStruct((b, t, 512), F32), jax.ShapeDtypeStruct((b, t, 256), BF16)],
        compiler_params=_params(("parallel", "parallel")),
        name="cmp_select",
    )(q, kc, vc, tab, msel, ovr)


def _cmp_select_t_kernel(qt_ref, kc_ref, vc_ref, tab_ref, msel_ref, ovr_ref, o_ref, pen_ref,
                         *, tq, n_sel, k_top):
    qt = qt_ref[...]
    kc = kc_ref[0]
    vct = vc_ref[0].astype(F32).T.astype(BF16)
    ovr = ovr_ref[0, 0:n_sel, :]
    row = lax.broadcasted_iota(jnp.int32, (n_sel, tq), 0)
    for g in range(ATT_KV_HEADS):
        tab = tab_ref[g, 0]
        s = _dot(kc, _group_queries_t(qt, g, tq)) + tab
        m = jnp.max(s, axis=0, keepdims=True)
        e = jnp.where(tab > 0.5 * NEG, jnp.exp(s - m), 0.0)
        p = e / jnp.maximum(jnp.sum(e, axis=0, keepdims=True), 1e-30)
        o_ref[0, :, g * 256:(g + 1) * 256] = _unpack_heads_t(_dot(vct, p.astype(BF16)), g, tq)
        imp = p[:, 0:tq]
        for r in range(1, ATT_REP):
            imp = imp + p[:, r * tq:(r + 1) * tq]
        score = _split_dot_left(msel_ref[...], imp)[0:n_sel, :]
        score = jnp.where(ovr == 0.0, score, ovr)
        rank = jnp.zeros((n_sel, tq), F32)
        for j in range(n_sel):
            cand = score[j:j + 1, :]
            beats = (cand > score) | ((cand == score) & (row > j))
            rank = rank + jnp.where(beats, 1.0, 0.0)
        pen_ref[0, g, 0:n_sel, :] = jnp.where(rank < k_top, 0.0, NEG).astype(BF16)
        pen_ref[0, g, n_sel:LANES, :] = jnp.zeros((LANES - n_sel, tq), BF16)


def _cmp_select_t(qt, kc, vc, tab, msel_t, ovr_t, *, tq, n_sel, k_top):
    b, nb, _ = kc.shape
    n_qt = tab.shape[1]
    t = n_qt * tq
    return pl.pallas_call(
        functools.partial(_cmp_select_t_kernel, tq=tq, n_sel=n_sel, k_top=k_top),
        grid=(n_qt, b),
        in_specs=[pl.BlockSpec((512, tq), lambda qi, bi: (0, bi * n_qt + qi)),
                  pl.BlockSpec((1, nb, 128), lambda qi, bi: (bi, 0, 0)),
                  pl.BlockSpec((1, nb, 128), lambda qi, bi: (bi, 0, 0)),
                  pl.BlockSpec((ATT_KV_HEADS, 1, nb, ATT_REP * tq), lambda qi, bi: (0, qi, 0, 0)),
                  pl.BlockSpec(msel_t.shape, lambda qi, bi: (0, 0)),
                  pl.BlockSpec((1, 128, tq), lambda qi, bi: (qi, 0, 0))],
        out_specs=[pl.BlockSpec((1, tq, 512), lambda qi, bi: (bi, qi, 0)),
                   pl.BlockSpec((1, ATT_KV_HEADS, 128, tq), lambda qi, bi: (bi, 0, 0, qi))],
        out_shape=[jax.ShapeDtypeStruct((b, t, 512), F32), jax.ShapeDtypeStruct((b, ATT_KV_HEADS, 128, t), BF16)],
        compiler_params=_params(("parallel", "parallel")),
        name="cmp_select_t",
    )(qt, kc, vc, tab, msel_t, ovr_t)


SUM_ROWS = 16


def _flash_init(m_ref, acc_ref):
    m_ref[...] = jnp.full(m_ref.shape, NEG, F32)
    acc_ref[...] = jnp.zeros(acc_ref.shape, F32)


def _flash_tile_t(k, qa_ref, q_rows, v_t, bias, m_ref, acc_ref):
    cols = slice(0, qa_ref.shape[1])
    s = _dot(k, qa_ref[0:q_rows, cols])
    if bias is not None:
        s = s + bias(cols)
    m_old = m_ref[...]
    m_new = jnp.maximum(m_old, jnp.max(s, axis=0, keepdims=True))
    alpha = jnp.exp(m_old - m_new)
    p = jnp.exp((s - m_new).astype(BF16))
    v_aug = jnp.concatenate([v_t, jnp.ones((SUM_ROWS, v_t.shape[1]), BF16)], axis=0)
    acc_ref[...] = alpha * acc_ref[...] + _dot(v_aug, p)
    m_ref[...] = m_new


def _flash_result(acc_ref, g):
    return acc_ref[g, 0:128] / acc_ref[g, 128:129]


def _group_queries_t(qt, g, tq):
    row = lax.broadcasted_iota(jnp.int32, (LANES, tq), 0)
    mine = (row >= g * HEAD_DIM) & (row < (g + 1) * HEAD_DIM)
    zero = jnp.zeros((LANES, tq), qt.dtype)
    return jnp.concatenate([jnp.where(mine, qt[r * 128:(r + 1) * 128, :], zero) for r in range(ATT_REP)], axis=1)


def _unpack_heads_t(o_t, g, tq):
    rows = jnp.concatenate([o_t[:, r * tq:(r + 1) * tq].T for r in range(ATT_REP)], axis=0)
    return _pack_heads(rows, g, tq)


def _gate_chunks(gd, gexp):
    sig = 1.0 / (1.0 + jnp.exp(-gd))
    return _split_dot(sig, gexp, 3)


def _combine(gx, o_cmp, o_sel, o_win):
    return gx[:, 0:256] * o_cmp + gx[:, 256:512] * o_sel + gx[:, 512:768] * o_win


def _prompt_attn_kernel(qt_ref, pen_ref, oc_ref, gd_ref, ksa_ref, vst_ref, kw_ref, vwt_ref, a_ref, gexp_ref,
                        o_ref, m_ref, acc_ref, qa_s, *, tq):
    qi = pl.program_id(1)
    groups = range(ATT_KV_HEADS)
    for g in groups:
        qa_s[g, 0:128, :] = _group_queries_t(qt_ref[...], g, tq)
        qa_s[g, 128:256, :] = jnp.concatenate([pen_ref[0, g]] * ATT_REP, axis=1)

    def sel_tile(start, n_keys, bias):
        k = ksa_ref[0, pl.ds(start, n_keys), :]
        v = vst_ref[:, pl.ds(start, n_keys)]
        for g in groups:
            _flash_tile_t(k, qa_s.at[g], 256, v, None if bias is None else functools.partial(bias, g),
                          m_ref.at[g], acc_ref.at[g])

    def win_tile(start, n_keys, bias):
        k = kw_ref[0, pl.ds(start, n_keys), :]
        v = vwt_ref[:, pl.ds(start, n_keys)]
        for g in groups:
            _flash_tile_t(k, qa_s.at[g], 128, v, functools.partial(bias, g), m_ref.at[g], acc_ref.at[g])

    def near(tile_fn, max_tiles):
        n_kinds = a_ref.shape[1]
        for n in range(1, max_tiles + 1):
            hit = (qi + 1 == n) if n < max_tiles else (qi + 1 >= n)

            def body(i, carry, n=n):
                bias = lambda g, cols: a_ref[g, n_kinds - n:n_kinds, :, cols].reshape(n * tq, cols.stop - cols.start)
                tile_fn(pl.multiple_of((qi + 1 - n) * tq, tq), n * tq, bias)
                return carry
            lax.fori_loop(0, hit.astype(jnp.int32), body, 0)

    _flash_init(m_ref, acc_ref)
    near(sel_tile, 2)
    n_far = jnp.maximum(qi - 1, 0)

    def far_pair(i, carry):
        sel_tile(pl.multiple_of(2 * i * tq, tq), 2 * tq, None)
        return carry
    lax.fori_loop(0, n_far // 2, far_pair, 0)

    def far_last(i, carry):
        sel_tile(pl.multiple_of((n_far - 1) * tq, tq), tq, None)
        return carry
    lax.fori_loop(0, n_far % 2, far_last, 0)
    o_sel = [_unpack_heads_t(_flash_result(acc_ref, g), g, tq) for g in groups]

    _flash_init(m_ref, acc_ref)
    near(win_tile, 3)
    for g in groups:
        o_win = _unpack_heads_t(_flash_result(acc_ref, g), g, tq)
        gx = _gate_chunks(gd_ref[0], gexp_ref[g])
        o_ref[0, :, g * 256:(g + 1) * 256] = _combine(gx, oc_ref[0, :, g * 256:(g + 1) * 256], o_sel[g], o_win).astype(BF16)


def _prompt_attn(qt, pen_t, o_cmp, gd, ksa, vst, kwb, vwt, atab, gexp, *, tq):
    b, t, _ = ksa.shape
    n_qt = t // tq
    nq = ATT_REP * tq
    qblk = lambda w: pl.BlockSpec((1, tq, w), lambda bi, qi: (bi, qi, 0))
    seq = lambda w: pl.BlockSpec((1, t, w), lambda bi, qi: (bi, 0, 0))
    seq_t = pl.BlockSpec((128, t), lambda bi, qi: (0, bi))
    full = lambda a: pl.BlockSpec(a.shape, lambda bi, qi: (0,) * a.ndim)
    return pl.pallas_call(
        functools.partial(_prompt_attn_kernel, tq=tq),
        grid=(b, n_qt),
        in_specs=[pl.BlockSpec((512, tq), lambda bi, qi: (0, bi * n_qt + qi)),
                  pl.BlockSpec((1, ATT_KV_HEADS, 128, tq), lambda bi, qi: (bi, 0, 0, qi)),
                  qblk(512), qblk(128), seq(256), seq_t, seq(128), seq_t, full(atab), full(gexp)],
        out_specs=qblk(512),
        out_shape=jax.ShapeDtypeStruct((b, t, 512), BF16),
        scratch_shapes=[pltpu.VMEM((ATT_KV_HEADS, 1, nq), F32),
                        pltpu.VMEM((ATT_KV_HEADS, 128 + SUM_ROWS, nq), F32), pltpu.VMEM((ATT_KV_HEADS, 256, nq), BF16)],
        compiler_params=_params(("parallel", "arbitrary")),
        name="prompt_attn",
    )(qt, pen_t, o_cmp, gd, ksa, vst, kwb, vwt, atab, gexp)


def _softmax_two(s_a, s_b, vt_a, v_b):
    m = jnp.maximum(jnp.max(s_a, axis=-1, keepdims=True), jnp.max(s_b, axis=-1, keepdims=True))
    p_a = jnp.exp(s_a - m)
    p_b = jnp.exp(s_b - m)
    l = jnp.sum(p_a, axis=-1, keepdims=True) + jnp.sum(p_b, axis=-1, keepdims=True)
    return (_dot_nt(p_a.astype(BF16), vt_a) + _dot(p_b.astype(BF16), v_b)) / l


def _slide_window(buf_t, new_t, n_new):
    w = buf_t.shape[1]
    rolled = pltpu.roll(buf_t, w - n_new, 1)
    lane = lax.broadcasted_iota(jnp.int32, (LANES, LANES), 1)
    last = jnp.where(lane >= LANES - n_new, new_t, rolled[:, w - LANES:w])
    return jnp.concatenate([rolled[:, 0:w - LANES], last], axis=1)


def _sample_attn_kernel(pt_ref, q_ref, pen_ref, oc_ref, gd_ref, ck_ref, cv_ref, ksn_ref, vsn_ref,
                        cw_ref, cvw_ref, kwn_ref, vwn_ref, kwt_ref, vwt_ref, eall_ref, tsel_ref, tnew_ref,
                        twin_ref, gexp_ref, o_ref, kwo_ref, vwo_ref, kbuf, vbuf, ke_s, ksem, vsem,
                        *, n_pages, n_new):
    slot = _gather_pipeline(pt_ref, (ck_ref, cv_ref), (kbuf, vbuf), (ksem, vsem), n_pages, lane_major=True)
    tq = TQS
    rows = ATT_REP * tq

    @pl.when(pl.program_id(0) == 0)
    def _():
        ke_s[128:256, :] = eall_ref[...]

    ke_s[0:128, :] = kbuf[slot].astype(BF16)
    q = q_ref[0]
    pen = pen_ref[0]
    qa = jnp.concatenate(
        [jnp.concatenate([_group_queries(q, g, tq), jnp.concatenate([pen[:, g * 128:(g + 1) * 128]] * ATT_REP, axis=0)],
                         axis=1) for g in range(ATT_KV_HEADS)], axis=0)
    qg = qa[:, 0:128]
    stack = lambda t_ref: t_ref[:, 0].reshape(ATT_KV_HEADS * rows, t_ref.shape[-1])
    tnew = stack(tnew_ref)
    s_past = _dot(qa, ke_s[...]) + stack(tsel_ref)
    s_new = _dot_nt(qg, ksn_ref[0]) + tnew
    o_sel = _softmax_two(s_past, s_new, vbuf[slot].astype(BF16), vsn_ref[0])
    s_buf = _dot(qg, cw_ref[0].astype(BF16)) + stack(twin_ref)
    s_new = _dot_nt(qg, kwn_ref[0]) + tnew
    o_win = _softmax_two(s_buf, s_new, cvw_ref[0].astype(BF16), vwn_ref[0])
    for g in range(ATT_KV_HEADS):
        gx = _gate_chunks(gd_ref[0], gexp_ref[g])
        part = slice(g * rows, (g + 1) * rows)
        o = _combine(gx, oc_ref[0, :, g * 256:(g + 1) * 256], _pack_heads(o_sel[part], g, tq), _pack_heads(o_win[part], g, tq))
        o_ref[0, :, g * 256:(g + 1) * 256] = o.astype(BF16)
    kwo_ref[0] = _slide_window(cw_ref[0], kwt_ref[0], n_new)
    vwo_ref[0] = _slide_window(cvw_ref[0], vwt_ref[0], n_new)


def _sample_attn(page_table, q, pen, o_cmp, gd, ck, cv, ksn, vsn, cw, cvw, kwn, vwn, kwt, vwt,
                 eall, tsel, tnew, twin, gexp, *, n_new):
    bs, n_pages = page_table.shape
    past = n_pages * PAGE_SIZE
    per_b = lambda a: pl.BlockSpec((1,) + a.shape[1:], lambda i, pt: (i,) + (0,) * (a.ndim - 1))
    full = lambda a: pl.BlockSpec(a.shape, lambda i, pt: (0,) * a.ndim)
    anyspec = pl.BlockSpec(memory_space=pl.ANY)
    ins = [q, pen, o_cmp, gd, ck, cv, ksn, vsn, cw, cvw, kwn, vwn, kwt, vwt, eall, tsel, tnew, twin, gexp]
    specs = [per_b(q), per_b(pen), per_b(o_cmp), per_b(gd), anyspec, anyspec, per_b(ksn), per_b(vsn),
             per_b(cw), per_b(cvw), per_b(kwn), per_b(vwn), per_b(kwt), per_b(vwt),
             full(eall), full(tsel), full(tnew), full(twin), full(gexp)]
    win = pl.BlockSpec((1, 128, WINDOW), lambda i, pt: (i, 0, 0))
    return pl.pallas_call(
        functools.partial(_sample_attn_kernel, n_pages=n_pages, n_new=n_new),
        grid_spec=pltpu.PrefetchScalarGridSpec(
            num_scalar_prefetch=1,
            grid=(bs,),
            in_specs=specs,
            out_specs=[pl.BlockSpec((1, TQS, 512), lambda i, pt: (i, 0, 0)), win, win],
            scratch_shapes=[pltpu.VMEM((2, 128, past), F32), pltpu.VMEM((2, 128, past), F32),
                            pltpu.VMEM((256, past), BF16),
                            pltpu.SemaphoreType.DMA((2,)), pltpu.SemaphoreType.DMA((2,))]),
        out_shape=[jax.ShapeDtypeStruct((bs, TQS, 512), BF16),
                   jax.ShapeDtypeStruct((bs, 128, WINDOW), F32), jax.ShapeDtypeStruct((bs, 128, WINDOW), F32)],
        compiler_params=_params(("arbitrary",)),
        name="sample_attn",
    )(page_table, *ins)


def _lane_pair(cols, h0, h1, rows):
    lane = lax.broadcasted_iota(jnp.int32, (rows, LANES), 1)
    a = jnp.broadcast_to(cols[:, h0:h0 + 1], (rows, LANES))
    b = jnp.broadcast_to(cols[:, h1:h1 + 1], (rows, LANES))
    return jnp.where(lane < HEAD_DIM, a, b)


def _ssd_kernel(xbc_ref, z_ref, gd_ref, carry_ref, h0_ref, cw_ref, cb_ref, dtb_ref, arow_ref, dskip_ref,
                gnorm_ref, ltri_ref, y_ref, hout_ref, h_s, xfull, *, rows, n_valid):
    c = pl.program_id(1)
    halo = 8

    @pl.when(c == 0)
    def _():
        h_s[...] = h0_ref[0]
        xfull[0:halo, :] = carry_ref[0]

    xfull[halo:halo + rows, :] = xbc_ref[0]
    conv = cb_ref[...]
    for k in range(SSD_CONV):
        conv = conv + xfull[pl.ds(halo - (SSD_CONV - 1) + k, rows), :] * cw_ref[k:k + 1, :]
    tail = xfull[rows:rows + halo, :]
    xfull[0:halo, :] = tail
    xc = _silu(conv)
    xs = xc[:, 0:D_SSD]
    bm = xc[:, D_SSD:D_SSD + 128]
    cm = xc[:, D_SSD + 128:D_SSD + 256]

    lane = lax.broadcasted_iota(jnp.int32, (rows, LANES), 1)
    rowi = lax.broadcasted_iota(jnp.int32, (rows, LANES), 0)
    t = gd_ref[0] + dtb_ref[...]
    sp = jnp.maximum(t, 0.0) + jnp.log(1.0 + jnp.exp(-jnp.abs(t)))
    dt = jnp.where((lane >= DT_LANE) & (lane < DT_LANE + SSD_HEADS) & (rowi < n_valid), sp, 0.0)
    a = dt * arow_ref[...]
    acum = _split_dot_left(ltri_ref[...], a)
    acum_t = acum.T
    a_last = acum[rows - 1:rows, :]
    to_end = jnp.exp(a_last - acum)
    eac = jnp.exp(acum)
    dec = jnp.exp(a_last)

    li = lax.broadcasted_iota(jnp.int32, (rows, rows), 0)
    si = lax.broadcasted_iota(jnp.int32, (rows, rows), 1)
    causal = li >= si
    bmb = bm.astype(BF16)
    rowp = lax.broadcasted_iota(jnp.int32, (LANES, LANES), 0)
    cbs = []
    cmask = []
    for g in range(2):
        cg = jnp.where((lane >= g * 64) & (lane < (g + 1) * 64), cm, 0.0).astype(BF16)
        cmask.append(cg)
        cbs.append(_dot_nt(cg, bmb))
    for k in range(SSD_HEADS // 2):
        g = k // 2
        h0, h1 = DT_LANE + 2 * k, DT_LANE + 2 * k + 1
        xs_p = xs[:, k * 128:(k + 1) * 128]
        xdt = xs_p * _lane_pair(dt, h0, h1, rows)
        xdt_b = xdt.astype(BF16)
        ys = []
        for h in (h0, h1):
            seg = jnp.broadcast_to(acum[:, h:h + 1], (rows, rows)) - acum_t[h:h + 1, :]
            decay = jnp.where(causal, jnp.exp(jnp.where(causal, seg, 0.0)), 0.0)
            ys.append(_dot((cbs[g] * decay).astype(BF16), xdt_b))
        y = jnp.where(lane < HEAD_DIM, ys[0], ys[1])
        hp = h_s[k * 128:(k + 1) * 128, :]
        y = y + _dot_nt(cmask[g], hp.astype(BF16)) * _lane_pair(eac, h0, h1, rows)
        y = y + dskip_ref[:, k * 128:(k + 1) * 128] * xs_p
        xw = xdt * _lane_pair(to_end, h0, h1, rows)
        st = _dot(xw.T.astype(BF16), bmb)
        dfac = jnp.where(rowp < HEAD_DIM, dec[:, h0:h0 + 1], dec[:, h1:h1 + 1])
        h_s[k * 128:(k + 1) * 128, :] = hp * dfac + st
        xfull_y = y * _silu(z_ref[0, :, k * 128:(k + 1) * 128])
        y_ref[0, :, k * 128:(k + 1) * 128] = xfull_y.astype(y_ref.dtype)

    for g in range(2):
        yg = y_ref[0, :, g * 256:(g + 1) * 256].astype(F32)
        ms = jnp.mean(yg * yg, axis=-1, keepdims=True)
        y_ref[0, :, g * 256:(g + 1) * 256] = (yg * lax.rsqrt(ms + RMS_EPS)
                                               * gnorm_ref[:, g * 256:(g + 1) * 256]).astype(y_ref.dtype)

    @pl.when(c == pl.num_programs(1) - 1)
    def _():
        half = D_SSD // 2
        hout_ref[0, 0:half, :] = h_s[0:half, 0:SSD_STATE]
        hout_ref[0, half:D_SSD, :] = h_s[half:D_SSD, SSD_STATE:2 * SSD_STATE]


def _split_dot_left(tri, a):
    acc = None
    rem = a
    for _ in range(3):
        piece = rem.astype(BF16)
        rem = rem - piece.astype(F32)
        t = _dot(tri, piece)
        acc = t if acc is None else acc + t
    return acc


def _ssd(xbc, z, gd, carry, h0, wts, *, n_valid):
    b, l, _ = xbc.shape
    rows = SSD_CHUNK
    nc = l // rows
    blk = lambda w: pl.BlockSpec((1, rows, w), lambda bi, ci: (bi, ci, 0))
    per_b = lambda a: pl.BlockSpec((1,) + a.shape[1:], lambda bi, ci: (bi,) + (0,) * (a.ndim - 1))
    full = lambda a: pl.BlockSpec(a.shape, lambda bi, ci: (0,) * a.ndim)
    consts = [wts["conv_w"], wts["conv_b"], wts["dtb"], wts["arow"], wts["dskip"], wts["gnorm"], wts["ltri"]]
    return pl.pallas_call(
        functools.partial(_ssd_kernel, rows=rows, n_valid=n_valid),
        grid=(b, nc),
        in_specs=[blk(768), blk(512), blk(128), per_b(carry), per_b(h0)] + [full(a) for a in consts],
        out_specs=[blk(512), pl.BlockSpec((1, D_SSD, SSD_STATE), lambda bi, ci: (bi, 0, 0))],
        out_shape=[jax.ShapeDtypeStruct((b, l, 512), F32), jax.ShapeDtypeStruct((b, D_SSD, SSD_STATE), F32)],
        scratch_shapes=[pltpu.VMEM((512, 128), F32), pltpu.VMEM((rows + 8, 768), F32)],
        compiler_params=_params(("parallel", "arbitrary")),
        name="ssd",
    )(xbc, z, gd, carry, h0, *consts)


def _finish_kernel(x_ref, oa_ref, ys_ref, woa_ref, wos_ref, gf_ref, wg_ref, wu_ref, wd_ref,
                   y_ref, h_s, u_s, acc_s):
    f = pl.program_id(1)

    @pl.when(f == 0)
    def _():
        h = x_ref[...] + _dot(oa_ref[...], woa_ref[...]) + _dot(ys_ref[...].astype(BF16), wos_ref[...])
        h_s[...] = h
        ms = jnp.mean(h * h, axis=-1, keepdims=True)
        u_s[...] = (h * lax.rsqrt(ms + RMS_EPS) * gf_ref[...]).astype(BF16)
        acc_s[...] = jnp.zeros(acc_s.shape, F32)

    u = u_s[...]
    act = _silu(_dot(u, wg_ref[...])) * _dot(u, wu_ref[...])
    acc_s[...] += _dot(act.astype(BF16), wd_ref[...])

    @pl.when(f == pl.num_programs(1) - 1)
    def _():
        y_ref[...] = h_s[...] + acc_s[...]


def _finish(x2d, o_att, y_ssd, wts):
    n = x2d.shape[0]
    tm = min(512, n)
    nf = 2
    tf = D_FF // nf
    row = lambda w: pl.BlockSpec((tm, w), lambda i, f: (i, 0))
    full = lambda a: pl.BlockSpec(a.shape, lambda i, f: (0,) * a.ndim)
    return pl.pallas_call(
        _finish_kernel,
        grid=(n // tm, nf),
        in_specs=[row(D_MODEL), row(512), row(512), full(wts["wo_att"]), full(wts["wo_ssd"]), full(wts["gffn"]),
                  pl.BlockSpec((D_MODEL, tf), lambda i, f: (0, f)),
                  pl.BlockSpec((D_MODEL, tf), lambda i, f: (0, f)),
                  pl.BlockSpec((tf, D_MODEL), lambda i, f: (f, 0))],
        out_specs=row(D_MODEL),
        out_shape=jax.ShapeDtypeStruct((n, D_MODEL), F32),
        scratch_shapes=[pltpu.VMEM((tm, D_MODEL), F32), pltpu.VMEM((tm, D_MODEL), BF16),
                        pltpu.VMEM((tm, D_MODEL), F32)],
        compiler_params=_params(("parallel", "arbitrary")),
        name="finish",
    )(x2d, o_att, y_ssd, wts["wo_att"], wts["wo_ssd"], wts["gffn"], wts["w_gate"], wts["w_up"], wts["w_down"])


def _pair_perm():
    cols = []
    for r in range(ATT_REP):
        cols += list(range(r * HEAD_DIM, (r + 1) * HEAD_DIM))
        cols += list(range((ATT_REP + r) * HEAD_DIM, (ATT_REP + r + 1) * HEAD_DIM))
    return np.asarray(cols, np.int32)


def _block_ones(n, blk):
    i = np.arange(n)
    return (i[:, None] // blk == i[None, :] // blk).astype(np.float32) / blk


def _prep_weights(norm_mix, w_in, q_norm, k_norm, cmp_pe, cmp_w1, cmp_w2, conv_w, conv_b, dt_bias, a_log,
                  d_skip, ssd_norm, w_out, norm_ffn, w_gate, w_up, w_down):
    perm = _pair_perm()
    w = w_in
    gd = jnp.concatenate([w[:, OFF_GATE:OFF_Z], w[:, OFF_DT:D_IN],
                          jnp.zeros((D_MODEL, 128 - 3 * ATT_HEADS - SSD_HEADS), w.dtype)], axis=1)
    w_r = jnp.concatenate([w[:, :D_ATT][:, perm], w[:, OFF_KV:OFF_GATE], w[:, OFF_Z:OFF_XBC],
                           w[:, OFF_XBC:OFF_DT], gd], axis=1).astype(BF16)
    wts = dict(
        gmix=norm_mix.reshape(1, D_MODEL), w_in=w_r,
        qg=(jnp.tile(q_norm, ATT_HEADS) * (HEAD_DIM ** -0.5)).reshape(1, D_ATT),
        kg=jnp.tile(k_norm, ATT_KV_HEADS).reshape(1, D_KV),
        bq=jnp.asarray(_block_ones(D_ATT, HEAD_DIM), BF16), bk=jnp.asarray(_block_ones(D_KV, HEAD_DIM), BF16))

    def w1_big(w1):
        w1r = w1.reshape(2, CMP_STRIDE, HEAD_DIM, CMP_HIDDEN)
        eye = jnp.eye(ATT_KV_HEADS, dtype=w1.dtype)
        big = jnp.einsum("jsdh,gk->sgdjkh", w1r, eye)
        return big.reshape(CMP_STRIDE * D_KV, 2 * D_KV).astype(BF16)

    def pe_rows(pe):
        per = jnp.broadcast_to(pe.reshape(2, CMP_STRIDE, 1, HEAD_DIM), (2, CMP_STRIDE, ATT_KV_HEADS, HEAD_DIM))
        per = per.reshape(2, CMP_STRIDE * D_KV)
        return jnp.concatenate([per, jnp.zeros((6, CMP_STRIDE * D_KV), pe.dtype)], axis=0).astype(BF16)

    def w2_big(w2):
        eye = jnp.eye(ATT_KV_HEADS, dtype=w2.dtype)
        return jnp.einsum("hd,gk->ghkd", w2, eye).reshape(D_KV, D_KV).astype(BF16)

    wts.update(w1k=w1_big(cmp_w1[0]), w1v=w1_big(cmp_w1[1]), pek=pe_rows(cmp_pe[0]), pev=pe_rows(cmp_pe[1]),
               w2k=w2_big(cmp_w2[0]), w2v=w2_big(cmp_w2[1]))

    pad_lanes = lambda v: jnp.zeros((1, LANES), F32).at[0, DT_LANE:DT_LANE + SSD_HEADS].set(v)
    ltri = np.tril(np.ones((SSD_CHUNK, SSD_CHUNK), np.float32))
    wts.update(conv_w=jnp.concatenate([conv_w, jnp.zeros((4, D_CONV), F32)], axis=0), conv_b=conv_b.reshape(1, D_CONV),
               dtb=pad_lanes(dt_bias), arow=pad_lanes(-jnp.exp(a_log)),
               dskip=jnp.repeat(d_skip, 64).reshape(1, D_SSD), gnorm=ssd_norm.reshape(1, D_SSD),
               ltri=jnp.asarray(ltri, BF16))
    wts.update(wo_att=w_out[:D_ATT].astype(BF16), wo_ssd=w_out[D_ATT:].astype(BF16),
               gffn=norm_ffn.reshape(1, D_MODEL), w_gate=w_gate.astype(BF16), w_up=w_up.astype(BF16),
               w_down=w_down.astype(BF16))
    return wts


def _gate_expand():
    m = np.zeros((ATT_KV_HEADS, LANES, 3 * ATT_REP * HEAD_DIM), np.float32)
    for g in range(ATT_KV_HEADS):
        for r in range(ATT_REP):
            for br in range(3):
                c0 = br * ATT_REP * HEAD_DIM + r * HEAD_DIM
                m[g, g * 3 * ATT_REP + r * 3 + br, c0:c0 + HEAD_DIM] = 1.0
    return jnp.asarray(m, BF16)


def _sel_matrix(n_blk_pad, n_sel):
    m = np.zeros((n_blk_pad, LANES), np.float32)
    for n in range(n_blk_pad - 1):
        for j in {n // 4, (n + 1) // 4}:
            if j < n_sel:
                m[n, j] = 1.0
    return jnp.asarray(m, BF16)


def _override(q_pos, n_sel):
    j = np.arange(LANES)[None, :]
    cur = (q_pos // SEL_BLOCK)[:, None]
    forced = (j == 0) | (j == cur) | (j == cur - 1)
    ovr = np.where(forced, 1e30, np.where(j <= cur, 0.0, -1e30))
    ovr = np.where(j < n_sel, ovr, -1e30)
    return ovr.astype(np.float32)


def _prompt_tables(rel_bias, t):
    n_qt = t // TQ
    pos = np.arange(t)
    nb = t // CMP_STRIDE
    e = CMP_STRIDE * np.arange(nb) + (CMP_BLOCK - 1)
    dist = pos[None, :] - e[:, None]
    cmp_idx = _idx_table(dist, (dist >= 0) & (np.arange(nb)[:, None] < nb - 1))
    cmp_idx = cmp_idx.reshape(nb, n_qt, TQ).transpose(1, 0, 2)
    i = np.arange(TQ)[None, :]
    j = np.arange(TQ)[:, None]
    diag = _idx_table(i - j, i >= j)
    prev = _idx_table(TQ + i - j, np.ones((TQ, TQ), bool))
    prev2 = _idx_table(2 * TQ + i - j, (2 * TQ + i - j) < WINDOW)
    att_idx = np.stack([prev2, prev, diag])
    return _bias_tables(rel_bias, cmp_idx, stack_cols=True), _bias_tables(rel_bias, att_idx, stack_cols=True)


def _prompt_layer(x, wts, rel_bias):
    b, t, _ = x.shape
    n = b * t
    p = _project(x.reshape(n, D_MODEL), wts, t, "prompt")
    n_sub = t // CMP_STRIDE
    sub = lambda a: a.reshape(b, n_sub, CMP_STRIDE * D_KV)
    kcmp, vcmp = _compress_prompt(sub(p["kc_b"]), sub(p["vc_b"]), wts)
    cmp_tab, att_tab = _prompt_tables(rel_bias, t)
    n_sel = t // SEL_BLOCK
    ovr_t = jnp.asarray(_override(np.arange(t), n_sel).reshape(t // TQ, TQ, LANES).transpose(0, 2, 1))
    r3 = lambda a: a.reshape(b, t, a.shape[-1])
    o_cmp, pen_t = _cmp_select_t(p["q_t"], kcmp, vcmp, cmp_tab, _sel_matrix(n_sub, n_sel).T, ovr_t,
                                 tq=TQ, n_sel=n_sel, k_top=min(SEL_TOPN, n_sel))
    o_att = _prompt_attn(p["q_t"], pen_t, o_cmp, r3(p["gd"]), r3(p["ksa"]), p["vs_tb"], r3(p["kw_b"]), p["vw_tb"],
                         att_tab, _gate_expand(), tq=TQ)
    carry = jnp.zeros((b, 8, D_CONV), F32)
    h0 = jnp.zeros((b, 512, 128), F32)
    y_ssd, h_fin = _ssd(r3(p["xbc"]), r3(p["z"]), r3(p["gd"]), carry, h0, wts, n_valid=SSD_CHUNK)
    y = _finish(x.reshape(n, D_MODEL), o_att.reshape(n, 512), y_ssd.reshape(n, 512), wts)
    wb = min(WINDOW, t)
    kv_out = {k: _token_major(p[k + "_t"]) for k in KV_NAMES}
    ssm = h_fin.reshape(b, SSD_HEADS, 64, SSD_STATE)
    return y.reshape(b, t, D_MODEL), (kv_out["kc"], kv_out["vc"], kv_out["ks"], kv_out["vs"],
                                      kv_out["kw"][:, t - wb:], kv_out["vw"][:, t - wb:],
                                      r3(p["xbc"])[:, t - (SSD_CONV - 1):], ssm)


def _token_major(a_t):
    b, _, t = a_t.shape
    return jnp.swapaxes(a_t, 1, 2).reshape(b, t, ATT_KV_HEADS, HEAD_DIM)


def _feature_major(a):
    return jnp.swapaxes(a.reshape(a.shape[:-2] + (D_KV,)), -1, -2)


def _sample_tables(rel_bias, past, s_new, n_blk_pad):
    s = np.minimum(np.arange(TQS), s_new - 1)[:, None]
    pos = past + s
    nidx = np.arange(n_blk_pad)[None, :]
    e = CMP_STRIDE * nidx + (CMP_BLOCK - 1)
    cmp_idx = _idx_table(pos - e, (e <= pos) & (nidx < n_blk_pad - 1))[None]
    key = np.arange(past)[None, :]
    sel_idx = _idx_table(pos - key, np.ones((TQS, past), bool))[None]
    jn = np.arange(LANES)[None, :]
    new_idx = _idx_table(s - jn, (jn <= s) & (jn < s_new))[None]
    wi = np.arange(WINDOW)[None, :]
    wdist = pos - (past - WINDOW + wi)
    win_idx = _idx_table(wdist, (wdist >= 0) & (wdist < WINDOW))[None]
    tabs = [_bias_tables(rel_bias, t) for t in (cmp_idx, sel_idx, new_idx, win_idx)]
    return tabs, pos[:, 0]


def _sample_layer(x, c_kc, c_vc, c_ks, c_vs, c_kw, c_vw, s_conv, s_ssm, page_table, wts, rel_bias):
    bs, s_new, _ = x.shape
    n = bs * s_new
    n_pages = page_table.shape[1]
    past = n_pages * PAGE_SIZE
    p = _project(x.reshape(n, D_MODEL), wts, s_new, "sample")
    r3 = lambda a: a.reshape(bs, s_new, a.shape[-1])
    padq = lambda a, rows: jnp.pad(r3(a), ((0, 0), (0, rows - s_new), (0, 0)))
    new_t = lambda a_t: jnp.pad(jnp.swapaxes(a_t.reshape(D_KV, bs, s_new), 0, 1), ((0, 0), (0, 0), (LANES - s_new, 0)))
    kcmp, vcmp = _compress_sample(page_table, _feature_major(c_kc), _feature_major(c_vc), wts)
    n_blk_pad = past // CMP_STRIDE
    (cmp_tab, sel_tab, new_tab, win_tab), pos = _sample_tables(rel_bias, past, s_new, n_blk_pad)
    n_sel = past // SEL_BLOCK
    ovr = jnp.asarray(_override(pos, n_sel)[None])
    qp = padq(p["q"], TQS)
    o_cmp, pen = _cmp_select(qp, kcmp, vcmp, cmp_tab, _sel_matrix(n_blk_pad, n_sel), ovr,
                             tq=TQS, n_sel=n_sel, k_top=min(SEL_TOPN - 1, n_sel))
    blk_of_key = np.arange(past)[None, :] // SEL_BLOCK
    eall = jnp.asarray((np.arange(LANES)[:, None] == blk_of_key).astype(np.float32), BF16)
    o_att, kw_new, vw_new = _sample_attn(
        page_table, qp, pen, o_cmp, padq(p["gd"], TQS),
        _feature_major(c_ks), _feature_major(c_vs), padq(p["ks_b"], LANES), padq(p["vs_b"], LANES),
        _feature_major(c_kw), _feature_major(c_vw), padq(p["kw_b"], LANES), padq(p["vw_b"], LANES),
        new_t(p["kw_t"]), new_t(p["vw_t"]),
        eall, sel_tab, new_tab, win_tab, _gate_expand(), n_new=s_new)
    carry = jnp.pad(s_conv, ((0, 0), (8 - (SSD_CONV - 1), 0), (0, 0)))
    h0 = s_ssm.reshape(bs, 512, SSD_STATE)
    h0 = jnp.concatenate([h0, h0], axis=-1)
    y_ssd, h_fin = _ssd(padq(p["xbc"], SSD_CHUNK), padq(p["z"], SSD_CHUNK), padq(p["gd"], SSD_CHUNK), carry, h0, wts,
                        n_valid=s_new)
    y = _finish(x.reshape(n, D_MODEL), o_att[:, :s_new].reshape(n, 512), y_ssd[:, :s_new].reshape(n, 512), wts)
    kv4 = lambda a: a.reshape(bs, -1, ATT_KV_HEADS, HEAD_DIM)
    ssm = h_fin.reshape(bs, SSD_HEADS, 64, SSD_STATE)
    conv_state = jnp.concatenate([s_conv, r3(p["xbc"])], axis=1)[:, s_new:]
    return y.reshape(bs, s_new, D_MODEL), (kv4(p["kc"]), kv4(p["vc"]), kv4(p["ks"]), kv4(p["vs"]),
                                           _token_major(kw_new), _token_major(vw_new), conv_state, ssm)


def kernel(x_prompt, x_sample, cache_k_cmp, cache_v_cmp, cache_k_sel, cache_v_sel, cache_k_win, cache_v_win,
           state_conv, state_ssm, page_table, norm_mix, w_in, q_norm, k_norm, cmp_pe, cmp_w1, cmp_w2, rel_bias,
           conv_w, conv_b, dt_bias, a_log, d_skip, ssd_norm, w_out, norm_ffn, w_gate, w_up, w_down):
    depth = w_in.shape[0]
    y_p, y_s = x_prompt, x_sample
    p_states, s_states = [], []
    for l in range(depth):
        wts = _prep_weights(norm_mix[l], w_in[l], q_norm[l], k_norm[l], cmp_pe[l], cmp_w1[l], cmp_w2[l],
                            conv_w[l], conv_b[l], dt_bias[l], a_log[l], d_skip[l], ssd_norm[l], w_out[l],
                            norm_ffn[l], w_gate[l], w_up[l], w_down[l])
        y_p, st_p = _prompt_layer(y_p, wts, rel_bias)
        y_s, st_s = _sample_layer(y_s, cache_k_cmp[l], cache_v_cmp[l], cache_k_sel[l], cache_v_sel[l],
                                  cache_k_win[l], cache_v_win[l], state_conv[l], state_ssm[l], page_table,
                                  wts, rel_bias)
        p_states.append(st_p)
        s_states.append(st_s)
    p_out = [jnp.stack(a) for a in zip(*p_states)]
    s_out = [jnp.stack(a) for a in zip(*s_states)]
    return (y_p, y_s, *p_out, *s_out)
```

```python
import functools
import math

import numpy as np
import jax
import jax.numpy as jnp
from jax import lax
from jax.experimental import pallas as pl
from jax.experimental.pallas import tpu as pltpu

F32 = jnp.float32
BF16 = jnp.bfloat16

D_MODEL = 1024
HEAD_DIM = 64
ATT_HEADS = 8
ATT_KV_HEADS = 2
ATT_REP = ATT_HEADS // ATT_KV_HEADS
CMP_BLOCK = 32
CMP_STRIDE = 16
CMP_HIDDEN = 64
SEL_BLOCK = 64
SEL_TOPN = 16
WINDOW = 512
N_BUCKETS = 32
MAX_DISTANCE = 128
PAGE_SIZE = 128
SSD_HEADS = 8
SSD_STATE = 64
SSD_CONV = 4
SSD_CHUNK = 128
D_ATT = ATT_HEADS * HEAD_DIM
D_SSD = SSD_HEADS * 64
D_KV = ATT_KV_HEADS * HEAD_DIM
D_CONV = D_SSD + 2 * 2 * SSD_STATE
D_FF = ((8 * D_MODEL // 3 + 255) // 256) * 256
OFF_KV = D_ATT
OFF_GATE = OFF_KV + 6 * D_KV
OFF_Z = OFF_GATE + 3 * ATT_HEADS
OFF_XBC = OFF_Z + D_SSD
OFF_DT = OFF_XBC + D_CONV
D_IN = OFF_DT + SSD_HEADS
RMS_EPS = 1e-6
NEG = -1e30

C_Q, C_KV, C_Z, C_XBC, C_GD = 0, 512, 1280, 1792, 2560
D_INR = 2688
DT_LANE = 24
LANES = 128
TQ = 256
TQS = 16
VMEM_LIMIT = 48 * 1024 * 1024


def _dot(a, b):
    return jnp.dot(a, b, preferred_element_type=F32)


def _dot_nt(a, b):
    return lax.dot_general(a, b, (((1,), (1,)), ((), ())), preferred_element_type=F32)


def _split_dot(a, b, parts):
    acc = None
    rem = a
    for _ in range(parts):
        piece = rem.astype(BF16)
        rem = rem - piece.astype(F32)
        t = _dot(piece, b)
        acc = t if acc is None else acc + t
    return acc


def _silu(x):
    return x * (1.0 / (1.0 + jnp.exp(-x)))


def _params(sem=None):
    kw = dict(vmem_limit_bytes=VMEM_LIMIT)
    if sem is not None:
        kw["dimension_semantics"] = sem
    return pltpu.CompilerParams(**kw)


KV_NAMES = ("kc", "vc", "ks", "vs", "kw", "vw")
PROJ_OUTPUTS = {
    "prompt": ([(k + "_t", "seq", 128, F32) for k in KV_NAMES]
               + [("kc_b", "row", 128, BF16), ("vc_b", "row", 128, BF16), ("ksa", "row", 256, BF16),
                  ("kw_b", "row", 128, BF16), ("q_t", "col", 512, BF16), ("vs_tb", "col", 128, BF16),
                  ("vw_tb", "col", 128, BF16)]),
    "sample": ([(k, "row", 128, F32) for k in KV_NAMES]
               + [("q", "row", 512, BF16), ("ks_b", "row", 128, BF16), ("vs_b", "row", 128, BF16),
                  ("kw_b", "row", 128, BF16), ("vw_b", "row", 128, BF16),
                  ("kw_t", "col", 128, F32), ("vw_t", "col", 128, F32)]),
}
PROJ_COMMON = [("z", "row", 512, F32), ("xbc", "row", 768, F32), ("gd", "row", 128, F32)]


def _proj_kernel(x_ref, gmix_ref, w_ref, qg_ref, kg_ref, bq_ref, bk_ref, *out_refs, names, tm, t_len):
    out = dict(zip(names, out_refs))
    x = x_ref[...]
    ms = jnp.mean(x * x, axis=-1, keepdims=True)
    u = (x * lax.rsqrt(ms + RMS_EPS) * gmix_ref[...]).astype(BF16)

    parts = [(lo, _dot(u, w_ref[:, lo:hi])) for lo, hi in ((C_Q, C_Z), (C_Z, D_INR))]

    def proj(lo, hi):
        base, val = parts[0] if hi <= C_Z else parts[1]
        return val[:, lo - base:hi - base]

    def headnorm(v, b_ref, g_ref):
        msq = _dot((v * v).astype(BF16), b_ref[...])
        return v * lax.rsqrt(msq + RMS_EPS) * g_ref[...]

    def put(name, value):
        if name in out:
            ref = out[name]
            ref[...] = value().astype(ref.dtype).reshape(ref.shape)

    q = headnorm(proj(C_Q, C_Q + 512), bq_ref, qg_ref)
    put("q", lambda: q)
    put("q_t", lambda: q.T)
    kv = {}
    for i, name in enumerate(KV_NAMES):
        v = proj(C_KV + 128 * i, C_KV + 128 * (i + 1))
        kv[name] = headnorm(v, bk_ref, kg_ref) if name in ("ks", "kw") else v
    for name, v in kv.items():
        put(name, lambda v=v: v)
        put(name + "_b", lambda v=v: v)
        put(name + "_t", lambda v=v: v.T)
        put(name + "_tb", lambda v=v: v.T)
    if "ksa" in out:
        row = pl.program_id(0) * tm + lax.broadcasted_iota(jnp.int32, (tm, LANES), 0)
        blk = (row % t_len) // SEL_BLOCK
        lane = lax.broadcasted_iota(jnp.int32, (tm, LANES), 1)
        out["ksa"][:, 0:128] = kv["ks"].astype(BF16)
        out["ksa"][:, 128:256] = jnp.where(lane == blk, 1.0, 0.0).astype(BF16)
    put("z", lambda: proj(C_Z, C_Z + 512))
    put("xbc", lambda: proj(C_XBC, C_XBC + 768))
    put("gd", lambda: proj(C_GD, C_GD + 128))


def _project(x2d, wts, t_len, mode):
    n = x2d.shape[0]
    tm = min(512, n)
    per_seq = max(t_len // tm, 1)
    full = lambda a: pl.BlockSpec(a.shape, lambda i: (0,) * a.ndim)
    ins = [x2d, wts["gmix"], wts["w_in"], wts["qg"], wts["kg"], wts["bq"], wts["bk"]]
    outs = PROJ_OUTPUTS[mode] + PROJ_COMMON
    specs, shapes = [], []
    for _, layout, w, dt in outs:
        if layout == "row":
            specs.append(pl.BlockSpec((tm, w), lambda i: (i, 0)))
            shapes.append(jax.ShapeDtypeStruct((n, w), dt))
        elif layout == "col":
            specs.append(pl.BlockSpec((w, tm), lambda i: (0, i)))
            shapes.append(jax.ShapeDtypeStruct((w, n), dt))
        else:
            specs.append(pl.BlockSpec((1, w, tm), lambda i: (i // per_seq, 0, i % per_seq)))
            shapes.append(jax.ShapeDtypeStruct((n // t_len, w, t_len), dt))
    names = tuple(name for name, _, _, _ in outs)
    res = pl.pallas_call(
        functools.partial(_proj_kernel, names=names, tm=tm, t_len=t_len),
        grid=(n // tm,),
        in_specs=[pl.BlockSpec((tm, D_MODEL), lambda i: (i, 0))] + [full(a) for a in ins[1:]],
        out_specs=specs,
        out_shape=shapes,
        compiler_params=_params(("parallel",)),
        name="proj",
    )(*ins)
    return dict(zip(names, res))


def _bucket_np(dist):
    n = np.maximum(dist, 0)
    max_exact = N_BUCKETS // 2
    nf = np.maximum(n, 1).astype(np.float64)
    large = max_exact + (np.log(nf / max_exact) / math.log(MAX_DISTANCE / max_exact)
                         * (N_BUCKETS - max_exact)).astype(np.int64)
    large = np.minimum(large, N_BUCKETS - 1)
    return np.where(n < max_exact, n, large).astype(np.int32)


def _idx_table(dist, valid):
    return np.where(valid, _bucket_np(dist), -1).astype(np.int32)


def _table_kernel(rb_ref, idx_ref, out_ref):
    h = pl.program_id(0) * ATT_REP + pl.program_id(2)
    idx = idx_ref[0]
    far = rb_ref[N_BUCKETS - 1, h]
    acc = jnp.zeros(idx.shape, F32)
    for b in range(N_BUCKETS - 1):
        acc = jnp.where(idx == b, rb_ref[b, h] - far, acc)
    out_ref[...] = jnp.where(idx < 0, NEG, acc).reshape(out_ref.shape)


def _bias_tables(rel_bias, idx, stack_cols=False):
    k, r, c = idx.shape
    if stack_cols:
        return pl.pallas_call(
            _table_kernel,
            grid=(ATT_KV_HEADS, k, ATT_REP),
            in_specs=[pl.BlockSpec(memory_space=pltpu.SMEM),
                      pl.BlockSpec((1, r, c), lambda g, kk, rr: (kk, 0, 0))],
            out_specs=pl.BlockSpec((1, 1, r, c), lambda g, kk, rr: (g, kk, 0, rr)),
            out_shape=jax.ShapeDtypeStruct((ATT_KV_HEADS, k, r, ATT_REP * c), F32),
            name="bias_table_t",
        )(rel_bias, jnp.asarray(idx))
    out = pl.pallas_call(
        _table_kernel,
        grid=(ATT_KV_HEADS, k, ATT_REP),
        in_specs=[pl.BlockSpec(memory_space=pltpu.SMEM),
                  pl.BlockSpec((1, r, c), lambda g, kk, rr: (kk, 0, 0))],
        out_specs=pl.BlockSpec((1, 1, 1, r, c), lambda g, kk, rr: (g, kk, rr, 0, 0)),
        out_shape=jax.ShapeDtypeStruct((ATT_KV_HEADS, k, ATT_REP, r, c), F32),
        name="bias_table",
    )(rel_bias, jnp.asarray(idx))
    return out.reshape(ATT_KV_HEADS, k, ATT_REP * r, c)


def _compress_core(x, w1, pe, w2):
    return _compress_tail(_dot(x.astype(BF16), w1), w1, pe, w2)


def _compress_tail(u, w1, pe, w2):
    n_sub = u.shape[0]
    upe = _dot(pe, w1)
    nxt = pltpu.roll(u[:, 128:256], n_sub - 1, 0)
    pre = u[:, 0:128] + nxt + upe[0:1, 0:128] + upe[1:2, 128:256]
    return _dot(_silu(pre).astype(BF16), w2)


def _knorm(v, bk, kg):
    msq = _dot((v * v).astype(BF16), bk)
    return v * lax.rsqrt(msq + RMS_EPS) * kg


def _compress_prompt_kernel(kc_ref, vc_ref, w1k_ref, w1v_ref, pek_ref, pev_ref, w2k_ref, w2v_ref,
                            kg_ref, bk_ref, ko_ref, vo_ref):
    kc = _compress_core(kc_ref[0], w1k_ref[...], pek_ref[...], w2k_ref[...])
    ko_ref[0] = _knorm(kc, bk_ref[...], kg_ref[...]).astype(BF16)
    vo_ref[0] = _compress_core(vc_ref[0], w1v_ref[...], pev_ref[...], w2v_ref[...]).astype(BF16)


def _compress_prompt(kc, vc, wts):
    b, n_sub, w = kc.shape
    full = lambda a: pl.BlockSpec(a.shape, lambda i: (0,) * a.ndim)
    consts = [wts["w1k"], wts["w1v"], wts["pek"], wts["pev"], wts["w2k"], wts["w2v"], wts["kg"], wts["bk"]]
    blk = pl.BlockSpec((1, n_sub, w), lambda i: (i, 0, 0))
    oblk = pl.BlockSpec((1, n_sub, 128), lambda i: (i, 0, 0))
    return pl.pallas_call(
        _compress_prompt_kernel,
        grid=(b,),
        in_specs=[blk, blk] + [full(a) for a in consts],
        out_specs=[oblk, oblk],
        out_shape=[jax.ShapeDtypeStruct((b, n_sub, 128), BF16)] * 2,
        compiler_params=_params(("parallel",)),
        name="compress_prompt",
    )(kc, vc, *consts)


def _page_copy(cache_ref, page, buf_ref, slot, p, sem_ref, lane_major):
    if lane_major:
        dst = buf_ref.at[slot, :, pl.ds(pl.multiple_of(p * PAGE_SIZE, PAGE_SIZE), PAGE_SIZE)]
    else:
        dst = buf_ref.at[slot, p]
    return pltpu.make_async_copy(cache_ref.at[page], dst, sem_ref.at[slot])


def _gather_start(pt_ref, b, slot, caches, bufs, sems, n_pages, lane_major):
    def body(p, carry):
        page = pt_ref[b, p]
        for cache_ref, buf_ref, sem_ref in zip(caches, bufs, sems):
            _page_copy(cache_ref, page, buf_ref, slot, p, sem_ref, lane_major).start()
        return carry
    lax.fori_loop(0, n_pages, body, 0)


def _gather_wait(slot, caches, bufs, sems, n_pages, lane_major):
    def body(p, carry):
        for cache_ref, buf_ref, sem_ref in zip(caches, bufs, sems):
            _page_copy(cache_ref, 0, buf_ref, slot, p, sem_ref, lane_major).wait()
        return carry
    lax.fori_loop(0, n_pages, body, 0)


def _gather_pipeline(pt_ref, caches, bufs, sems, n_pages, lane_major):
    b = pl.program_id(0)
    nb = pl.num_programs(0)
    slot = b % 2

    @pl.when(b == 0)
    def _():
        _gather_start(pt_ref, 0, 0, caches, bufs, sems, n_pages, lane_major)

    @pl.when(b + 1 < nb)
    def _():
        _gather_start(pt_ref, b + 1, 1 - slot, caches, bufs, sems, n_pages, lane_major)

    _gather_wait(slot, caches, bufs, sems, n_pages, lane_major)
    return slot


def _compress_paged(buf, slot, rows_s, perm, w1, pe, w2, n_pages):
    groups = PAGE_SIZE // CMP_STRIDE

    def body(i, carry):
        pair = buf[slot, pl.ds(2 * i, 2)].reshape(2 * D_KV, PAGE_SIZE)
        rows = _dot_nt(perm, pair.astype(BF16))
        for half in range(2):
            start = pl.multiple_of((2 * i + half) * groups, groups)
            for s in range(CMP_STRIDE):
                rows_s[s // 2, pl.ds(start, groups), (s % 2) * D_KV:(s % 2 + 1) * D_KV] = (
                    rows[s * groups:(s + 1) * groups, half * D_KV:(half + 1) * D_KV])
        return carry
    lax.fori_loop(0, n_pages // 2, body, 0, unroll=4)
    u = None
    for j in range(CMP_STRIDE // 2):
        t = _dot(rows_s[j].astype(BF16), w1[2 * j * D_KV:(2 * j + 2) * D_KV, :])
        u = t if u is None else u + t
    return _compress_tail(u, w1, pe, w2)


def _compress_sample_kernel(pt_ref, ck_ref, cv_ref, perm_ref, w1k_ref, w1v_ref, pek_ref, pev_ref, w2k_ref, w2v_ref,
                            kg_ref, bk_ref, ko_ref, vo_ref, kbuf, vbuf, rows_s, ksem, vsem, *, n_pages):
    slot = _gather_pipeline(pt_ref, (ck_ref, cv_ref), (kbuf, vbuf), (ksem, vsem), n_pages, lane_major=False)
    perm = perm_ref[...]
    kc = _compress_paged(kbuf, slot, rows_s, perm, w1k_ref[...], pek_ref[...], w2k_ref[...], n_pages)
    ko_ref[0] = _knorm(kc, bk_ref[...], kg_ref[...]).astype(BF16)
    vo_ref[0] = _compress_paged(vbuf, slot, rows_s, perm, w1v_ref[...], pev_ref[...], w2v_ref[...], n_pages).astype(BF16)


def _compress_sample(page_table, ck, cv, wts):
    bs, n_pages = page_table.shape
    n_sub = n_pages * PAGE_SIZE // CMP_STRIDE
    full = lambda a: pl.BlockSpec(a.shape, lambda i, pt: (0,) * a.ndim)
    groups = PAGE_SIZE // CMP_STRIDE
    r = np.arange(PAGE_SIZE)
    perm = jnp.asarray((np.arange(PAGE_SIZE)[None, :] == (CMP_STRIDE * (r % groups) + r // groups)[:, None])
                       .astype(np.float32), BF16)
    consts = [perm, wts["w1k"], wts["w1v"], wts["pek"], wts["pev"], wts["w2k"], wts["w2v"], wts["kg"], wts["bk"]]
    anyspec = pl.BlockSpec(memory_space=pl.ANY)
    oblk = pl.BlockSpec((1, n_sub, 128), lambda i, pt: (i, 0, 0))
    pages = pltpu.VMEM((2, n_pages, D_KV, PAGE_SIZE), F32)
    return pl.pallas_call(
        functools.partial(_compress_sample_kernel, n_pages=n_pages),
        grid_spec=pltpu.PrefetchScalarGridSpec(
            num_scalar_prefetch=1,
            grid=(bs,),
            in_specs=[anyspec, anyspec] + [full(a) for a in consts],
            out_specs=[oblk, oblk],
            scratch_shapes=[pages, pages, pltpu.VMEM((CMP_STRIDE // 2, n_sub, 2 * D_KV), F32),
                            pltpu.SemaphoreType.DMA((2,)), pltpu.SemaphoreType.DMA((2,))]),
        out_shape=[jax.ShapeDtypeStruct((bs, n_sub, 128), BF16)] * 2,
        compiler_params=_params(("arbitrary",)),
        name="compress_sample",
    )(page_table, ck, cv, *consts)


def _group_queries(q, g, tq):
    lane = lax.broadcasted_iota(jnp.int32, (tq, LANES), 1)
    mine = (lane >= g * HEAD_DIM) & (lane < (g + 1) * HEAD_DIM)
    zero = jnp.zeros((tq, LANES), q.dtype)
    return jnp.concatenate([jnp.where(mine, q[:, r * 128:(r + 1) * 128], zero) for r in range(ATT_REP)], axis=0)


def _pack_heads(o, g, tq):
    lane = lax.broadcasted_iota(jnp.int32, (tq, LANES), 1)
    first = g == 0
    chunks = []
    for k in range(ATT_REP // 2):
        a = o[2 * k * tq:(2 * k + 1) * tq]
        b = o[(2 * k + 1) * tq:(2 * k + 2) * tq]
        lo = jnp.where(first, a, pltpu.roll(a, HEAD_DIM, 1))
        hi = jnp.where(first, pltpu.roll(b, HEAD_DIM, 1), b)
        chunks.append(jnp.where(lane < HEAD_DIM, lo, hi))
    return jnp.concatenate(chunks, axis=1)


def _cmp_select_kernel(q_ref, kc_ref, vc_ref, tab_ref, msel_ref, ovr_ref, o_ref, pen_ref, *, tq, n_sel, k_top):
    q = q_ref[0]
    kc = kc_ref[0]
    vc = vc_ref[0]
    ovr = ovr_ref[0]
    lane = lax.broadcasted_iota(jnp.int32, (tq, LANES), 1)
    outs = []
    for g in range(ATT_KV_HEADS):
        tab = tab_ref[g, 0]
        s = _dot_nt(_group_queries(q, g, tq), kc) + tab
        m = jnp.max(s, axis=-1, keepdims=True)
        e = jnp.where(tab > 0.5 * NEG, jnp.exp(s - m), 0.0)
        p = e / jnp.maximum(jnp.sum(e, axis=-1, keepdims=True), 1e-30)
        outs.append(_dot(p.astype(BF16), vc))
        imp = p[0:tq]
        for r in range(1, ATT_REP):
            imp = imp + p[r * tq:(r + 1) * tq]
        score = _split_dot(imp, msel_ref[...], 3)
        score = jnp.where(ovr == 0.0, score, ovr)
        rank = jnp.zeros((tq, LANES), F32)
        for j in range(n_sel):
            col = score[:, j:j + 1]
            beats = (col > score) | ((col == score) & (lane > j))
            rank = rank + jnp.where(beats, 1.0, 0.0)
        pen_ref[0, :, g * 128:(g + 1) * 128] = jnp.where(rank < k_top, 0.0, NEG).astype(BF16)
    for g in range(ATT_KV_HEADS):
        o_ref[0, :, g * 256:(g + 1) * 256] = _pack_heads(outs[g], g, tq)


def _cmp_select(q, kc, vc, tab, msel, ovr, *, tq, n_sel, k_top):
    b, t, _ = q.shape
    nb = kc.shape[1]
    n_qt = t // tq
    return pl.pallas_call(
        functools.partial(_cmp_select_kernel, tq=tq, n_sel=n_sel, k_top=k_top),
        grid=(n_qt, b),
        in_specs=[pl.BlockSpec((1, tq, 512), lambda qi, bi: (bi, qi, 0)),
                  pl.BlockSpec((1, nb, 128), lambda qi, bi: (bi, 0, 0)),
                  pl.BlockSpec((1, nb, 128), lambda qi, bi: (bi, 0, 0)),
                  pl.BlockSpec((ATT_KV_HEADS, 1, ATT_REP * tq, nb), lambda qi, bi: (0, qi, 0, 0)),
                  pl.BlockSpec(msel.shape, lambda qi, bi: (0, 0)),
                  pl.BlockSpec((1, tq, 128), lambda qi, bi: (qi, 0, 0))],
        out_specs=[pl.BlockSpec((1, tq, 512), lambda qi, bi: (bi, qi, 0)),
                   pl.BlockSpec((1, tq, 256), lambda qi, bi: (bi, qi, 0))],
        out_shape=[jax.ShapeDtypeStruct((b, t, 512), F32), jax.ShapeDtypeStruct((b, t, 256), BF16)],
        compiler_params=_params(("parallel", "parallel")),
        name="cmp_select",
    )(q, kc, vc, tab, msel, ovr)


def _cmp_select_t_kernel(qt_ref, kc_ref, vc_ref, tab_ref, msel_ref, ovr_ref, o_ref, pen_ref,
                         *, tq, n_sel, k_top):
    qt = qt_ref[...]
    kc = kc_ref[0]
    vct = vc_ref[0].astype(F32).T.astype(BF16)
    ovr = ovr_ref[0, 0:n_sel, :]
    row = lax.broadcasted_iota(jnp.int32, (n_sel, tq), 0)
    for g in range(ATT_KV_HEADS):
        tab = tab_ref[g, 0]
        s = _dot(kc, _group_queries_t(qt, g, tq)) + tab
        m = jnp.max(s, axis=0, keepdims=True)
        e = jnp.where(tab > 0.5 * NEG, jnp.exp(s - m), 0.0)
        p = e / jnp.maximum(jnp.sum(e, axis=0, keepdims=True), 1e-30)
        o_ref[0, :, g * 256:(g + 1) * 256] = _unpack_heads_t(_dot(vct, p.astype(BF16)), g, tq)
        imp = p[:, 0:tq]
        for r in range(1, ATT_REP):
            imp = imp + p[:, r * tq:(r + 1) * tq]
        score = _split_dot_left(msel_ref[...], imp)[0:n_sel, :]
        score = jnp.where(ovr == 0.0, score, ovr)
        rank = jnp.zeros((n_sel, tq), F32)
        for j in range(n_sel):
            cand = score[j:j + 1, :]
            beats = (cand > score) | ((cand == score) & (row > j))
            rank = rank + jnp.where(beats, 1.0, 0.0)
        pen_ref[0, g, 0:n_sel, :] = jnp.where(rank < k_top, 0.0, NEG).astype(BF16)
        pen_ref[0, g, n_sel:LANES, :] = jnp.zeros((LANES - n_sel, tq), BF16)


def _cmp_select_t(qt, kc, vc, tab, msel_t, ovr_t, *, tq, n_sel, k_top):
    b, nb, _ = kc.shape
    n_qt = tab.shape[1]
    t = n_qt * tq
    return pl.pallas_call(
        functools.partial(_cmp_select_t_kernel, tq=tq, n_sel=n_sel, k_top=k_top),
        grid=(n_qt, b),
        in_specs=[pl.BlockSpec((512, tq), lambda qi, bi: (0, bi * n_qt + qi)),
                  pl.BlockSpec((1, nb, 128), lambda qi, bi: (bi, 0, 0)),
                  pl.BlockSpec((1, nb, 128), lambda qi, bi: (bi, 0, 0)),
                  pl.BlockSpec((ATT_KV_HEADS, 1, nb, ATT_REP * tq), lambda qi, bi: (0, qi, 0, 0)),
                  pl.BlockSpec(msel_t.shape, lambda qi, bi: (0, 0)),
                  pl.BlockSpec((1, 128, tq), lambda qi, bi: (qi, 0, 0))],
        out_specs=[pl.BlockSpec((1, tq, 512), lambda qi, bi: (bi, qi, 0)),
                   pl.BlockSpec((1, ATT_KV_HEADS, 128, tq), lambda qi, bi: (bi, 0, 0, qi))],
        out_shape=[jax.ShapeDtypeStruct((b, t, 512), F32), jax.ShapeDtypeStruct((b, ATT_KV_HEADS, 128, t), BF16)],
        compiler_params=_params(("parallel", "parallel")),
        name="cmp_select_t",
    )(qt, kc, vc, tab, msel_t, ovr_t)


SUM_ROWS = 16


def _flash_init(m_ref, acc_ref):
    m_ref[...] = jnp.full(m_ref.shape, NEG, F32)
    acc_ref[...] = jnp.zeros(acc_ref.shape, F32)


def _flash_tile_t(k, qa_ref, q_rows, v_t, bias, m_ref, acc_ref):
    cols = slice(0, qa_ref.shape[1])
    s = _dot(k, qa_ref[0:q_rows, cols])
    if bias is not None:
        s = s + bias(cols)
    m_old = m_ref[...]
    m_new = jnp.maximum(m_old, jnp.max(s, axis=0, keepdims=True))
    alpha = jnp.exp(m_old - m_new)
    p = jnp.exp((s - m_new).astype(BF16))
    v_aug = jnp.concatenate([v_t, jnp.ones((SUM_ROWS, v_t.shape[1]), BF16)], axis=0)
    acc_ref[...] = alpha * acc_ref[...] + _dot(v_aug, p)
    m_ref[...] = m_new


def _flash_result(acc_ref, g):
    return acc_ref[g, 0:128] / acc_ref[g, 128:129]


def _group_queries_t(qt, g, tq):
    row = lax.broadcasted_iota(jnp.int32, (LANES, tq), 0)
    mine = (row >= g * HEAD_DIM) & (row < (g + 1) * HEAD_DIM)
    zero = jnp.zeros((LANES, tq), qt.dtype)
    return jnp.concatenate([jnp.where(mine, qt[r * 128:(r + 1) * 128, :], zero) for r in range(ATT_REP)], axis=1)


def _unpack_heads_t(o_t, g, tq):
    rows = jnp.concatenate([o_t[:, r * tq:(r + 1) * tq].T for r in range(ATT_REP)], axis=0)
    return _pack_heads(rows, g, tq)


def _gate_chunks(gd, gexp):
    sig = 1.0 / (1.0 + jnp.exp(-gd))
    return _split_dot(sig, gexp, 3)


def _combine(gx, o_cmp, o_sel, o_win):
    return gx[:, 0:256] * o_cmp + gx[:, 256:512] * o_sel + gx[:, 512:768] * o_win


def _prompt_attn_kernel(qt_ref, pen_ref, oc_ref, gd_ref, ksa_ref, vst_ref, kw_ref, vwt_ref, a_ref, gexp_ref,
                        o_ref, m_ref, acc_ref, qa_s, *, tq):
    qi = pl.program_id(1)
    groups = range(ATT_KV_HEADS)
    for g in groups:
        qa_s[g, 0:128, :] = _group_queries_t(qt_ref[...], g, tq)
        qa_s[g, 128:256, :] = jnp.concatenate([pen_ref[0, g]] * ATT_REP, axis=1)

    def sel_tile(start, n_keys, bias):
        k = ksa_ref[0, pl.ds(start, n_keys), :]
        v = vst_ref[:, pl.ds(start, n_keys)]
        for g in groups:
            _flash_tile_t(k, qa_s.at[g], 256, v, None if bias is None else functools.partial(bias, g),
                          m_ref.at[g], acc_ref.at[g])

    def win_tile(start, n_keys, bias):
        k = kw_ref[0, pl.ds(start, n_keys), :]
        v = vwt_ref[:, pl.ds(start, n_keys)]
        for g in groups:
            _flash_tile_t(k, qa_s.at[g], 128, v, functools.partial(bias, g), m_ref.at[g], acc_ref.at[g])

    def near(tile_fn, max_tiles):
        n_kinds = a_ref.shape[1]
        for n in range(1, max_tiles + 1):
            hit = (qi + 1 == n) if n < max_tiles else (qi + 1 >= n)

            def body(i, carry, n=n):
                bias = lambda g, cols: a_ref[g, n_kinds - n:n_kinds, :, cols].reshape(n * tq, cols.stop - cols.start)
                tile_fn(pl.multiple_of((qi + 1 - n) * tq, tq), n * tq, bias)
                return carry
            lax.fori_loop(0, hit.astype(jnp.int32), body, 0)

    _flash_init(m_ref, acc_ref)
    near(sel_tile, 2)
    n_far = jnp.maximum(qi - 1, 0)

    def far_pair(i, carry):
        sel_tile(pl.multiple_of(2 * i * tq, tq), 2 * tq, None)
        return carry
    lax.fori_loop(0, n_far // 2, far_pair, 0)

    def far_last(i, carry):
        sel_tile(pl.multiple_of((n_far - 1) * tq, tq), tq, None)
        return carry
    lax.fori_loop(0, n_far % 2, far_last, 0)
    o_sel = [_unpack_heads_t(_flash_result(acc_ref, g), g, tq) for g in groups]

    _flash_init(m_ref, acc_ref)
    near(win_tile, 3)
    for g in groups:
        o_win = _unpack_heads_t(_flash_result(acc_ref, g), g, tq)
        gx = _gate_chunks(gd_ref[0], gexp_ref[g])
        o_ref[0, :, g * 256:(g + 1) * 256] = _combine(gx, oc_ref[0, :, g * 256:(g + 1) * 256], o_sel[g], o_win).astype(BF16)


def _prompt_attn(qt, pen_t, o_cmp, gd, ksa, vst, kwb, vwt, atab, gexp, *, tq):
    b, t, _ = ksa.shape
    n_qt = t // tq
    nq = ATT_REP * tq
    qblk = lambda w: pl.BlockSpec((1, tq, w), lambda bi, qi: (bi, qi, 0))
    seq = lambda w: pl.BlockSpec((1, t, w), lambda bi, qi: (bi, 0, 0))
    seq_t = pl.BlockSpec((128, t), lambda bi, qi: (0, bi))
    full = lambda a: pl.BlockSpec(a.shape, lambda bi, qi: (0,) * a.ndim)
    return pl.pallas_call(
        functools.partial(_prompt_attn_kernel, tq=tq),
        grid=(b, n_qt),
        in_specs=[pl.BlockSpec((512, tq), lambda bi, qi: (0, bi * n_qt + qi)),
                  pl.BlockSpec((1, ATT_KV_HEADS, 128, tq), lambda bi, qi: (bi, 0, 0, qi)),
                  qblk(512), qblk(128), seq(256), seq_t, seq(128), seq_t, full(atab), full(gexp)],
        out_specs=qblk(512),
        out_shape=jax.ShapeDtypeStruct((b, t, 512), BF16),
        scratch_shapes=[pltpu.VMEM((ATT_KV_HEADS, 1, nq), F32),
                        pltpu.VMEM((ATT_KV_HEADS, 128 + SUM_ROWS, nq), F32), pltpu.VMEM((ATT_KV_HEADS, 256, nq), BF16)],
        compiler_params=_params(("parallel", "arbitrary")),
        name="prompt_attn",
    )(qt, pen_t, o_cmp, gd, ksa, vst, kwb, vwt, atab, gexp)


def _softmax_two(s_a, s_b, vt_a, v_b):
    m = jnp.maximum(jnp.max(s_a, axis=-1, keepdims=True), jnp.max(s_b, axis=-1, keepdims=True))
    p_a = jnp.exp(s_a - m)
    p_b = jnp.exp(s_b - m)
    l = jnp.sum(p_a, axis=-1, keepdims=True) + jnp.sum(p_b, axis=-1, keepdims=True)
    return (_dot_nt(p_a.astype(BF16), vt_a) + _dot(p_b.astype(BF16), v_b)) / l


def _slide_window(buf_t, new_t, n_new):
    w = buf_t.shape[1]
    rolled = pltpu.roll(buf_t, w - n_new, 1)
    lane = lax.broadcasted_iota(jnp.int32, (LANES, LANES), 1)
    last = jnp.where(lane >= LANES - n_new, new_t, rolled[:, w - LANES:w])
    return jnp.concatenate([rolled[:, 0:w - LANES], last], axis=1)


def _sample_attn_kernel(pt_ref, q_ref, pen_ref, oc_ref, gd_ref, ck_ref, cv_ref, ksn_ref, vsn_ref,
                        cw_ref, cvw_ref, kwn_ref, vwn_ref, kwt_ref, vwt_ref, eall_ref, tsel_ref, tnew_ref,
                        twin_ref, gexp_ref, o_ref, kwo_ref, vwo_ref, kbuf, vbuf, ke_s, ksem, vsem,
                        *, n_pages, n_new):
    slot = _gather_pipeline(pt_ref, (ck_ref, cv_ref), (kbuf, vbuf), (ksem, vsem), n_pages, lane_major=True)
    tq = TQS
    rows = ATT_REP * tq

    @pl.when(pl.program_id(0) == 0)
    def _():
        ke_s[128:256, :] = eall_ref[...]

    ke_s[0:128, :] = kbuf[slot].astype(BF16)
    q = q_ref[0]
    pen = pen_ref[0]
    qa = jnp.concatenate(
        [jnp.concatenate([_group_queries(q, g, tq), jnp.concatenate([pen[:, g * 128:(g + 1) * 128]] * ATT_REP, axis=0)],
                         axis=1) for g in range(ATT_KV_HEADS)], axis=0)
    qg = qa[:, 0:128]
    stack = lambda t_ref: t_ref[:, 0].reshape(ATT_KV_HEADS * rows, t_ref.shape[-1])
    tnew = stack(tnew_ref)
    s_past = _dot(qa, ke_s[...]) + stack(tsel_ref)
    s_new = _dot_nt(qg, ksn_ref[0]) + tnew
    o_sel = _softmax_two(s_past, s_new, vbuf[slot].astype(BF16), vsn_ref[0])
    s_buf = _dot(qg, cw_ref[0].astype(BF16)) + stack(twin_ref)
    s_new = _dot_nt(qg, kwn_ref[0]) + tnew
    o_win = _softmax_two(s_buf, s_new, cvw_ref[0].astype(BF16), vwn_ref[0])
    for g in range(ATT_KV_HEADS):
        gx = _gate_chunks(gd_ref[0], gexp_ref[g])
        part = slice(g * rows, (g + 1) * rows)
        o = _combine(gx, oc_ref[0, :, g * 256:(g + 1) * 256], _pack_heads(o_sel[part], g, tq), _pack_heads(o_win[part], g, tq))
        o_ref[0, :, g * 256:(g + 1) * 256] = o.astype(BF16)
    kwo_ref[0] = _slide_window(cw_ref[0], kwt_ref[0], n_new)
    vwo_ref[0] = _slide_window(cvw_ref[0], vwt_ref[0], n_new)


def _sample_attn(page_table, q, pen, o_cmp, gd, ck, cv, ksn, vsn, cw, cvw, kwn, vwn, kwt, vwt,
                 eall, tsel, tnew, twin, gexp, *, n_new):
    bs, n_pages = page_table.shape
    past = n_pages * PAGE_SIZE
    per_b = lambda a: pl.BlockSpec((1,) + a.shape[1:], lambda i, pt: (i,) + (0,) * (a.ndim - 1))
    full = lambda a: pl.BlockSpec(a.shape, lambda i, pt: (0,) * a.ndim)
    anyspec = pl.BlockSpec(memory_space=pl.ANY)
    ins = [q, pen, o_cmp, gd, ck, cv, ksn, vsn, cw, cvw, kwn, vwn, kwt, vwt, eall, tsel, tnew, twin, gexp]
    specs = [per_b(q), per_b(pen), per_b(o_cmp), per_b(gd), anyspec, anyspec, per_b(ksn), per_b(vsn),
             per_b(cw), per_b(cvw), per_b(kwn), per_b(vwn), per_b(kwt), per_b(vwt),
             full(eall), full(tsel), full(tnew), full(twin), full(gexp)]
    win = pl.BlockSpec((1, 128, WINDOW), lambda i, pt: (i, 0, 0))
    return pl.pallas_call(
        functools.partial(_sample_attn_kernel, n_pages=n_pages, n_new=n_new),
        grid_spec=pltpu.PrefetchScalarGridSpec(
            num_scalar_prefetch=1,
            grid=(bs,),
            in_specs=specs,
            out_specs=[pl.BlockSpec((1, TQS, 512), lambda i, pt: (i, 0, 0)), win, win],
            scratch_shapes=[pltpu.VMEM((2, 128, past), F32), pltpu.VMEM((2, 128, past), F32),
                            pltpu.VMEM((256, past), BF16),
                            pltpu.SemaphoreType.DMA((2,)), pltpu.SemaphoreType.DMA((2,))]),
        out_shape=[jax.ShapeDtypeStruct((bs, TQS, 512), BF16),
                   jax.ShapeDtypeStruct((bs, 128, WINDOW), F32), jax.ShapeDtypeStruct((bs, 128, WINDOW), F32)],
        compiler_params=_params(("arbitrary",)),
        name="sample_attn",
    )(page_table, *ins)


def _lane_pair(cols, h0, h1, rows):
    lane = lax.broadcasted_iota(jnp.int32, (rows, LANES), 1)
    a = jnp.broadcast_to(cols[:, h0:h0 + 1], (rows, LANES))
    b = jnp.broadcast_to(cols[:, h1:h1 + 1], (rows, LANES))
    return jnp.where(lane < HEAD_DIM, a, b)


def _ssd_kernel(xbc_ref, z_ref, gd_ref, carry_ref, h0_ref, cw_ref, cb_ref, dtb_ref, arow_ref, dskip_ref,
                gnorm_ref, ltri_ref, y_ref, hout_ref, h_s, xfull, *, rows, n_valid):
    c = pl.program_id(1)
    halo = 8

    @pl.when(c == 0)
    def _():
        h_s[...] = h0_ref[0]
        xfull[0:halo, :] = carry_ref[0]

    xfull[halo:halo + rows, :] = xbc_ref[0]
    conv = cb_ref[...]
    for k in range(SSD_CONV):
        conv = conv + xfull[pl.ds(halo - (SSD_CONV - 1) + k, rows), :] * cw_ref[k:k + 1, :]
    tail = xfull[rows:rows + halo, :]
    xfull[0:halo, :] = tail
    xc = _silu(conv)
    xs = xc[:, 0:D_SSD]
    bm = xc[:, D_SSD:D_SSD + 128]
    cm = xc[:, D_SSD + 128:D_SSD + 256]

    lane = lax.broadcasted_iota(jnp.int32, (rows, LANES), 1)
    rowi = lax.broadcasted_iota(jnp.int32, (rows, LANES), 0)
    t = gd_ref[0] + dtb_ref[...]
    sp = jnp.maximum(t, 0.0) + jnp.log(1.0 + jnp.exp(-jnp.abs(t)))
    dt = jnp.where((lane >= DT_LANE) & (lane < DT_LANE + SSD_HEADS) & (rowi < n_valid), sp, 0.0)
    a = dt * arow_ref[...]
    acum = _split_dot_left(ltri_ref[...], a)
    acum_t = acum.T
    a_last = acum[rows - 1:rows, :]
    to_end = jnp.exp(a_last - acum)
    eac = jnp.exp(acum)
    dec = jnp.exp(a_last)

    li = lax.broadcasted_iota(jnp.int32, (rows, rows), 0)
    si = lax.broadcasted_iota(jnp.int32, (rows, rows), 1)
    causal = li >= si
    bmb = bm.astype(BF16)
    rowp = lax.broadcasted_iota(jnp.int32, (LANES, LANES), 0)
    cbs = []
    cmask = []
    for g in range(2):
        cg = jnp.where((lane >= g * 64) & (lane < (g + 1) * 64), cm, 0.0).astype(BF16)
        cmask.append(cg)
        cbs.append(_dot_nt(cg, bmb))
    for k in range(SSD_HEADS // 2):
        g = k // 2
        h0, h1 = DT_LANE + 2 * k, DT_LANE + 2 * k + 1
        xs_p = xs[:, k * 128:(k + 1) * 128]
        xdt = xs_p * _lane_pair(dt, h0, h1, rows)
        xdt_b = xdt.astype(BF16)
        ys = []
        for h in (h0, h1):
            seg = jnp.broadcast_to(acum[:, h:h + 1], (rows, rows)) - acum_t[h:h + 1, :]
            decay = jnp.where(causal, jnp.exp(jnp.where(causal, seg, 0.0)), 0.0)
            ys.append(_dot((cbs[g] * decay).astype(BF16), xdt_b))
        y = jnp.where(lane < HEAD_DIM, ys[0], ys[1])
        hp = h_s[k * 128:(k + 1) * 128, :]
        y = y + _dot_nt(cmask[g], hp.astype(BF16)) * _lane_pair(eac, h0, h1, rows)
        y = y + dskip_ref[:, k * 128:(k + 1) * 128] * xs_p
        xw = xdt * _lane_pair(to_end, h0, h1, rows)
        st = _dot(xw.T.astype(BF16), bmb)
        dfac = jnp.where(rowp < HEAD_DIM, dec[:, h0:h0 + 1], dec[:, h1:h1 + 1])
        h_s[k * 128:(k + 1) * 128, :] = hp * dfac + st
        xfull_y = y * _silu(z_ref[0, :, k * 128:(k + 1) * 128])
        y_ref[0, :, k * 128:(k + 1) * 128] = xfull_y.astype(y_ref.dtype)

    for g in range(2):
        yg = y_ref[0, :, g * 256:(g + 1) * 256].astype(F32)
        ms = jnp.mean(yg * yg, axis=-1, keepdims=True)
        y_ref[0, :, g * 256:(g + 1) * 256] = (yg * lax.rsqrt(ms + RMS_EPS)
                                               * gnorm_ref[:, g * 256:(g + 1) * 256]).astype(y_ref.dtype)

    @pl.when(c == pl.num_programs(1) - 1)
    def _():
        half = D_SSD // 2
        hout_ref[0, 0:half, :] = h_s[0:half, 0:SSD_STATE]
        hout_ref[0, half:D_SSD, :] = h_s[half:D_SSD, SSD_STATE:2 * SSD_STATE]


def _split_dot_left(tri, a):
    acc = None
    rem = a
    for _ in range(3):
        piece = rem.astype(BF16)
        rem = rem - piece.astype(F32)
        t = _dot(tri, piece)
        acc = t if acc is None else acc + t
    return acc


def _ssd(xbc, z, gd, carry, h0, wts, *, n_valid):
    b, l, _ = xbc.shape
    rows = SSD_CHUNK
    nc = l // rows
    blk = lambda w: pl.BlockSpec((1, rows, w), lambda bi, ci: (bi, ci, 0))
    per_b = lambda a: pl.BlockSpec((1,) + a.shape[1:], lambda bi, ci: (bi,) + (0,) * (a.ndim - 1))
    full = lambda a: pl.BlockSpec(a.shape, lambda bi, ci: (0,) * a.ndim)
    consts = [wts["conv_w"], wts["conv_b"], wts["dtb"], wts["arow"], wts["dskip"], wts["gnorm"], wts["ltri"]]
    return pl.pallas_call(
        functools.partial(_ssd_kernel, rows=rows, n_valid=n_valid),
        grid=(b, nc),
        in_specs=[blk(768), blk(512), blk(128), per_b(carry), per_b(h0)] + [full(a) for a in consts],
        out_specs=[blk(512), pl.BlockSpec((1, D_SSD, SSD_STATE), lambda bi, ci: (bi, 0, 0))],
        out_shape=[jax.ShapeDtypeStruct((b, l, 512), F32), jax.ShapeDtypeStruct((b, D_SSD, SSD_STATE), F32)],
        scratch_shapes=[pltpu.VMEM((512, 128), F32), pltpu.VMEM((rows + 8, 768), F32)],
        compiler_params=_params(("parallel", "arbitrary")),
        name="ssd",
    )(xbc, z, gd, carry, h0, *consts)


def _finish_kernel(x_ref, oa_ref, ys_ref, gf_ref, woa_hbm, wos_hbm, wg_hbm, wu_hbm, wd_hbm, y_ref,
                   woa_ref, wos_ref, wg_ref, wu_ref, wd_ref, sem):
    @pl.when(pl.program_id(0) == 0)
    def _():
        copies = [pltpu.make_async_copy(src, dst, sem.at[i]) for i, (src, dst) in enumerate(
            ((woa_hbm, woa_ref), (wos_hbm, wos_ref), (wg_hbm, wg_ref), (wu_hbm, wu_ref), (wd_hbm, wd_ref)))]
        for c in copies:
            c.start()
        for c in copies:
            c.wait()

    h = x_ref[...] + _dot(oa_ref[...], woa_ref[...]) + _dot(ys_ref[...].astype(BF16), wos_ref[...])
    ms = jnp.mean(h * h, axis=-1, keepdims=True)
    u = (h * lax.rsqrt(ms + RMS_EPS) * gf_ref[...]).astype(BF16)
    act = _silu(_dot(u, wg_ref[...])) * _dot(u, wu_ref[...])
    y_ref[...] = h + _dot(act.astype(BF16), wd_ref[...])


def _finish(x2d, o_att, y_ssd, wts):
    n = x2d.shape[0]
    tm = min(512, n)
    row = lambda w: pl.BlockSpec((tm, w), lambda i: (i, 0))
    weights = [wts["wo_att"], wts["wo_ssd"], wts["w_gate"], wts["w_up"], wts["w_down"]]
    anyspec = pl.BlockSpec(memory_space=pl.ANY)
    return pl.pallas_call(
        _finish_kernel,
        grid=(n // tm,),
        in_specs=[row(D_MODEL), row(512), row(512), pl.BlockSpec((1, D_MODEL), lambda i: (0, 0))]
        + [anyspec] * len(weights),
        out_specs=row(D_MODEL),
        out_shape=jax.ShapeDtypeStruct((n, D_MODEL), F32),
        scratch_shapes=[pltpu.VMEM(w.shape, w.dtype) for w in weights] + [pltpu.SemaphoreType.DMA((len(weights),))],
        compiler_params=_params(("arbitrary",)),
        name="finish",
    )(x2d, o_att, y_ssd, wts["gffn"], *weights)


def _pair_perm():
    cols = []
    for r in range(ATT_REP):
        cols += list(range(r * HEAD_DIM, (r + 1) * HEAD_DIM))
        cols += list(range((ATT_REP + r) * HEAD_DIM, (ATT_REP + r + 1) * HEAD_DIM))
    return np.asarray(cols, np.int32)


def _block_ones(n, blk):
    i = np.arange(n)
    return (i[:, None] // blk == i[None, :] // blk).astype(np.float32) / blk


def _prep_weights(norm_mix, w_in, q_norm, k_norm, cmp_pe, cmp_w1, cmp_w2, conv_w, conv_b, dt_bias, a_log,
                  d_skip, ssd_norm, w_out, norm_ffn, w_gate, w_up, w_down):
    perm = _pair_perm()
    w = w_in
    gd = jnp.concatenate([w[:, OFF_GATE:OFF_Z], w[:, OFF_DT:D_IN],
                          jnp.zeros((D_MODEL, 128 - 3 * ATT_HEADS - SSD_HEADS), w.dtype)], axis=1)
    w_r = jnp.concatenate([w[:, :D_ATT][:, perm], w[:, OFF_KV:OFF_GATE], w[:, OFF_Z:OFF_XBC],
                           w[:, OFF_XBC:OFF_DT], gd], axis=1).astype(BF16)
    wts = dict(
        gmix=norm_mix.reshape(1, D_MODEL), w_in=w_r,
        qg=(jnp.tile(q_norm, ATT_HEADS) * (HEAD_DIM ** -0.5)).reshape(1, D_ATT),
        kg=jnp.tile(k_norm, ATT_KV_HEADS).reshape(1, D_KV),
        bq=jnp.asarray(_block_ones(D_ATT, HEAD_DIM), BF16), bk=jnp.asarray(_block_ones(D_KV, HEAD_DIM), BF16))

    def w1_big(w1):
        w1r = w1.reshape(2, CMP_STRIDE, HEAD_DIM, CMP_HIDDEN)
        eye = jnp.eye(ATT_KV_HEADS, dtype=w1.dtype)
        big = jnp.einsum("jsdh,gk->sgdjkh", w1r, eye)
        return big.reshape(CMP_STRIDE * D_KV, 2 * D_KV).astype(BF16)

    def pe_rows(pe):
        per = jnp.broadcast_to(pe.reshape(2, CMP_STRIDE, 1, HEAD_DIM), (2, CMP_STRIDE, ATT_KV_HEADS, HEAD_DIM))
        per = per.reshape(2, CMP_STRIDE * D_KV)
        return jnp.concatenate([per, jnp.zeros((6, CMP_STRIDE * D_KV), pe.dtype)], axis=0).astype(BF16)

    def w2_big(w2):
        eye = jnp.eye(ATT_KV_HEADS, dtype=w2.dtype)
        return jnp.einsum("hd,gk->ghkd", w2, eye).reshape(D_KV, D_KV).astype(BF16)

    wts.update(w1k=w1_big(cmp_w1[0]), w1v=w1_big(cmp_w1[1]), pek=pe_rows(cmp_pe[0]), pev=pe_rows(cmp_pe[1]),
               w2k=w2_big(cmp_w2[0]), w2v=w2_big(cmp_w2[1]))

    pad_lanes = lambda v: jnp.zeros((1, LANES), F32).at[0, DT_LANE:DT_LANE + SSD_HEADS].set(v)
    ltri = np.tril(np.ones((SSD_CHUNK, SSD_CHUNK), np.float32))
    wts.update(conv_w=jnp.concatenate([conv_w, jnp.zeros((4, D_CONV), F32)], axis=0), conv_b=conv_b.reshape(1, D_CONV),
               dtb=pad_lanes(dt_bias), arow=pad_lanes(-jnp.exp(a_log)),
               dskip=jnp.repeat(d_skip, 64).reshape(1, D_SSD), gnorm=ssd_norm.reshape(1, D_SSD),
               ltri=jnp.asarray(ltri, BF16))
    wts.update(wo_att=w_out[:D_ATT].astype(BF16), wo_ssd=w_out[D_ATT:].astype(BF16),
               gffn=norm_ffn.reshape(1, D_MODEL), w_gate=w_gate.astype(BF16), w_up=w_up.astype(BF16),
               w_down=w_down.astype(BF16))
    return wts


def _gate_expand():
    m = np.zeros((ATT_KV_HEADS, LANES, 3 * ATT_REP * HEAD_DIM), np.float32)
    for g in range(ATT_KV_HEADS):
        for r in range(ATT_REP):
            for br in range(3):
                c0 = br * ATT_REP * HEAD_DIM + r * HEAD_DIM
                m[g, g * 3 * ATT_REP + r * 3 + br, c0:c0 + HEAD_DIM] = 1.0
    return jnp.asarray(m, BF16)


def _sel_matrix(n_blk_pad, n_sel):
    m = np.zeros((n_blk_pad, LANES), np.float32)
    for n in range(n_blk_pad - 1):
        for j in {n // 4, (n + 1) // 4}:
            if j < n_sel:
                m[n, j] = 1.0
    return jnp.asarray(m, BF16)


def _override(q_pos, n_sel):
    j = np.arange(LANES)[None, :]
    cur = (q_pos // SEL_BLOCK)[:, None]
    forced = (j == 0) | (j == cur) | (j == cur - 1)
    ovr = np.where(forced, 1e30, np.where(j <= cur, 0.0, -1e30))
    ovr = np.where(j < n_sel, ovr, -1e30)
    return ovr.astype(np.float32)


def _prompt_tables(rel_bias, t):
    n_qt = t // TQ
    pos = np.arange(t)
    nb = t // CMP_STRIDE
    e = CMP_STRIDE * np.arange(nb) + (CMP_BLOCK - 1)
    dist = pos[None, :] - e[:, None]
    cmp_idx = _idx_table(dist, (dist >= 0) & (np.arange(nb)[:, None] < nb - 1))
    cmp_idx = cmp_idx.reshape(nb, n_qt, TQ).transpose(1, 0, 2)
    i = np.arange(TQ)[None, :]
    j = np.arange(TQ)[:, None]
    diag = _idx_table(i - j, i >= j)
    prev = _idx_table(TQ + i - j, np.ones((TQ, TQ), bool))
    prev2 = _idx_table(2 * TQ + i - j, (2 * TQ + i - j) < WINDOW)
    att_idx = np.stack([prev2, prev, diag])
    return _bias_tables(rel_bias, cmp_idx, stack_cols=True), _bias_tables(rel_bias, att_idx, stack_cols=True)


def _prompt_layer(x, wts, rel_bias):
    b, t, _ = x.shape
    n = b * t
    p = _project(x.reshape(n, D_MODEL), wts, t, "prompt")
    n_sub = t // CMP_STRIDE
    sub = lambda a: a.reshape(b, n_sub, CMP_STRIDE * D_KV)
    kcmp, vcmp = _compress_prompt(sub(p["kc_b"]), sub(p["vc_b"]), wts)
    cmp_tab, att_tab = _prompt_tables(rel_bias, t)
    n_sel = t // SEL_BLOCK
    ovr_t = jnp.asarray(_override(np.arange(t), n_sel).reshape(t // TQ, TQ, LANES).transpose(0, 2, 1))
    r3 = lambda a: a.reshape(b, t, a.shape[-1])
    o_cmp, pen_t = _cmp_select_t(p["q_t"], kcmp, vcmp, cmp_tab, _sel_matrix(n_sub, n_sel).T, ovr_t,
                                 tq=TQ, n_sel=n_sel, k_top=min(SEL_TOPN, n_sel))
    o_att = _prompt_attn(p["q_t"], pen_t, o_cmp, r3(p["gd"]), r3(p["ksa"]), p["vs_tb"], r3(p["kw_b"]), p["vw_tb"],
                         att_tab, _gate_expand(), tq=TQ)
    carry = jnp.zeros((b, 8, D_CONV), F32)
    h0 = jnp.zeros((b, 512, 128), F32)
    y_ssd, h_fin = _ssd(r3(p["xbc"]), r3(p["z"]), r3(p["gd"]), carry, h0, wts, n_valid=SSD_CHUNK)
    y = _finish(x.reshape(n, D_MODEL), o_att.reshape(n, 512), y_ssd.reshape(n, 512), wts)
    wb = min(WINDOW, t)
    kv_out = {k: _token_major(p[k + "_t"]) for k in KV_NAMES}
    ssm = h_fin.reshape(b, SSD_HEADS, 64, SSD_STATE)
    return y.reshape(b, t, D_MODEL), (kv_out["kc"], kv_out["vc"], kv_out["ks"], kv_out["vs"],
                                      kv_out["kw"][:, t - wb:], kv_out["vw"][:, t - wb:],
                                      r3(p["xbc"])[:, t - (SSD_CONV - 1):], ssm)


def _token_major(a_t):
    b, _, t = a_t.shape
    return jnp.swapaxes(a_t, 1, 2).reshape(b, t, ATT_KV_HEADS, HEAD_DIM)


def _feature_major(a):
    return jnp.swapaxes(a.reshape(a.shape[:-2] + (D_KV,)), -1, -2)


def _sample_tables(rel_bias, past, s_new, n_blk_pad):
    s = np.minimum(np.arange(TQS), s_new - 1)[:, None]
    pos = past + s
    nidx = np.arange(n_blk_pad)[None, :]
    e = CMP_STRIDE * nidx + (CMP_BLOCK - 1)
    cmp_idx = _idx_table(pos - e, (e <= pos) & (nidx < n_blk_pad - 1))[None]
    key = np.arange(past)[None, :]
    sel_idx = _idx_table(pos - key, np.ones((TQS, past), bool))[None]
    jn = np.arange(LANES)[None, :]
    new_idx = _idx_table(s - jn, (jn <= s) & (jn < s_new))[None]
    wi = np.arange(WINDOW)[None, :]
    wdist = pos - (past - WINDOW + wi)
    win_idx = _idx_table(wdist, (wdist >= 0) & (wdist < WINDOW))[None]
    tabs = [_bias_tables(rel_bias, t) for t in (cmp_idx, sel_idx, new_idx, win_idx)]
    return tabs, pos[:, 0]


def _sample_layer(x, c_kc, c_vc, c_ks, c_vs, c_kw, c_vw, s_conv, s_ssm, page_table, wts, rel_bias):
    bs, s_new, _ = x.shape
    n = bs * s_new
    n_pages = page_table.shape[1]
    past = n_pages * PAGE_SIZE
    p = _project(x.reshape(n, D_MODEL), wts, s_new, "sample")
    r3 = lambda a: a.reshape(bs, s_new, a.shape[-1])
    padq = lambda a, rows: jnp.pad(r3(a), ((0, 0), (0, rows - s_new), (0, 0)))
    new_t = lambda a_t: jnp.pad(jnp.swapaxes(a_t.reshape(D_KV, bs, s_new), 0, 1), ((0, 0), (0, 0), (LANES - s_new, 0)))
    kcmp, vcmp = _compress_sample(page_table, _feature_major(c_kc), _feature_major(c_vc), wts)
    n_blk_pad = past // CMP_STRIDE
    (cmp_tab, sel_tab, new_tab, win_tab), pos = _sample_tables(rel_bias, past, s_new, n_blk_pad)
    n_sel = past // SEL_BLOCK
    ovr = jnp.asarray(_override(pos, n_sel)[None])
    qp = padq(p["q"], TQS)
    o_cmp, pen = _cmp_select(qp, kcmp, vcmp, cmp_tab, _sel_matrix(n_blk_pad, n_sel), ovr,
                             tq=TQS, n_sel=n_sel, k_top=min(SEL_TOPN - 1, n_sel))
    blk_of_key = np.arange(past)[None, :] // SEL_BLOCK
    eall = jnp.asarray((np.arange(LANES)[:, None] == blk_of_key).astype(np.float32), BF16)
    o_att, kw_new, vw_new = _sample_attn(
        page_table, qp, pen, o_cmp, padq(p["gd"], TQS),
        _feature_major(c_ks), _feature_major(c_vs), padq(p["ks_b"], LANES), padq(p["vs_b"], LANES),
        _feature_major(c_kw), _feature_major(c_vw), padq(p["kw_b"], LANES), padq(p["vw_b"], LANES),
        new_t(p["kw_t"]), new_t(p["vw_t"]),
        eall, sel_tab, new_tab, win_tab, _gate_expand(), n_new=s_new)
    carry = jnp.pad(s_conv, ((0, 0), (8 - (SSD_CONV - 1), 0), (0, 0)))
    h0 = s_ssm.reshape(bs, 512, SSD_STATE)
    h0 = jnp.concatenate([h0, h0], axis=-1)
    y_ssd, h_fin = _ssd(padq(p["xbc"], SSD_CHUNK), padq(p["z"], SSD_CHUNK), padq(p["gd"], SSD_CHUNK), carry, h0, wts,
                        n_valid=s_new)
    y = _finish(x.reshape(n, D_MODEL), o_att[:, :s_new].reshape(n, 512), y_ssd[:, :s_new].reshape(n, 512), wts)
    kv4 = lambda a: a.reshape(bs, -1, ATT_KV_HEADS, HEAD_DIM)
    ssm = h_fin.reshape(bs, SSD_HEADS, 64, SSD_STATE)
    conv_state = jnp.concatenate([s_conv, r3(p["xbc"])], axis=1)[:, s_new:]
    return y.reshape(bs, s_new, D_MODEL), (kv4(p["kc"]), kv4(p["vc"]), kv4(p["ks"]), kv4(p["vs"]),
                                           _token_major(kw_new), _token_major(vw_new), conv_state, ssm)


def kernel(x_prompt, x_sample, cache_k_cmp, cache_v_cmp, cache_k_sel, cache_v_sel, cache_k_win, cache_v_win,
           state_conv, state_ssm, page_table, norm_mix, w_in, q_norm, k_norm, cmp_pe, cmp_w1, cmp_w2, rel_bias,
           conv_w, conv_b, dt_bias, a_log, d_skip, ssd_norm, w_out, norm_ffn, w_gate, w_up, w_down):
    depth = w_in.shape[0]
    y_p, y_s = x_prompt, x_sample
    p_states, s_states = [], []
    for l in range(depth):
        wts = _prep_weights(norm_mix[l], w_in[l], q_norm[l], k_norm[l], cmp_pe[l], cmp_w1[l], cmp_w2[l],
                            conv_w[l], conv_b[l], dt_bias[l], a_log[l], d_skip[l], ssd_norm[l], w_out[l],
                            norm_ffn[l], w_gate[l], w_up[l], w_down[l])
        y_p, st_p = _prompt_layer(y_p, wts, rel_bias)
        y_s, st_s = _sample_layer(y_s, cache_k_cmp[l], cache_v_cmp[l], cache_k_sel[l], cache_v_sel[l],
                                  cache_k_win[l], cache_v_win[l], state_conv[l], state_ssm[l], page_table,
                                  wts, rel_bias)
        p_states.append(st_p)
        s_states.append(st_s)
    p_out = [jnp.stack(a) for a in zip(*p_states)]
    s_out = [jnp.stack(a) for a in zip(*s_states)]
    return (y_p, y_s, *p_out, *s_out)
```

```python
import functools
import math

import numpy as np
import jax
import jax.numpy as jnp
from jax import lax
from jax.experimental import pallas as pl
from jax.experimental.pallas import tpu as pltpu

F32 = jnp.float32
BF16 = jnp.bfloat16

D_MODEL = 1024
HEAD_DIM = 64
ATT_HEADS = 8
ATT_KV_HEADS = 2
ATT_REP = ATT_HEADS // ATT_KV_HEADS
CMP_BLOCK = 32
CMP_STRIDE = 16
CMP_HIDDEN = 64
SEL_BLOCK = 64
SEL_TOPN = 16
WINDOW = 512
N_BUCKETS = 32
MAX_DISTANCE = 128
PAGE_SIZE = 128
SSD_HEADS = 8
SSD_STATE = 64
SSD_CONV = 4
SSD_CHUNK = 128
D_ATT = ATT_HEADS * HEAD_DIM
D_SSD = SSD_HEADS * 64
D_KV = ATT_KV_HEADS * HEAD_DIM
D_CONV = D_SSD + 2 * 2 * SSD_STATE
D_FF = ((8 * D_MODEL // 3 + 255) // 256) * 256
OFF_KV = D_ATT
OFF_GATE = OFF_KV + 6 * D_KV
OFF_Z = OFF_GATE + 3 * ATT_HEADS
OFF_XBC = OFF_Z + D_SSD
OFF_DT = OFF_XBC + D_CONV
D_IN = OFF_DT + SSD_HEADS
RMS_EPS = 1e-6
NEG = -1e30

C_Q, C_KV, C_Z, C_XBC, C_GD = 0, 512, 1280, 1792, 2560
D_INR = 2688
DT_LANE = 24
LANES = 128
TQ = 256
TQS = 16
VMEM_LIMIT = 48 * 1024 * 1024


def _dot(a, b):
    return jnp.dot(a, b, preferred_element_type=F32)


def _dot_nt(a, b):
    return lax.dot_general(a, b, (((1,), (1,)), ((), ())), preferred_element_type=F32)


def _split_dot(a, b, parts):
    acc = None
    rem = a
    for _ in range(parts):
        piece = rem.astype(BF16)
        rem = rem - piece.astype(F32)
        t = _dot(piece, b)
        acc = t if acc is None else acc + t
    return acc


def _silu(x):
    return x * (1.0 / (1.0 + jnp.exp(-x)))


def _params(sem=None):
    kw = dict(vmem_limit_bytes=VMEM_LIMIT)
    if sem is not None:
        kw["dimension_semantics"] = sem
    return pltpu.CompilerParams(**kw)


KV_NAMES = ("kc", "vc", "ks", "vs", "kw", "vw")
PROJ_OUTPUTS = {
    "prompt": ([(k + "_t", "seq", 128, F32) for k in KV_NAMES]
               + [("kc_b", "row", 128, BF16), ("vc_b", "row", 128, BF16), ("ksa", "row", 256, BF16),
                  ("kw_b", "row", 128, BF16), ("q_t", "col", 512, BF16), ("vs_tb", "col", 128, BF16),
                  ("vw_tb", "col", 128, BF16)]),
    "sample": ([(k, "row", 128, F32) for k in KV_NAMES]
               + [("q", "row", 512, BF16), ("ks_b", "row", 128, BF16), ("vs_b", "row", 128, BF16),
                  ("kw_b", "row", 128, BF16), ("vw_b", "row", 128, BF16),
                  ("kw_t", "col", 128, F32), ("vw_t", "col", 128, F32)]),
}
PROJ_COMMON = [("z", "row", 512, F32), ("xbc", "row", 768, F32), ("gd", "row", 128, F32)]


def _proj_kernel(x_ref, gmix_ref, w_ref, qg_ref, kg_ref, bq_ref, bk_ref, *out_refs, names, tm, t_len):
    out = dict(zip(names, out_refs))
    x = x_ref[...]
    ms = jnp.mean(x * x, axis=-1, keepdims=True)
    u = (x * lax.rsqrt(ms + RMS_EPS) * gmix_ref[...]).astype(BF16)

    parts = [(lo, _dot(u, w_ref[:, lo:hi])) for lo, hi in ((C_Q, C_Z), (C_Z, D_INR))]

    def proj(lo, hi):
        base, val = parts[0] if hi <= C_Z else parts[1]
        return val[:, lo - base:hi - base]

    def headnorm(v, b_ref, g_ref):
        msq = _dot((v * v).astype(BF16), b_ref[...])
        return v * lax.rsqrt(msq + RMS_EPS) * g_ref[...]

    def put(name, value):
        if name in out:
            ref = out[name]
            ref[...] = value().astype(ref.dtype).reshape(ref.shape)

    q = headnorm(proj(C_Q, C_Q + 512), bq_ref, qg_ref)
    put("q", lambda: q)
    put("q_t", lambda: q.T)
    kv = {}
    for i, name in enumerate(KV_NAMES):
        v = proj(C_KV + 128 * i, C_KV + 128 * (i + 1))
        kv[name] = headnorm(v, bk_ref, kg_ref) if name in ("ks", "kw") else v
    for name, v in kv.items():
        put(name, lambda v=v: v)
        put(name + "_b", lambda v=v: v)
        put(name + "_t", lambda v=v: v.T)
        put(name + "_tb", lambda v=v: v.T)
    if "ksa" in out:
        row = pl.program_id(0) * tm + lax.broadcasted_iota(jnp.int32, (tm, LANES), 0)
        blk = (row % t_len) // SEL_BLOCK
        lane = lax.broadcasted_iota(jnp.int32, (tm, LANES), 1)
        out["ksa"][:, 0:128] = kv["ks"].astype(BF16)
        out["ksa"][:, 128:256] = jnp.where(lane == blk, 1.0, 0.0).astype(BF16)
    put("z", lambda: proj(C_Z, C_Z + 512))
    put("xbc", lambda: proj(C_XBC, C_XBC + 768))
    put("gd", lambda: proj(C_GD, C_GD + 128))


def _project(x2d, wts, t_len, mode):
    n = x2d.shape[0]
    tm = min(512, n)
    per_seq = max(t_len // tm, 1)
    full = lambda a: pl.BlockSpec(a.shape, lambda i: (0,) * a.ndim)
    ins = [x2d, wts["gmix"], wts["w_in"], wts["qg"], wts["kg"], wts["bq"], wts["bk"]]
    outs = PROJ_OUTPUTS[mode] + PROJ_COMMON
    specs, shapes = [], []
    for _, layout, w, dt in outs:
        if layout == "row":
            specs.append(pl.BlockSpec((tm, w), lambda i: (i, 0)))
            shapes.append(jax.ShapeDtypeStruct((n, w), dt))
        elif layout == "col":
            specs.append(pl.BlockSpec((w, tm), lambda i: (0, i)))
            shapes.append(jax.ShapeDtypeStruct((w, n), dt))
        else:
            specs.append(pl.BlockSpec((1, w, tm), lambda i: (i // per_seq, 0, i % per_seq)))
            shapes.append(jax.ShapeDtypeStruct((n // t_len, w, t_len), dt))
    names = tuple(name for name, _, _, _ in outs)
    res = pl.pallas_call(
        functools.partial(_proj_kernel, names=names, tm=tm, t_len=t_len),
        grid=(n // tm,),
        in_specs=[pl.BlockSpec((tm, D_MODEL), lambda i: (i, 0))] + [full(a) for a in ins[1:]],
        out_specs=specs,
        out_shape=shapes,
        compiler_params=_params(("parallel",)),
        name="proj",
    )(*ins)
    return dict(zip(names, res))


def _bucket_np(dist):
    n = np.maximum(dist, 0)
    max_exact = N_BUCKETS // 2
    nf = np.maximum(n, 1).astype(np.float64)
    large = max_exact + (np.log(nf / max_exact) / math.log(MAX_DISTANCE / max_exact)
                         * (N_BUCKETS - max_exact)).astype(np.int64)
    large = np.minimum(large, N_BUCKETS - 1)
    return np.where(n < max_exact, n, large).astype(np.int32)


def _idx_table(dist, valid):
    return np.where(valid, _bucket_np(dist), -1).astype(np.int32)


def _table_kernel(rb_ref, idx_ref, out_ref):
    h = pl.program_id(0) * ATT_REP + pl.program_id(2)
    idx = idx_ref[0]
    far = rb_ref[N_BUCKETS - 1, h]
    acc = jnp.zeros(idx.shape, F32)
    for b in range(N_BUCKETS - 1):
        acc = jnp.where(idx == b, rb_ref[b, h] - far, acc)
    out_ref[...] = jnp.where(idx < 0, NEG, acc).reshape(out_ref.shape)


def _bias_tables(rel_bias, idx, stack_cols=False):
    k, r, c = idx.shape
    if stack_cols:
        return pl.pallas_call(
            _table_kernel,
            grid=(ATT_KV_HEADS, k, ATT_REP),
            in_specs=[pl.BlockSpec(memory_space=pltpu.SMEM),
                      pl.BlockSpec((1, r, c), lambda g, kk, rr: (kk, 0, 0))],
            out_specs=pl.BlockSpec((1, 1, r, c), lambda g, kk, rr: (g, kk, 0, rr)),
            out_shape=jax.ShapeDtypeStruct((ATT_KV_HEADS, k, r, ATT_REP * c), F32),
            name="bias_table_t",
        )(rel_bias, jnp.asarray(idx))
    out = pl.pallas_call(
        _table_kernel,
        grid=(ATT_KV_HEADS, k, ATT_REP),
        in_specs=[pl.BlockSpec(memory_space=pltpu.SMEM),
                  pl.BlockSpec((1, r, c), lambda g, kk, rr: (kk, 0, 0))],
        out_specs=pl.BlockSpec((1, 1, 1, r, c), lambda g, kk, rr: (g, kk, rr, 0, 0)),
        out_shape=jax.ShapeDtypeStruct((ATT_KV_HEADS, k, ATT_REP, r, c), F32),
        name="bias_table",
    )(rel_bias, jnp.asarray(idx))
    return out.reshape(ATT_KV_HEADS, k, ATT_REP * r, c)


def _compress_core(x, w1, pe, w2):
    return _compress_tail(_dot(x.astype(BF16), w1), w1, pe, w2)


def _compress_tail(u, w1, pe, w2):
    n_sub = u.shape[0]
    upe = _dot(pe, w1)
    nxt = pltpu.roll(u[:, 128:256], n_sub - 1, 0)
    pre = u[:, 0:128] + nxt + upe[0:1, 0:128] + upe[1:2, 128:256]
    return _dot(_silu(pre).astype(BF16), w2)


def _knorm(v, bk, kg):
    msq = _dot((v * v).astype(BF16), bk)
    return v * lax.rsqrt(msq + RMS_EPS) * kg


def _compress_prompt_kernel(kc_ref, vc_ref, w1k_ref, w1v_ref, pek_ref, pev_ref, w2k_ref, w2v_ref,
                            kg_ref, bk_ref, ko_ref, vo_ref):
    kc = _compress_core(kc_ref[0], w1k_ref[...], pek_ref[...], w2k_ref[...])
    ko_ref[0] = _knorm(kc, bk_ref[...], kg_ref[...]).astype(BF16)
    vo_ref[0] = _compress_core(vc_ref[0], w1v_ref[...], pev_ref[...], w2v_ref[...]).astype(BF16)


def _compress_prompt(kc, vc, wts):
    b, n_sub, w = kc.shape
    full = lambda a: pl.BlockSpec(a.shape, lambda i: (0,) * a.ndim)
    consts = [wts["w1k"], wts["w1v"], wts["pek"], wts["pev"], wts["w2k"], wts["w2v"], wts["kg"], wts["bk"]]
    blk = pl.BlockSpec((1, n_sub, w), lambda i: (i, 0, 0))
    oblk = pl.BlockSpec((1, n_sub, 128), lambda i: (i, 0, 0))
    return pl.pallas_call(
        _compress_prompt_kernel,
        grid=(b,),
        in_specs=[blk, blk] + [full(a) for a in consts],
        out_specs=[oblk, oblk],
        out_shape=[jax.ShapeDtypeStruct((b, n_sub, 128), BF16)] * 2,
        compiler_params=_params(("parallel",)),
        name="compress_prompt",
    )(kc, vc, *consts)


def _page_copy(cache_ref, page, buf_ref, slot, p, sem_ref, lane_major):
    if lane_major:
        dst = buf_ref.at[slot, :, pl.ds(pl.multiple_of(p * PAGE_SIZE, PAGE_SIZE), PAGE_SIZE)]
    else:
        dst = buf_ref.at[slot, p]
    return pltpu.make_async_copy(cache_ref.at[page], dst, sem_ref.at[slot])


def _gather_start(pt_ref, b, slot, caches, bufs, sems, n_pages, lane_major):
    def body(p, carry):
        page = pt_ref[b, p]
        for cache_ref, buf_ref, sem_ref in zip(caches, bufs, sems):
            _page_copy(cache_ref, page, buf_ref, slot, p, sem_ref, lane_major).start()
        return carry
    lax.fori_loop(0, n_pages, body, 0)


def _gather_wait(slot, caches, bufs, sems, n_pages, lane_major):
    def body(p, carry):
        for cache_ref, buf_ref, sem_ref in zip(caches, bufs, sems):
            _page_copy(cache_ref, 0, buf_ref, slot, p, sem_ref, lane_major).wait()
        return carry
    lax.fori_loop(0, n_pages, body, 0)


def _gather_pipeline(pt_ref, caches, bufs, sems, n_pages, lane_major):
    b = pl.program_id(0)
    nb = pl.num_programs(0)
    slot = b % 2

    @pl.when(b == 0)
    def _():
        _gather_start(pt_ref, 0, 0, caches, bufs, sems, n_pages, lane_major)

    @pl.when(b + 1 < nb)
    def _():
        _gather_start(pt_ref, b + 1, 1 - slot, caches, bufs, sems, n_pages, lane_major)

    _gather_wait(slot, caches, bufs, sems, n_pages, lane_major)
    return slot


def _compress_paged(buf, slot, rows_s, perm, w1, pe, w2, n_pages):
    groups = PAGE_SIZE // CMP_STRIDE

    def body(i, carry):
        pair = buf[slot, pl.ds(2 * i, 2)].reshape(2 * D_KV, PAGE_SIZE)
        rows = _dot_nt(perm, pair.astype(BF16))
        for half in range(2):
            start = pl.multiple_of((2 * i + half) * groups, groups)
            for s in range(CMP_STRIDE):
                rows_s[s // 2, pl.ds(start, groups), (s % 2) * D_KV:(s % 2 + 1) * D_KV] = (
                    rows[s * groups:(s + 1) * groups, half * D_KV:(half + 1) * D_KV])
        return carry
    lax.fori_loop(0, n_pages // 2, body, 0, unroll=4)
    u = None
    for j in range(CMP_STRIDE // 2):
        t = _dot(rows_s[j].astype(BF16), w1[2 * j * D_KV:(2 * j + 2) * D_KV, :])
        u = t if u is None else u + t
    return _compress_tail(u, w1, pe, w2)


def _compress_sample_kernel(pt_ref, ck_ref, cv_ref, perm_ref, w1k_ref, w1v_ref, pek_ref, pev_ref, w2k_ref, w2v_ref,
                            kg_ref, bk_ref, ko_ref, vo_ref, kbuf, vbuf, rows_s, ksem, vsem, *, n_pages):
    slot = _gather_pipeline(pt_ref, (ck_ref, cv_ref), (kbuf, vbuf), (ksem, vsem), n_pages, lane_major=False)
    perm = perm_ref[...]
    kc = _compress_paged(kbuf, slot, rows_s, perm, w1k_ref[...], pek_ref[...], w2k_ref[...], n_pages)
    ko_ref[0] = _knorm(kc, bk_ref[...], kg_ref[...]).astype(BF16)
    vo_ref[0] = _compress_paged(vbuf, slot, rows_s, perm, w1v_ref[...], pev_ref[...], w2v_ref[...], n_pages).astype(BF16)


def _compress_sample(page_table, ck, cv, wts):
    bs, n_pages = page_table.shape
    n_sub = n_pages * PAGE_SIZE // CMP_STRIDE
    full = lambda a: pl.BlockSpec(a.shape, lambda i, pt: (0,) * a.ndim)
    groups = PAGE_SIZE // CMP_STRIDE
    r = np.arange(PAGE_SIZE)
    perm = jnp.asarray((np.arange(PAGE_SIZE)[None, :] == (CMP_STRIDE * (r % groups) + r // groups)[:, None])
                       .astype(np.float32), BF16)
    consts = [perm, wts["w1k"], wts["w1v"], wts["pek"], wts["pev"], wts["w2k"], wts["w2v"], wts["kg"], wts["bk"]]
    anyspec = pl.BlockSpec(memory_space=pl.ANY)
    oblk = pl.BlockSpec((1, n_sub, 128), lambda i, pt: (i, 0, 0))
    pages = pltpu.VMEM((2, n_pages, D_KV, PAGE_SIZE), F32)
    return pl.pallas_call(
        functools.partial(_compress_sample_kernel, n_pages=n_pages),
        grid_spec=pltpu.PrefetchScalarGridSpec(
            num_scalar_prefetch=1,
            grid=(bs,),
            in_specs=[anyspec, anyspec] + [full(a) for a in consts],
            out_specs=[oblk, oblk],
            scratch_shapes=[pages, pages, pltpu.VMEM((CMP_STRIDE // 2, n_sub, 2 * D_KV), F32),
                            pltpu.SemaphoreType.DMA((2,)), pltpu.SemaphoreType.DMA((2,))]),
        out_shape=[jax.ShapeDtypeStruct((bs, n_sub, 128), BF16)] * 2,
        compiler_params=_params(("arbitrary",)),
        name="compress_sample",
    )(page_table, ck, cv, *consts)


def _group_queries(q, g, tq):
    lane = lax.broadcasted_iota(jnp.int32, (tq, LANES), 1)
    mine = (lane >= g * HEAD_DIM) & (lane < (g + 1) * HEAD_DIM)
    zero = jnp.zeros((tq, LANES), q.dtype)
    return jnp.concatenate([jnp.where(mine, q[:, r * 128:(r + 1) * 128], zero) for r in range(ATT_REP)], axis=0)


def _pack_heads(o, g, tq):
    lane = lax.broadcasted_iota(jnp.int32, (tq, LANES), 1)
    first = g == 0
    chunks = []
    for k in range(ATT_REP // 2):
        a = o[2 * k * tq:(2 * k + 1) * tq]
        b = o[(2 * k + 1) * tq:(2 * k + 2) * tq]
        lo = jnp.where(first, a, pltpu.roll(a, HEAD_DIM, 1))
        hi = jnp.where(first, pltpu.roll(b, HEAD_DIM, 1), b)
        chunks.append(jnp.where(lane < HEAD_DIM, lo, hi))
    return jnp.concatenate(chunks, axis=1)


def _cmp_select_kernel(q_ref, kc_ref, vc_ref, tab_ref, msel_ref, ovr_ref, o_ref, pen_ref,
                       *, tq, n_sel, k_top, bb):
    ovr = ovr_ref[0]
    scores = []
    for i in range(bb):
        q = q_ref[i]
        kc = kc_ref[i]
        vc = vc_ref[i]
        for g in range(ATT_KV_HEADS):
            tab = tab_ref[g, 0]
            s = _dot_nt(_group_queries(q, g, tq), kc) + tab
            m = jnp.max(s, axis=-1, keepdims=True)
            e = jnp.where(tab > 0.5 * NEG, jnp.exp(s - m), 0.0)
            p = e / jnp.maximum(jnp.sum(e, axis=-1, keepdims=True), 1e-30)
            o_ref[i, :, g * 256:(g + 1) * 256] = _pack_heads(_dot(p.astype(BF16), vc), g, tq)
            imp = p[0:tq]
            for r in range(1, ATT_REP):
                imp = imp + p[r * tq:(r + 1) * tq]
            score = _split_dot(imp, msel_ref[...], 3)
            scores.append(jnp.where(ovr == 0.0, score, ovr))
    score = jnp.concatenate(scores, axis=0)
    lane = lax.broadcasted_iota(jnp.int32, score.shape, 1)
    rank = jnp.zeros(score.shape, F32)
    for j in range(n_sel):
        col = score[:, j:j + 1]
        beats = (col > score) | ((col == score) & (lane > j))
        rank = rank + jnp.where(beats, 1.0, 0.0)
    pen = jnp.where(rank < k_top, 0.0, NEG).astype(BF16)
    for i in range(bb):
        for g in range(ATT_KV_HEADS):
            r0 = (i * ATT_KV_HEADS + g) * tq
            pen_ref[i, :, g * 128:(g + 1) * 128] = pen[r0:r0 + tq]


def _cmp_select(q, kc, vc, tab, msel, ovr, *, tq, n_sel, k_top):
    b, t, _ = q.shape
    nb = kc.shape[1]
    n_qt = t // tq
    bb = math.gcd(b, 8)
    return pl.pallas_call(
        functools.partial(_cmp_select_kernel, tq=tq, n_sel=n_sel, k_top=k_top, bb=bb),
        grid=(n_qt, b // bb),
        in_specs=[pl.BlockSpec((bb, tq, 512), lambda qi, bi: (bi, qi, 0)),
                  pl.BlockSpec((bb, nb, 128), lambda qi, bi: (bi, 0, 0)),
                  pl.BlockSpec((bb, nb, 128), lambda qi, bi: (bi, 0, 0)),
                  pl.BlockSpec((ATT_KV_HEADS, 1, ATT_REP * tq, nb), lambda qi, bi: (0, qi, 0, 0)),
                  pl.BlockSpec(msel.shape, lambda qi, bi: (0, 0)),
                  pl.BlockSpec((1, tq, 128), lambda qi, bi: (qi, 0, 0))],
        out_specs=[pl.BlockSpec((bb, tq, 512), lambda qi, bi: (bi, qi, 0)),
                   pl.BlockSpec((bb, tq, 256), lambda qi, bi: (bi, qi, 0))],
        out_shape=[jax.ShapeDtypeStruct((b, t, 512), F32), jax.ShapeDtypeStruct((b, t, 256), BF16)],
        compiler_params=_params(("parallel", "parallel")),
        name="cmp_select",
    )(q, kc, vc, tab, msel, ovr)


def _cmp_select_t_kernel(qt_ref, kc_ref, vc_ref, tab_ref, msel_ref, ovr_ref, o_ref, pen_ref,
                         *, tq, n_sel, k_top):
    qt = qt_ref[...]
    kc = kc_ref[0]
    vct = vc_ref[0].astype(F32).T.astype(BF16)
    ovr = ovr_ref[0, 0:n_sel, :]
    row = lax.broadcasted_iota(jnp.int32, (n_sel, tq), 0)
    for g in range(ATT_KV_HEADS):
        tab = tab_ref[g, 0]
        s = _dot(kc, _group_queries_t(qt, g, tq)) + tab
        m = jnp.max(s, axis=0, keepdims=True)
        e = jnp.where(tab > 0.5 * NEG, jnp.exp(s - m), 0.0)
        p = e / jnp.maximum(jnp.sum(e, axis=0, keepdims=True), 1e-30)
        o_ref[0, :, g * 256:(g + 1) * 256] = _unpack_heads_t(_dot(vct, p.astype(BF16)), g, tq)
        imp = p[:, 0:tq]
        for r in range(1, ATT_REP):
            imp = imp + p[:, r * tq:(r + 1) * tq]
        score = _split_dot_left(msel_ref[...], imp)[0:n_sel, :]
        score = jnp.where(ovr == 0.0, score, ovr)
        rank = jnp.zeros((n_sel, tq), F32)
        for j in range(n_sel):
            cand = score[j:j + 1, :]
            beats = (cand > score) | ((cand == score) & (row > j))
            rank = rank + jnp.where(beats, 1.0, 0.0)
        pen_ref[0, g, 0:n_sel, :] = jnp.where(rank < k_top, 0.0, NEG).astype(BF16)
        pen_ref[0, g, n_sel:LANES, :] = jnp.zeros((LANES - n_sel, tq), BF16)


def _cmp_select_t(qt, kc, vc, tab, msel_t, ovr_t, *, tq, n_sel, k_top):
    b, nb, _ = kc.shape
    n_qt = tab.shape[1]
    t = n_qt * tq
    return pl.pallas_call(
        functools.partial(_cmp_select_t_kernel, tq=tq, n_sel=n_sel, k_top=k_top),
        grid=(n_qt, b),
        in_specs=[pl.BlockSpec((512, tq), lambda qi, bi: (0, bi * n_qt + qi)),
                  pl.BlockSpec((1, nb, 128), lambda qi, bi: (bi, 0, 0)),
                  pl.BlockSpec((1, nb, 128), lambda qi, bi: (bi, 0, 0)),
                  pl.BlockSpec((ATT_KV_HEADS, 1, nb, ATT_REP * tq), lambda qi, bi: (0, qi, 0, 0)),
                  pl.BlockSpec(msel_t.shape, lambda qi, bi: (0, 0)),
                  pl.BlockSpec((1, 128, tq), lambda qi, bi: (qi, 0, 0))],
        out_specs=[pl.BlockSpec((1, tq, 512), lambda qi, bi: (bi, qi, 0)),
                   pl.BlockSpec((1, ATT_KV_HEADS, 128, tq), lambda qi, bi: (bi, 0, 0, qi))],
        out_shape=[jax.ShapeDtypeStruct((b, t, 512), F32), jax.ShapeDtypeStruct((b, ATT_KV_HEADS, 128, t), BF16)],
        compiler_params=_params(("parallel", "parallel")),
        name="cmp_select_t",
    )(qt, kc, vc, tab, msel_t, ovr_t)


SUM_ROWS = 16


def _flash_init(m_ref, acc_ref):
    m_ref[...] = jnp.full(m_ref.shape, NEG, F32)
    acc_ref[...] = jnp.zeros(acc_ref.shape, F32)


def _flash_tile_t(k, qa_ref, q_rows, v_t, bias, m_ref, acc_ref):
    cols = slice(0, qa_ref.shape[1])
    s = _dot(k, qa_ref[0:q_rows, cols])
    if bias is not None:
        s = s + bias(cols)
    m_old = m_ref[...]
    m_new = jnp.maximum(m_old, jnp.max(s, axis=0, keepdims=True))
    alpha = jnp.exp(m_old - m_new)
    p = jnp.exp((s - m_new).astype(BF16))
    v_aug = jnp.concatenate([v_t, jnp.ones((SUM_ROWS, v_t.shape[1]), BF16)], axis=0)
    acc_ref[...] = alpha * acc_ref[...] + _dot(v_aug, p)
    m_ref[...] = m_new


def _flash_result(acc_ref, g):
    return acc_ref[g, 0:128] / acc_ref[g, 128:129]


def _group_queries_t(qt, g, tq):
    row = lax.broadcasted_iota(jnp.int32, (LANES, tq), 0)
    mine = (row >= g * HEAD_DIM) & (row < (g + 1) * HEAD_DIM)
    zero = jnp.zeros((LANES, tq), qt.dtype)
    return jnp.concatenate([jnp.where(mine, qt[r * 128:(r + 1) * 128, :], zero) for r in range(ATT_REP)], axis=1)


def _unpack_heads_t(o_t, g, tq):
    rows = jnp.concatenate([o_t[:, r * tq:(r + 1) * tq].T for r in range(ATT_REP)], axis=0)
    return _pack_heads(rows, g, tq)


def _gate_chunks(gd, gexp):
    sig = 1.0 / (1.0 + jnp.exp(-gd))
    return _split_dot(sig, gexp, 3)


def _combine(gx, o_cmp, o_sel, o_win):
    return gx[:, 0:256] * o_cmp + gx[:, 256:512] * o_sel + gx[:, 512:768] * o_win


def _prompt_attn_kernel(qt_ref, pen_ref, oc_ref, gd_ref, ksa_ref, vst_ref, kw_ref, vwt_ref, a_ref, gexp_ref,
                        o_ref, m_ref, acc_ref, qa_s, *, tq):
    qi = pl.program_id(1)
    groups = range(ATT_KV_HEADS)
    for g in groups:
        qa_s[g, 0:128, :] = _group_queries_t(qt_ref[...], g, tq)
        qa_s[g, 128:256, :] = jnp.concatenate([pen_ref[0, g]] * ATT_REP, axis=1)

    def sel_tile(start, n_keys, bias):
        k = ksa_ref[0, pl.ds(start, n_keys), :]
        v = vst_ref[:, pl.ds(start, n_keys)]
        for g in groups:
            _flash_tile_t(k, qa_s.at[g], 256, v, None if bias is None else functools.partial(bias, g),
                          m_ref.at[g], acc_ref.at[g])

    def win_tile(start, n_keys, bias):
        k = kw_ref[0, pl.ds(start, n_keys), :]
        v = vwt_ref[:, pl.ds(start, n_keys)]
        for g in groups:
            _flash_tile_t(k, qa_s.at[g], 128, v, functools.partial(bias, g), m_ref.at[g], acc_ref.at[g])

    def near(tile_fn, max_tiles):
        n_kinds = a_ref.shape[1]
        for n in range(1, max_tiles + 1):
            hit = (qi + 1 == n) if n < max_tiles else (qi + 1 >= n)

            def body(i, carry, n=n):
                bias = lambda g, cols: a_ref[g, n_kinds - n:n_kinds, :, cols].reshape(n * tq, cols.stop - cols.start)
                tile_fn(pl.multiple_of((qi + 1 - n) * tq, tq), n * tq, bias)
                return carry
            lax.fori_loop(0, hit.astype(jnp.int32), body, 0)

    _flash_init(m_ref, acc_ref)
    near(sel_tile, 2)
    n_far = jnp.maximum(qi - 1, 0)

    def far_pair(i, carry):
        sel_tile(pl.multiple_of(2 * i * tq, tq), 2 * tq, None)
        return carry
    lax.fori_loop(0, n_far // 2, far_pair, 0)

    def far_last(i, carry):
        sel_tile(pl.multiple_of((n_far - 1) * tq, tq), tq, None)
        return carry
    lax.fori_loop(0, n_far % 2, far_last, 0)
    o_sel = [_unpack_heads_t(_flash_result(acc_ref, g), g, tq) for g in groups]

    _flash_init(m_ref, acc_ref)
    near(win_tile, 3)
    for g in groups:
        o_win = _unpack_heads_t(_flash_result(acc_ref, g), g, tq)
        gx = _gate_chunks(gd_ref[0], gexp_ref[g])
        o_ref[0, :, g * 256:(g + 1) * 256] = _combine(gx, oc_ref[0, :, g * 256:(g + 1) * 256], o_sel[g], o_win).astype(BF16)


def _prompt_attn(qt, pen_t, o_cmp, gd, ksa, vst, kwb, vwt, atab, gexp, *, tq):
    b, t, _ = ksa.shape
    n_qt = t // tq
    nq = ATT_REP * tq
    qblk = lambda w: pl.BlockSpec((1, tq, w), lambda bi, qi: (bi, qi, 0))
    seq = lambda w: pl.BlockSpec((1, t, w), lambda bi, qi: (bi, 0, 0))
    seq_t = pl.BlockSpec((128, t), lambda bi, qi: (0, bi))
    full = lambda a: pl.BlockSpec(a.shape, lambda bi, qi: (0,) * a.ndim)
    return pl.pallas_call(
        functools.partial(_prompt_attn_kernel, tq=tq),
        grid=(b, n_qt),
        in_specs=[pl.BlockSpec((512, tq), lambda bi, qi: (0, bi * n_qt + qi)),
                  pl.BlockSpec((1, ATT_KV_HEADS, 128, tq), lambda bi, qi: (bi, 0, 0, qi)),
                  qblk(512), qblk(128), seq(256), seq_t, seq(128), seq_t, full(atab), full(gexp)],
        out_specs=qblk(512),
        out_shape=jax.ShapeDtypeStruct((b, t, 512), BF16),
        scratch_shapes=[pltpu.VMEM((ATT_KV_HEADS, 1, nq), F32),
                        pltpu.VMEM((ATT_KV_HEADS, 128 + SUM_ROWS, nq), F32), pltpu.VMEM((ATT_KV_HEADS, 256, nq), BF16)],
        compiler_params=_params(("parallel", "arbitrary")),
        name="prompt_attn",
    )(qt, pen_t, o_cmp, gd, ksa, vst, kwb, vwt, atab, gexp)


def _softmax_two(s_a, s_b, vt_a, v_b):
    m = jnp.maximum(jnp.max(s_a, axis=-1, keepdims=True), jnp.max(s_b, axis=-1, keepdims=True))
    p_a = jnp.exp(s_a - m)
    p_b = jnp.exp(s_b - m)
    l = jnp.sum(p_a, axis=-1, keepdims=True) + jnp.sum(p_b, axis=-1, keepdims=True)
    return (_dot_nt(p_a.astype(BF16), vt_a) + _dot(p_b.astype(BF16), v_b)) / l


def _slide_window(buf_t, new_t, n_new):
    w = buf_t.shape[1]
    rolled = pltpu.roll(buf_t, w - n_new, 1)
    lane = lax.broadcasted_iota(jnp.int32, (LANES, LANES), 1)
    last = jnp.where(lane >= LANES - n_new, new_t, rolled[:, w - LANES:w])
    return jnp.concatenate([rolled[:, 0:w - LANES], last], axis=1)


def _sample_attn_kernel(pt_ref, q_ref, pen_ref, oc_ref, gd_ref, ck_ref, cv_ref, ksn_ref, vsn_ref,
                        cw_ref, cvw_ref, kwn_ref, vwn_ref, kwt_ref, vwt_ref, eall_ref, tsel_ref, tnew_ref,
                        twin_ref, gexp_ref, o_ref, kwo_ref, vwo_ref, kbuf, vbuf, ke_s, ksem, vsem,
                        *, n_pages, n_new):
    slot = _gather_pipeline(pt_ref, (ck_ref, cv_ref), (kbuf, vbuf), (ksem, vsem), n_pages, lane_major=True)
    tq = TQS
    rows = ATT_REP * tq

    @pl.when(pl.program_id(0) == 0)
    def _():
        ke_s[128:256, :] = eall_ref[...]

    ke_s[0:128, :] = kbuf[slot].astype(BF16)
    q = q_ref[0]
    pen = pen_ref[0]
    qa = jnp.concatenate(
        [jnp.concatenate([_group_queries(q, g, tq), jnp.concatenate([pen[:, g * 128:(g + 1) * 128]] * ATT_REP, axis=0)],
                         axis=1) for g in range(ATT_KV_HEADS)], axis=0)
    qg = qa[:, 0:128]
    stack = lambda t_ref: t_ref[:, 0].reshape(ATT_KV_HEADS * rows, t_ref.shape[-1])
    tnew = stack(tnew_ref)
    s_past = _dot(qa, ke_s[...]) + stack(tsel_ref)
    s_new = _dot_nt(qg, ksn_ref[0]) + tnew
    o_sel = _softmax_two(s_past, s_new, vbuf[slot].astype(BF16), vsn_ref[0])
    s_buf = _dot(qg, cw_ref[0].astype(BF16)) + stack(twin_ref)
    s_new = _dot_nt(qg, kwn_ref[0]) + tnew
    o_win = _softmax_two(s_buf, s_new, cvw_ref[0].astype(BF16), vwn_ref[0])
    for g in range(ATT_KV_HEADS):
        gx = _gate_chunks(gd_ref[0], gexp_ref[g])
        part = slice(g * rows, (g + 1) * rows)
        o = _combine(gx, oc_ref[0, :, g * 256:(g + 1) * 256], _pack_heads(o_sel[part], g, tq), _pack_heads(o_win[part], g, tq))
        o_ref[0, :, g * 256:(g + 1) * 256] = o.astype(BF16)
    kwo_ref[0] = _slide_window(cw_ref[0], kwt_ref[0], n_new)
    vwo_ref[0] = _slide_window(cvw_ref[0], vwt_ref[0], n_new)


def _sample_attn(page_table, q, pen, o_cmp, gd, ck, cv, ksn, vsn, cw, cvw, kwn, vwn, kwt, vwt,
                 eall, tsel, tnew, twin, gexp, *, n_new):
    bs, n_pages = page_table.shape
    past = n_pages * PAGE_SIZE
    per_b = lambda a: pl.BlockSpec((1,) + a.shape[1:], lambda i, pt: (i,) + (0,) * (a.ndim - 1))
    full = lambda a: pl.BlockSpec(a.shape, lambda i, pt: (0,) * a.ndim)
    anyspec = pl.BlockSpec(memory_space=pl.ANY)
    ins = [q, pen, o_cmp, gd, ck, cv, ksn, vsn, cw, cvw, kwn, vwn, kwt, vwt, eall, tsel, tnew, twin, gexp]
    specs = [per_b(q), per_b(pen), per_b(o_cmp), per_b(gd), anyspec, anyspec, per_b(ksn), per_b(vsn),
             per_b(cw), per_b(cvw), per_b(kwn), per_b(vwn), per_b(kwt), per_b(vwt),
             full(eall), full(tsel), full(tnew), full(twin), full(gexp)]
    win = pl.BlockSpec((1, 128, WINDOW), lambda i, pt: (i, 0, 0))
    return pl.pallas_call(
        functools.partial(_sample_attn_kernel, n_pages=n_pages, n_new=n_new),
        grid_spec=pltpu.PrefetchScalarGridSpec(
            num_scalar_prefetch=1,
            grid=(bs,),
            in_specs=specs,
            out_specs=[pl.BlockSpec((1, TQS, 512), lambda i, pt: (i, 0, 0)), win, win],
            scratch_shapes=[pltpu.VMEM((2, 128, past), F32), pltpu.VMEM((2, 128, past), F32),
                            pltpu.VMEM((256, past), BF16),
                            pltpu.SemaphoreType.DMA((2,)), pltpu.SemaphoreType.DMA((2,))]),
        out_shape=[jax.ShapeDtypeStruct((bs, TQS, 512), BF16),
                   jax.ShapeDtypeStruct((bs, 128, WINDOW), F32), jax.ShapeDtypeStruct((bs, 128, WINDOW), F32)],
        compiler_params=_params(("arbitrary",)),
        name="sample_attn",
    )(page_table, *ins)


def _lane_pair(cols, h0, h1, rows):
    lane = lax.broadcasted_iota(jnp.int32, (rows, LANES), 1)
    a = jnp.broadcast_to(cols[:, h0:h0 + 1], (rows, LANES))
    b = jnp.broadcast_to(cols[:, h1:h1 + 1], (rows, LANES))
    return jnp.where(lane < HEAD_DIM, a, b)


def _ssd_kernel(xbc_ref, z_ref, gd_ref, carry_ref, h0_ref, cw_ref, cb_ref, dtb_ref, arow_ref, dskip_ref,
                gnorm_ref, ltri_ref, y_ref, hout_ref, h_s, xfull, *, rows, n_valid):
    c = pl.program_id(1)
    halo = 8

    @pl.when(c == 0)
    def _():
        h_s[...] = h0_ref[0]
        xfull[0:halo, :] = carry_ref[0]

    xfull[halo:halo + rows, :] = xbc_ref[0]
    conv = cb_ref[...]
    for k in range(SSD_CONV):
        conv = conv + xfull[pl.ds(halo - (SSD_CONV - 1) + k, rows), :] * cw_ref[k:k + 1, :]
    tail = xfull[rows:rows + halo, :]
    xfull[0:halo, :] = tail
    xc = _silu(conv)
    xs = xc[:, 0:D_SSD]
    bm = xc[:, D_SSD:D_SSD + 128]
    cm = xc[:, D_SSD + 128:D_SSD + 256]

    lane = lax.broadcasted_iota(jnp.int32, (rows, LANES), 1)
    rowi = lax.broadcasted_iota(jnp.int32, (rows, LANES), 0)
    t = gd_ref[0] + dtb_ref[...]
    sp = jnp.maximum(t, 0.0) + jnp.log(1.0 + jnp.exp(-jnp.abs(t)))
    dt = jnp.where((lane >= DT_LANE) & (lane < DT_LANE + SSD_HEADS) & (rowi < n_valid), sp, 0.0)
    a = dt * arow_ref[...]
    acum = _split_dot_left(ltri_ref[...], a)
    acum_t = acum.T
    a_last = acum[rows - 1:rows, :]
    to_end = jnp.exp(a_last - acum)
    eac = jnp.exp(acum)
    dec = jnp.exp(a_last)

    li = lax.broadcasted_iota(jnp.int32, (rows, rows), 0)
    si = lax.broadcasted_iota(jnp.int32, (rows, rows), 1)
    causal = li >= si
    bmb = bm.astype(BF16)
    rowp = lax.broadcasted_iota(jnp.int32, (LANES, LANES), 0)
    cbs = []
    cmask = []
    for g in range(2):
        cg = jnp.where((lane >= g * 64) & (lane < (g + 1) * 64), cm, 0.0).astype(BF16)
        cmask.append(cg)
        cbs.append(_dot_nt(cg, bmb))
    for k in range(SSD_HEADS // 2):
        g = k // 2
        h0, h1 = DT_LANE + 2 * k, DT_LANE + 2 * k + 1
        xs_p = xs[:, k * 128:(k + 1) * 128]
        xdt = xs_p * _lane_pair(dt, h0, h1, rows)
        xdt_b = xdt.astype(BF16)
        ys = []
        for h in (h0, h1):
            seg = jnp.broadcast_to(acum[:, h:h + 1], (rows, rows)) - acum_t[h:h + 1, :]
            decay = jnp.where(causal, jnp.exp(jnp.where(causal, seg, 0.0)), 0.0)
            ys.append(_dot((cbs[g] * decay).astype(BF16), xdt_b))
        y = jnp.where(lane < HEAD_DIM, ys[0], ys[1])
        hp = h_s[k * 128:(k + 1) * 128, :]
        y = y + _dot_nt(cmask[g], hp.astype(BF16)) * _lane_pair(eac, h0, h1, rows)
        y = y + dskip_ref[:, k * 128:(k + 1) * 128] * xs_p
        xw = xdt * _lane_pair(to_end, h0, h1, rows)
        st = _dot(xw.T.astype(BF16), bmb)
        dfac = jnp.where(rowp < HEAD_DIM, dec[:, h0:h0 + 1], dec[:, h1:h1 + 1])
        h_s[k * 128:(k + 1) * 128, :] = hp * dfac + st
        xfull_y = y * _silu(z_ref[0, :, k * 128:(k + 1) * 128])
        y_ref[0, :, k * 128:(k + 1) * 128] = xfull_y.astype(y_ref.dtype)

    for g in range(2):
        yg = y_ref[0, :, g * 256:(g + 1) * 256].astype(F32)
        ms = jnp.mean(yg * yg, axis=-1, keepdims=True)
        y_ref[0, :, g * 256:(g + 1) * 256] = (yg * lax.rsqrt(ms + RMS_EPS)
                                               * gnorm_ref[:, g * 256:(g + 1) * 256]).astype(y_ref.dtype)

    @pl.when(c == pl.num_programs(1) - 1)
    def _():
        half = D_SSD // 2
        hout_ref[0, 0:half, :] = h_s[0:half, 0:SSD_STATE]
        hout_ref[0, half:D_SSD, :] = h_s[half:D_SSD, SSD_STATE:2 * SSD_STATE]


def _split_dot_left(tri, a):
    acc = None
    rem = a
    for _ in range(3):
        piece = rem.astype(BF16)
        rem = rem - piece.astype(F32)
        t = _dot(tri, piece)
        acc = t if acc is None else acc + t
    return acc


def _ssd(xbc, z, gd, carry, h0, wts, *, n_valid):
    b, l, _ = xbc.shape
    rows = SSD_CHUNK
    nc = l // rows
    blk = lambda w: pl.BlockSpec((1, rows, w), lambda bi, ci: (bi, ci, 0))
    per_b = lambda a: pl.BlockSpec((1,) + a.shape[1:], lambda bi, ci: (bi,) + (0,) * (a.ndim - 1))
    full = lambda a: pl.BlockSpec(a.shape, lambda bi, ci: (0,) * a.ndim)
    consts = [wts["conv_w"], wts["conv_b"], wts["dtb"], wts["arow"], wts["dskip"], wts["gnorm"], wts["ltri"]]
    return pl.pallas_call(
        functools.partial(_ssd_kernel, rows=rows, n_valid=n_valid),
        grid=(b, nc),
        in_specs=[blk(768), blk(512), blk(128), per_b(carry), per_b(h0)] + [full(a) for a in consts],
        out_specs=[blk(512), pl.BlockSpec((1, D_SSD, SSD_STATE), lambda bi, ci: (bi, 0, 0))],
        out_shape=[jax.ShapeDtypeStruct((b, l, 512), F32), jax.ShapeDtypeStruct((b, D_SSD, SSD_STATE), F32)],
        scratch_shapes=[pltpu.VMEM((512, 128), F32), pltpu.VMEM((rows + 8, 768), F32)],
        compiler_params=_params(("parallel", "arbitrary")),
        name="ssd",
    )(xbc, z, gd, carry, h0, *consts)


def _finish_kernel(x_ref, oa_ref, ys_ref, gf_ref, woa_hbm, wos_hbm, wg_hbm, wu_hbm, wd_hbm, y_ref,
                   woa_ref, wos_ref, wg_ref, wu_ref, wd_ref, sem):
    @pl.when(pl.program_id(0) == 0)
    def _():
        copies = [pltpu.make_async_copy(src, dst, sem.at[i]) for i, (src, dst) in enumerate(
            ((woa_hbm, woa_ref), (wos_hbm, wos_ref), (wg_hbm, wg_ref), (wu_hbm, wu_ref), (wd_hbm, wd_ref)))]
        for c in copies:
            c.start()
        for c in copies:
            c.wait()

    h = x_ref[...] + _dot(oa_ref[...], woa_ref[...]) + _dot(ys_ref[...].astype(BF16), wos_ref[...])
    ms = jnp.mean(h * h, axis=-1, keepdims=True)
    u = (h * lax.rsqrt(ms + RMS_EPS) * gf_ref[...]).astype(BF16)
    act = _silu(_dot(u, wg_ref[...])) * _dot(u, wu_ref[...])
    y_ref[...] = h + _dot(act.astype(BF16), wd_ref[...])


def _finish(x2d, o_att, y_ssd, wts):
    n = x2d.shape[0]
    tm = min(512, n)
    row = lambda w: pl.BlockSpec((tm, w), lambda i: (i, 0))
    weights = [wts["wo_att"], wts["wo_ssd"], wts["w_gate"], wts["w_up"], wts["w_down"]]
    anyspec = pl.BlockSpec(memory_space=pl.ANY)
    return pl.pallas_call(
        _finish_kernel,
        grid=(n // tm,),
        in_specs=[row(D_MODEL), row(512), row(512), pl.BlockSpec((1, D_MODEL), lambda i: (0, 0))]
        + [anyspec] * len(weights),
        out_specs=row(D_MODEL),
        out_shape=jax.ShapeDtypeStruct((n, D_MODEL), F32),
        scratch_shapes=[pltpu.VMEM(w.shape, w.dtype) for w in weights] + [pltpu.SemaphoreType.DMA((len(weights),))],
        compiler_params=_params(("arbitrary",)),
        name="finish",
    )(x2d, o_att, y_ssd, wts["gffn"], *weights)


def _pair_perm():
    cols = []
    for r in range(ATT_REP):
        cols += list(range(r * HEAD_DIM, (r + 1) * HEAD_DIM))
        cols += list(range((ATT_REP + r) * HEAD_DIM, (ATT_REP + r + 1) * HEAD_DIM))
    return np.asarray(cols, np.int32)


def _block_ones(n, blk):
    i = np.arange(n)
    return (i[:, None] // blk == i[None, :] // blk).astype(np.float32) / blk


def _prep_weights(norm_mix, w_in, q_norm, k_norm, cmp_pe, cmp_w1, cmp_w2, conv_w, conv_b, dt_bias, a_log,
                  d_skip, ssd_norm, w_out, norm_ffn, w_gate, w_up, w_down):
    perm = _pair_perm()
    w = w_in
    gd = jnp.concatenate([w[:, OFF_GATE:OFF_Z], w[:, OFF_DT:D_IN],
                          jnp.zeros((D_MODEL, 128 - 3 * ATT_HEADS - SSD_HEADS), w.dtype)], axis=1)
    w_r = jnp.concatenate([w[:, :D_ATT][:, perm], w[:, OFF_KV:OFF_GATE], w[:, OFF_Z:OFF_XBC],
                           w[:, OFF_XBC:OFF_DT], gd], axis=1).astype(BF16)
    wts = dict(
        gmix=norm_mix.reshape(1, D_MODEL), w_in=w_r,
        qg=(jnp.tile(q_norm, ATT_HEADS) * (HEAD_DIM ** -0.5)).reshape(1, D_ATT),
        kg=jnp.tile(k_norm, ATT_KV_HEADS).reshape(1, D_KV),
        bq=jnp.asarray(_block_ones(D_ATT, HEAD_DIM), BF16), bk=jnp.asarray(_block_ones(D_KV, HEAD_DIM), BF16))

    def w1_big(w1):
        w1r = w1.reshape(2, CMP_STRIDE, HEAD_DIM, CMP_HIDDEN)
        eye = jnp.eye(ATT_KV_HEADS, dtype=w1.dtype)
        big = jnp.einsum("jsdh,gk->sgdjkh", w1r, eye)
        return big.reshape(CMP_STRIDE * D_KV, 2 * D_KV).astype(BF16)

    def pe_rows(pe):
        per = jnp.broadcast_to(pe.reshape(2, CMP_STRIDE, 1, HEAD_DIM), (2, CMP_STRIDE, ATT_KV_HEADS, HEAD_DIM))
        per = per.reshape(2, CMP_STRIDE * D_KV)
        return jnp.concatenate([per, jnp.zeros((6, CMP_STRIDE * D_KV), pe.dtype)], axis=0).astype(BF16)

    def w2_big(w2):
        eye = jnp.eye(ATT_KV_HEADS, dtype=w2.dtype)
        return jnp.einsum("hd,gk->ghkd", w2, eye).reshape(D_KV, D_KV).astype(BF16)

    wts.update(w1k=w1_big(cmp_w1[0]), w1v=w1_big(cmp_w1[1]), pek=pe_rows(cmp_pe[0]), pev=pe_rows(cmp_pe[1]),
               w2k=w2_big(cmp_w2[0]), w2v=w2_big(cmp_w2[1]))

    pad_lanes = lambda v: jnp.zeros((1, LANES), F32).at[0, DT_LANE:DT_LANE + SSD_HEADS].set(v)
    ltri = np.tril(np.ones((SSD_CHUNK, SSD_CHUNK), np.float32))
    wts.update(conv_w=jnp.concatenate([conv_w, jnp.zeros((4, D_CONV), F32)], axis=0), conv_b=conv_b.reshape(1, D_CONV),
               dtb=pad_lanes(dt_bias), arow=pad_lanes(-jnp.exp(a_log)),
               dskip=jnp.repeat(d_skip, 64).reshape(1, D_SSD), gnorm=ssd_norm.reshape(1, D_SSD),
               ltri=jnp.asarray(ltri, BF16))
    wts.update(wo_att=w_out[:D_ATT].astype(BF16), wo_ssd=w_out[D_ATT:].astype(BF16),
               gffn=norm_ffn.reshape(1, D_MODEL), w_gate=w_gate.astype(BF16), w_up=w_up.astype(BF16),
               w_down=w_down.astype(BF16))
    return wts


def _gate_expand():
    m = np.zeros((ATT_KV_HEADS, LANES, 3 * ATT_REP * HEAD_DIM), np.float32)
    for g in range(ATT_KV_HEADS):
        for r in range(ATT_REP):
            for br in range(3):
                c0 = br * ATT_REP * HEAD_DIM + r * HEAD_DIM
                m[g, g * 3 * ATT_REP + r * 3 + br, c0:c0 + HEAD_DIM] = 1.0
    return jnp.asarray(m, BF16)


def _sel_matrix(n_blk_pad, n_sel):
    m = np.zeros((n_blk_pad, LANES), np.float32)
    for n in range(n_blk_pad - 1):
        for j in {n // 4, (n + 1) // 4}:
            if j < n_sel:
                m[n, j] = 1.0
    return jnp.asarray(m, BF16)


def _override(q_pos, n_sel):
    j = np.arange(LANES)[None, :]
    cur = (q_pos // SEL_BLOCK)[:, None]
    forced = (j == 0) | (j == cur) | (j == cur - 1)
    ovr = np.where(forced, 1e30, np.where(j <= cur, 0.0, -1e30))
    ovr = np.where(j < n_sel, ovr, -1e30)
    return ovr.astype(np.float32)


def _prompt_tables(rel_bias, t):
    n_qt = t // TQ
    pos = np.arange(t)
    nb = t // CMP_STRIDE
    e = CMP_STRIDE * np.arange(nb) + (CMP_BLOCK - 1)
    dist = pos[None, :] - e[:, None]
    cmp_idx = _idx_table(dist, (dist >= 0) & (np.arange(nb)[:, None] < nb - 1))
    cmp_idx = cmp_idx.reshape(nb, n_qt, TQ).transpose(1, 0, 2)
    i = np.arange(TQ)[None, :]
    j = np.arange(TQ)[:, None]
    diag = _idx_table(i - j, i >= j)
    prev = _idx_table(TQ + i - j, np.ones((TQ, TQ), bool))
    prev2 = _idx_table(2 * TQ + i - j, (2 * TQ + i - j) < WINDOW)
    att_idx = np.stack([prev2, prev, diag])
    return _bias_tables(rel_bias, cmp_idx, stack_cols=True), _bias_tables(rel_bias, att_idx, stack_cols=True)


def _prompt_layer(x, wts, rel_bias):
    b, t, _ = x.shape
    n = b * t
    p = _project(x.reshape(n, D_MODEL), wts, t, "prompt")
    n_sub = t // CMP_STRIDE
    sub = lambda a: a.reshape(b, n_sub, CMP_STRIDE * D_KV)
    kcmp, vcmp = _compress_prompt(sub(p["kc_b"]), sub(p["vc_b"]), wts)
    cmp_tab, att_tab = _prompt_tables(rel_bias, t)
    n_sel = t // SEL_BLOCK
    ovr_t = jnp.asarray(_override(np.arange(t), n_sel).reshape(t // TQ, TQ, LANES).transpose(0, 2, 1))
    r3 = lambda a: a.reshape(b, t, a.shape[-1])
    o_cmp, pen_t = _cmp_select_t(p["q_t"], kcmp, vcmp, cmp_tab, _sel_matrix(n_sub, n_sel).T, ovr_t,
                                 tq=TQ, n_sel=n_sel, k_top=min(SEL_TOPN, n_sel))
    o_att = _prompt_attn(p["q_t"], pen_t, o_cmp, r3(p["gd"]), r3(p["ksa"]), p["vs_tb"], r3(p["kw_b"]), p["vw_tb"],
                         att_tab, _gate_expand(), tq=TQ)
    carry = jnp.zeros((b, 8, D_CONV), F32)
    h0 = jnp.zeros((b, 512, 128), F32)
    y_ssd, h_fin = _ssd(r3(p["xbc"]), r3(p["z"]), r3(p["gd"]), carry, h0, wts, n_valid=SSD_CHUNK)
    y = _finish(x.reshape(n, D_MODEL), o_att.reshape(n, 512), y_ssd.reshape(n, 512), wts)
    wb = min(WINDOW, t)
    kv_out = {k: _token_major(p[k + "_t"]) for k in KV_NAMES}
    ssm = h_fin.reshape(b, SSD_HEADS, 64, SSD_STATE)
    return y.reshape(b, t, D_MODEL), (kv_out["kc"], kv_out["vc"], kv_out["ks"], kv_out["vs"],
                                      kv_out["kw"][:, t - wb:], kv_out["vw"][:, t - wb:],
                                      r3(p["xbc"])[:, t - (SSD_CONV - 1):], ssm)


def _token_major(a_t):
    b, _, t = a_t.shape
    return jnp.swapaxes(a_t, 1, 2).reshape(b, t, ATT_KV_HEADS, HEAD_DIM)


def _feature_major(a):
    return jnp.swapaxes(a.reshape(a.shape[:-2] + (D_KV,)), -1, -2)


def _sample_tables(rel_bias, past, s_new, n_blk_pad):
    s = np.minimum(np.arange(TQS), s_new - 1)[:, None]
    pos = past + s
    nidx = np.arange(n_blk_pad)[None, :]
    e = CMP_STRIDE * nidx + (CMP_BLOCK - 1)
    cmp_idx = _idx_table(pos - e, (e <= pos) & (nidx < n_blk_pad - 1))[None]
    key = np.arange(past)[None, :]
    sel_idx = _idx_table(pos - key, np.ones((TQS, past), bool))[None]
    jn = np.arange(LANES)[None, :]
    new_idx = _idx_table(s - jn, (jn <= s) & (jn < s_new))[None]
    wi = np.arange(WINDOW)[None, :]
    wdist = pos - (past - WINDOW + wi)
    win_idx = _idx_table(wdist, (wdist >= 0) & (wdist < WINDOW))[None]
    tabs = [_bias_tables(rel_bias, t) for t in (cmp_idx, sel_idx, new_idx, win_idx)]
    return tabs, pos[:, 0]


def _sample_layer(x, c_kc, c_vc, c_ks, c_vs, c_kw, c_vw, s_conv, s_ssm, page_table, wts, rel_bias):
    bs, s_new, _ = x.shape
    n = bs * s_new
    n_pages = page_table.shape[1]
    past = n_pages * PAGE_SIZE
    p = _project(x.reshape(n, D_MODEL), wts, s_new, "sample")
    r3 = lambda a: a.reshape(bs, s_new, a.shape[-1])
    padq = lambda a, rows: jnp.pad(r3(a), ((0, 0), (0, rows - s_new), (0, 0)))
    new_t = lambda a_t: jnp.pad(jnp.swapaxes(a_t.reshape(D_KV, bs, s_new), 0, 1), ((0, 0), (0, 0), (LANES - s_new, 0)))
    kcmp, vcmp = _compress_sample(page_table, _feature_major(c_kc), _feature_major(c_vc), wts)
    n_blk_pad = past // CMP_STRIDE
    (cmp_tab, sel_tab, new_tab, win_tab), pos = _sample_tables(rel_bias, past, s_new, n_blk_pad)
    n_sel = past // SEL_BLOCK
    ovr = jnp.asarray(_override(pos, n_sel)[None])
    qp = padq(p["q"], TQS)
    o_cmp, pen = _cmp_select(qp, kcmp, vcmp, cmp_tab, _sel_matrix(n_blk_pad, n_sel), ovr,
                             tq=TQS, n_sel=n_sel, k_top=min(SEL_TOPN - 1, n_sel))
    blk_of_key = np.arange(past)[None, :] // SEL_BLOCK
    eall = jnp.asarray((np.arange(LANES)[:, None] == blk_of_key).astype(np.float32), BF16)
    o_att, kw_new, vw_new = _sample_attn(
        page_table, qp, pen, o_cmp, padq(p["gd"], TQS),
        _feature_major(c_ks), _feature_major(c_vs), padq(p["ks_b"], LANES), padq(p["vs_b"], LANES),
        _feature_major(c_kw), _feature_major(c_vw), padq(p["kw_b"], LANES), padq(p["vw_b"], LANES),
        new_t(p["kw_t"]), new_t(p["vw_t"]),
        eall, sel_tab, new_tab, win_tab, _gate_expand(), n_new=s_new)
    carry = jnp.pad(s_conv, ((0, 0), (8 - (SSD_CONV - 1), 0), (0, 0)))
    h0 = s_ssm.reshape(bs, 512, SSD_STATE)
    h0 = jnp.concatenate([h0, h0], axis=-1)
    y_ssd, h_fin = _ssd(padq(p["xbc"], SSD_CHUNK), padq(p["z"], SSD_CHUNK), padq(p["gd"], SSD_CHUNK), carry, h0, wts,
                        n_valid=s_new)
    y = _finish(x.reshape(n, D_MODEL), o_att[:, :s_new].reshape(n, 512), y_ssd[:, :s_new].reshape(n, 512), wts)
    kv4 = lambda a: a.reshape(bs, -1, ATT_KV_HEADS, HEAD_DIM)
    ssm = h_fin.reshape(bs, SSD_HEADS, 64, SSD_STATE)
    conv_state = jnp.concatenate([s_conv, r3(p["xbc"])], axis=1)[:, s_new:]
    return y.reshape(bs, s_new, D_MODEL), (kv4(p["kc"]), kv4(p["vc"]), kv4(p["ks"]), kv4(p["vs"]),
                                           _token_major(kw_new), _token_major(vw_new), conv_state, ssm)


def kernel(x_prompt, x_sample, cache_k_cmp, cache_v_cmp, cache_k_sel, cache_v_sel, cache_k_win, cache_v_win,
           state_conv, state_ssm, page_table, norm_mix, w_in, q_norm, k_norm, cmp_pe, cmp_w1, cmp_w2, rel_bias,
           conv_w, conv_b, dt_bias, a_log, d_skip, ssd_norm, w_out, norm_ffn, w_gate, w_up, w_down):
    depth = w_in.shape[0]
    y_p, y_s = x_prompt, x_sample
    p_states, s_states = [], []
    for l in range(depth):
        wts = _prep_weights(norm_mix[l], w_in[l], q_norm[l], k_norm[l], cmp_pe[l], cmp_w1[l], cmp_w2[l],
                            conv_w[l], conv_b[l], dt_bias[l], a_log[l], d_skip[l], ssd_norm[l], w_out[l],
                            norm_ffn[l], w_gate[l], w_up[l], w_down[l])
        y_p, st_p = _prompt_layer(y_p, wts, rel_bias)
        y_s, st_s = _sample_layer(y_s, cache_k_cmp[l], cache_v_cmp[l], cache_k_sel[l], cache_v_sel[l],
                                  cache_k_win[l], cache_v_win[l], state_conv[l], state_ssm[l], page_table,
                                  wts, rel_bias)
        p_states.append(st_p)
        s_states.append(st_s)
    p_out = [jnp.stack(a) for a in zip(*p_states)]
    s_out = [jnp.stack(a) for a in zip(*s_states)]
    return (y_p, y_s, *p_out, *s_out)
```
